```python
import math
import jax, jax.numpy as jnp
from jax import lax
import numpy as np

D_MODEL = 1024
BATCH = 8
SEQ = 2048
DEPTH = 2
DEC_BATCH = 32
DEC_SEQ = 4
PAST_LEN = 8192
PAGE_SIZE = 128

HEAD_DIM = 64
N_MIXERS = 2
N_A_LAYERS = (DEPTH + 1) // 2
N_B_LAYERS = DEPTH // 2
WINDOWS = (128, 512, 2048)
DILATIONS = (1, 4, 16)
N_GROUPS = 3
GROUP_HEADS = 4
A_HEADS = N_GROUPS * GROUP_HEADS
A_KEYS = WINDOWS[0] // DILATIONS[0] + 1
A_QKV = 3 * A_HEADS * HEAD_DIM
A_OUT = GROUP_HEADS * HEAD_DIM
Q_BLOCK = 128
N_MEM = 256
MEM_HEADS = 4
MEM_W = MEM_HEADS * HEAD_DIM
RWKV_W = 3 * D_MODEL // 4
RWKV_HEADS = RWKV_W // HEAD_DIM
LORA_DECAY = 64
LORA_A = 64
C_SHIFT = 3 * RWKV_W + LORA_DECAY + LORA_A
GN_EPS = 64e-5
A_GATE = A_OUT + MEM_W
A_IN = A_QKV + MEM_W + A_GATE
B_GATE = RWKV_W + MEM_W
B_IN = C_SHIFT + MEM_W + B_GATE
N_BUCKETS = 32
MAX_DISTANCE = WINDOWS[-1]
RMS_EPS = 1e-6
NEG_INF = -1e30
SCALE = HEAD_DIM ** -0.5

kernel_name = "hybrid_dilated_rwkv7_memory_decoder_step"


def rms_norm(x, g):
    xf = x.astype(jnp.float32)
    y = xf * lax.rsqrt(jnp.mean(xf * xf, axis=-1, keepdims=True) + RMS_EPS)
    return (y * g.astype(jnp.float32)).astype(x.dtype)


def t5_bucket(dist):
    max_exact = N_BUCKETS // 2
    d = jnp.maximum(dist, 1).astype(jnp.float32)
    large = max_exact + (jnp.log(d / max_exact) / math.log(MAX_DISTANCE / max_exact)
                         * (N_BUCKETS - max_exact)).astype(jnp.int32)
    large = jnp.minimum(large, N_BUCKETS - 1)
    return jnp.where(dist < max_exact, dist, large)


def group_bias(rel_bias, g):
    dist = DILATIONS[g] * jnp.arange(A_KEYS, dtype=jnp.int32)
    b = rel_bias[t5_bucket(dist)]
    return b[:, g * GROUP_HEADS:(g + 1) * GROUP_HEADS].T.astype(jnp.float32)


def memory_kv(mem, g, w):
    kv = rms_norm(mem, g) @ w
    return kv.reshape(mem.shape[0], mem.shape[1], 2, MEM_HEADS, HEAD_DIM)


def cross_attn(q, kv):
    logits = jnp.einsum('bthd,bmhd->bhtm', q.astype(jnp.float32), kv[:, :, 0].astype(jnp.float32)) * SCALE
    p = jax.nn.softmax(logits, axis=-1)
    o = jnp.einsum('bhtm,bmhd->bthd', p, kv[:, :, 1].astype(jnp.float32))
    return o.reshape(q.shape[0], q.shape[1], MEM_W).astype(q.dtype)


def dilated_group(q, ctx, q_idx, dil, bias):
    idx = q_idx[:, None] - dil * jnp.arange(A_KEYS, dtype=jnp.int32)[None, :]
    valid = idx >= 0
    kv = jnp.take(ctx, jnp.maximum(idx, 0), axis=1, mode='clip')
    logits = jnp.einsum('bthd,btkhd->bhtk', q.astype(jnp.float32), kv[:, :, :, 0].astype(jnp.float32)) * SCALE
    logits = jnp.where(valid[None, None], logits + bias[None, :, None, :], NEG_INF)
    m = jnp.max(logits, axis=-1)
    p = jnp.exp(logits - m[..., None])
    s = jnp.sum(p, axis=-1)
    o = jnp.einsum('bhtk,btkhd->bthd', p, kv[:, :, :, 1].astype(jnp.float32))
    o = o / s.transpose(0, 2, 1)[..., None]
    return o, m.transpose(0, 2, 1), s.transpose(0, 2, 1)


def mixer_a(xn, w_in, w_out, rel_bias, mem_kv, past):
    bsz, T, _ = xn.shape
    p = xn @ w_in
    qkv = p[..., :A_QKV].reshape(bsz, T, 3, A_HEADS, HEAD_DIM)
    q_mem = p[..., A_QKV:A_QKV + MEM_W].reshape(bsz, T, MEM_HEADS, HEAD_DIM)
    gate = p[..., A_QKV + MEM_W:]
    q = qkv[:, :, 0]
    kv = qkv[:, :, 1:]
    ctxs, offs, new_rows, biases = [], [], [], []
    for g in range(N_GROUPS):
        kv_g = kv[:, :, :, g * GROUP_HEADS:(g + 1) * GROUP_HEADS]
        if past is None:
            ctxs.append(kv_g)
            offs.append(0)
            new_rows.append(kv_g[:, T - min(WINDOWS[g], T):])
        else:
            ctxs.append(jnp.concatenate([past[g].astype(kv_g.dtype), kv_g], axis=1))
            offs.append(past[g].shape[1])
            new_rows.append(kv_g)
        biases.append(group_bias(rel_bias, g))

    def block(args):
        q_blk, t_blk = args
        os_, ms, ss = [], [], []
        for g in range(N_GROUPS):
            o, m, s = dilated_group(q_blk[:, :, g * GROUP_HEADS:(g + 1) * GROUP_HEADS], ctxs[g],
                                    offs[g] + t_blk, DILATIONS[g], biases[g])
            os_.append(o); ms.append(m); ss.append(s)
        ms = jnp.stack(ms); ss = jnp.stack(ss)
        wts = jnp.exp(ms - jnp.max(ms, axis=0, keepdims=True)) * ss
        return jnp.sum(wts[..., None] * jnp.stack(os_), axis=0) / jnp.sum(wts, axis=0)[..., None]

    nb = T // Q_BLOCK if T % Q_BLOCK == 0 else 1
    qb = T // nb
    q_blocks = q.reshape(bsz, nb, qb, A_HEADS, HEAD_DIM).swapaxes(0, 1)
    t_blocks = jnp.arange(T, dtype=jnp.int32).reshape(nb, qb)
    o = lax.map(block, (q_blocks, t_blocks)).swapaxes(0, 1).reshape(bsz, T, A_OUT).astype(xn.dtype)
    o_mem = cross_attn(q_mem, mem_kv)
    h = jnp.concatenate([o, o_mem], axis=-1) * jax.nn.silu(gate)
    return h @ w_out, new_rows


def wkv_scan(r, w, k, v, a, b, s0):
    seq = tuple(t.astype(jnp.float32).swapaxes(0, 1) for t in (r, w, k, v, a, b))

    def step(S, inp):
        r_t, w_t, k_t, v_t, a_t, b_t = inp
        sa = jnp.einsum('bhvk,bhk->bhv', S, a_t)
        S = S * w_t[:, :, None, :] + sa[..., None] * b_t[:, :, None, :] + v_t[..., None] * k_t[:, :, None, :]
        return S, jnp.einsum('bhvk,bhk->bhv', S, r_t)

    S, ys = lax.scan(step, s0.astype(jnp.float32), seq)
    return ys.swapaxes(0, 1), S


def mixer_b(xn, w_in, w_out, mu, w0, w_up, a0, a_up, k_k, k_a, r_k, ln_w, ln_b, mem_kv, shift_prev, s0):
    bsz, T, _ = xn.shape
    p = xn @ w_in
    cols = p[..., :C_SHIFT]
    q_mem = p[..., C_SHIFT:C_SHIFT + MEM_W].reshape(bsz, T, MEM_HEADS, HEAD_DIM)
    gate = p[..., C_SHIFT + MEM_W:]
    prev = jnp.concatenate([shift_prev[:, None].astype(cols.dtype), cols[:, :-1]], axis=1)
    xs = (cols + mu * (prev - cols)).astype(jnp.float32)
    r = xs[..., :RWKV_W]
    k = xs[..., RWKV_W:2 * RWKV_W]
    v = xs[..., 2 * RWKV_W:3 * RWKV_W]
    wd = xs[..., 3 * RWKV_W:3 * RWKV_W + LORA_DECAY]
    ad = xs[..., 3 * RWKV_W + LORA_DECAY:]
    w_log = -jax.nn.softplus(-(w0 + jnp.tanh(wd) @ w_up)) - 0.5
    decay = jnp.exp(-jnp.exp(w_log))
    a = jax.nn.sigmoid(a0 + ad @ a_up)
    hs = (bsz, T, RWKV_HEADS, HEAD_DIM)
    r, k, v, decay, a = (t.reshape(hs) for t in (r, k, v, decay, a))
    kk = k * k_k.reshape(RWKV_HEADS, HEAD_DIM)
    kk = kk / jnp.maximum(jnp.sqrt(jnp.sum(kk * kk, axis=-1, keepdims=True)), 1e-12)
    k = k * (1.0 + (a - 1.0) * k_a.reshape(RWKV_HEADS, HEAD_DIM))
    y, S = wkv_scan(r, decay, k, v, -kk, kk * a, s0)
    mean = jnp.mean(y, axis=-1, keepdims=True)
    var = jnp.mean(jnp.square(y - mean), axis=-1, keepdims=True)
    y = (y - mean) * lax.rsqrt(var + GN_EPS) * ln_w.reshape(RWKV_HEADS, HEAD_DIM) + ln_b.reshape(RWKV_HEADS, HEAD_DIM)
    y = y + jnp.sum(r * k * r_k, axis=-1, keepdims=True) * v
    y = y.reshape(bsz, T, RWKV_W).astype(xn.dtype)
    o_mem = cross_attn(q_mem, mem_kv)
    h = jnp.concatenate([y, o_mem], axis=-1) * jax.nn.silu(gate)
    return h @ w_out, cols[:, -1], S.astype(xn.dtype)


def setup_inputs(seed: int = 0) -> dict:
    key = jax.random.key(seed)
    ks = list(jax.random.split(key, 40))

    def nrm(shape, scale):
        return scale * jax.random.normal(ks.pop(), shape, jnp.float32)

    def unif(shape, lo, hi):
        return jax.random.uniform(ks.pop(), shape, jnp.float32, lo, hi)

    return {
        "x_prompt": nrm((BATCH, SEQ, D_MODEL), 1.0),
        "x_sample": nrm((DEC_BATCH, DEC_SEQ, D_MODEL), 1.0),
        "mem_prompt": nrm((BATCH, N_MEM, D_MODEL), 1.0),
        "cache_mem_kv": nrm((DEPTH, DEC_BATCH, N_MEM, 2, MEM_HEADS, HEAD_DIM), 1.0),
        "cache_win0": nrm((N_A_LAYERS, DEC_BATCH, min(WINDOWS[0], PAST_LEN), 2, GROUP_HEADS, HEAD_DIM), 1.0),
        "cache_win1": nrm((N_A_LAYERS, DEC_BATCH, min(WINDOWS[1], PAST_LEN), 2, GROUP_HEADS, HEAD_DIM), 1.0),
        "cache_win2": nrm((N_A_LAYERS, DEC_BATCH, min(WINDOWS[2], PAST_LEN), 2, GROUP_HEADS, HEAD_DIM), 1.0),
        "state_wkv": nrm((N_B_LAYERS, DEC_BATCH, RWKV_HEADS, HEAD_DIM, HEAD_DIM), 1.0),
        "state_shift": nrm((N_B_LAYERS, DEC_BATCH, C_SHIFT), 1.0),
        "norm_pre": 1.0 + nrm((DEPTH, D_MODEL), 0.05),
        "norm_post": 1.0 + nrm((DEPTH, D_MODEL), 0.05),
        "norm_mem": 1.0 + nrm((DEPTH, D_MODEL), 0.05),
        "w_mem_kv": nrm((DEPTH, D_MODEL, 2 * MEM_W), D_MODEL ** -0.5),
        "rel_bias": nrm((N_BUCKETS, A_HEADS), 0.5),
        "w_in_a": nrm((N_A_LAYERS, D_MODEL, A_IN), D_MODEL ** -0.5),
        "w_out_a": nrm((N_A_LAYERS, A_GATE, D_MODEL), A_GATE ** -0.5),
        "w_in_b": nrm((N_B_LAYERS, D_MODEL, B_IN), D_MODEL ** -0.5),
        "w_out_b": nrm((N_B_LAYERS, B_GATE, D_MODEL), B_GATE ** -0.5),
        "rwkv_mu": unif((N_B_LAYERS, C_SHIFT), 0.0, 1.0),
        "rwkv_w0": unif((N_B_LAYERS, RWKV_W), -4.0, 1.0),
        "rwkv_w_up": nrm((N_B_LAYERS, LORA_DECAY, RWKV_W), 0.5 * LORA_DECAY ** -0.5),
        "rwkv_a0": nrm((N_B_LAYERS, RWKV_W), 0.5),
        "rwkv_a_up": nrm((N_B_LAYERS, LORA_A, RWKV_W), LORA_A ** -0.5),
        "rwkv_k_k": 0.85 + nrm((N_B_LAYERS, RWKV_W), 0.05),
        "rwkv_k_a": 1.0 + nrm((N_B_LAYERS, RWKV_W), 0.05),
        "rwkv_r_k": nrm((N_B_LAYERS, RWKV_HEADS, HEAD_DIM), 0.1),
        "rwkv_ln_w": 1.0 + nrm((N_B_LAYERS, RWKV_W), 0.05),
        "rwkv_ln_b": nrm((N_B_LAYERS, RWKV_W), 0.02),
    }


def reference(x_prompt, x_sample, mem_prompt, cache_mem_kv, cache_win0, cache_win1, cache_win2, state_wkv,
              state_shift, norm_pre, norm_post, norm_mem, w_mem_kv, rel_bias, w_in_a, w_out_a, w_in_b, w_out_b,
              rwkv_mu, rwkv_w0, rwkv_w_up, rwkv_a0, rwkv_a_up, rwkv_k_k, rwkv_k_a, rwkv_r_k, rwkv_ln_w, rwkv_ln_b):
    xp, xs = x_prompt, x_sample
    mem_new = []
    win_p = [[], [], []]
    win_s = [[], [], []]
    wkv_p, wkv_s, sh_p, sh_s = [], [], [], []
    for i in range(DEPTH):
        j = i // N_MIXERS
        mkv_p = memory_kv(mem_prompt, norm_mem[i], w_mem_kv[i])
        mem_new.append(mkv_p)
        mkv_s = cache_mem_kv[i]
        xn_p = rms_norm(xp, norm_pre[i])
        xn_s = rms_norm(xs, norm_pre[i])
        if i % N_MIXERS == 0:
            hp, rows_p = mixer_a(xn_p, w_in_a[j], w_out_a[j], rel_bias, mkv_p, None)
            hs, rows_s = mixer_a(xn_s, w_in_a[j], w_out_a[j], rel_bias, mkv_s,
                                 (cache_win0[j], cache_win1[j], cache_win2[j]))
            for g in range(N_GROUPS):
                win_p[g].append(rows_p[g])
                win_s[g].append(rows_s[g])
        else:
            rw = (rwkv_mu[j], rwkv_w0[j], rwkv_w_up[j], rwkv_a0[j], rwkv_a_up[j], rwkv_k_k[j], rwkv_k_a[j],
                  rwkv_r_k[j], rwkv_ln_w[j], rwkv_ln_b[j])
            shift0 = jnp.zeros((xp.shape[0], C_SHIFT), xp.dtype)
            s0 = jnp.zeros((xp.shape[0], RWKV_HEADS, HEAD_DIM, HEAD_DIM), jnp.float32)
            hp, shp, Sp = mixer_b(xn_p, w_in_b[j], w_out_b[j], *rw, mkv_p, shift0, s0)
            hs, shs, Ss = mixer_b(xn_s, w_in_b[j], w_out_b[j], *rw, mkv_s, state_shift[j], state_wkv[j])
            wkv_p.append(Sp); wkv_s.append(Ss); sh_p.append(shp); sh_s.append(shs)
        xp = xp + rms_norm(hp, norm_post[i])
        xs = xs + rms_norm(hs, norm_post[i])
    new_mem_kv = jnp.stack(mem_new)
    win0_p, win1_p, win2_p = (jnp.stack(w) for w in win_p)
    win0_s, win1_s, win2_s = (jnp.stack(w) for w in win_s)
    wkv_prompt = jnp.stack(wkv_p)
    wkv_sample = jnp.stack(wkv_s)
    shift_prompt = jnp.stack(sh_p)
    shift_sample = jnp.stack(sh_s)
    return (xp, xs, new_mem_kv, win0_p, win1_p, win2_p, win0_s, win1_s, win2_s,
            wkv_prompt, wkv_sample, shift_prompt, shift_sample)
```

```python
import functools
import math

import numpy as np
import jax
import jax.numpy as jnp
from jax import lax
from jax.experimental import pallas as pl
from jax.experimental.pallas import tpu as pltpu

F32 = jnp.float32
BF16 = jnp.bfloat16

D_MODEL = 1024
HEAD_DIM = 64
N_GROUPS = 3
GROUP_HEADS = 4
WINDOWS = (128, 512, 2048)
DILATIONS = (1, 4, 16)
A_HEADS = N_GROUPS * GROUP_HEADS
A_KEYS = 129
GROUP_W = GROUP_HEADS * HEAD_DIM
A_Q = A_HEADS * HEAD_DIM
N_MEM = 256
MEM_W = 256
RWKV_W = 768
RWKV_PAIRS = RWKV_W // 128
LORA = 64
C_SHIFT = 3 * RWKV_W + 2 * LORA
N_BUCKETS = 32
MAX_DISTANCE = WINDOWS[-1]
RMS_EPS = 1e-6
GN_EPS = 64e-5
NEG_INF = -1e30
SCALE = HEAD_DIM ** -0.5

Q_TILE = 128
SCAN_CHUNK = 64
SAMPLE_PAD_T = 8
VMEM_LIMIT = 56 * 1024 * 1024
SCAN_PARTS = 1
NN = (((1,), (0,)), ((), ()))
NT = (((1,), (1,)), ((), ()))
TN = (((0,), (0,)), ((), ()))


def _params(*sem):
    return pltpu.CompilerParams(dimension_semantics=sem, vmem_limit_bytes=VMEM_LIMIT)


def _dot(a, b):
    return jnp.dot(a, b, preferred_element_type=F32)


def _dot_nt(a, b):
    return lax.dot_general(a, b, NT, preferred_element_type=F32)


def _split(x, n):
    if x.dtype == BF16:
        return [x]
    parts, rem = [], x
    for i in range(n):
        parts.append(rem.astype(BF16))
        if i + 1 < n:
            rem = rem - parts[-1].astype(F32)
    return parts


def _mm(a, b, dims=NN, pa=1, pb=1):
    pas, pbs = _split(a, pa), _split(b, pb)
    order = max(len(pas), len(pbs))
    out = None
    for i, ai in enumerate(pas):
        for j, bj in enumerate(pbs):
            if i + j < order:
                term = lax.dot_general(ai, bj, dims, preferred_element_type=F32)
                out = term if out is None else out + term
    return out


def _rms(x, g):
    return x * lax.rsqrt(jnp.mean(x * x, axis=-1, keepdims=True) + RMS_EPS) * g


def _norm_proj_kernel(x_ref, g_ref, w_ref, *out_refs, pieces):
    xn = _rms(x_ref[...], g_ref[...]).astype(BF16)
    for out_ref, cols in zip(out_refs, pieces):
        off = 0
        for c0, c1 in cols:
            out_ref[:, off:off + c1 - c0] = _dot(xn, w_ref[:, c0:c1])
            off += c1 - c0


def _norm_proj(x, g, w, pieces, tm):
    m, d = x.shape
    n = w.shape[1]
    widths = [sum(c1 - c0 for c0, c1 in cols) for cols in pieces]
    return pl.pallas_call(
        functools.partial(_norm_proj_kernel, pieces=pieces),
        grid=(m // tm,),
        in_specs=[pl.BlockSpec((tm, d), lambda i: (i, 0)),
                  pl.BlockSpec((1, d), lambda i: (0, 0)),
                  pl.BlockSpec((d, n), lambda i: (0, 0))],
        out_specs=[pl.BlockSpec((tm, wd), lambda i: (i, 0)) for wd in widths],
        out_shape=[jax.ShapeDtypeStruct((m, wd), F32) for wd in widths],
        compiler_params=_params("parallel"),
        name="norm_proj",
    )(x, g.reshape(1, d), w.astype(BF16))


def _dil_attn_kernel(q_ref, kvc_ref, kvp_ref, bd_ref, bp_ref, acc_ref, st_ref, *, has_prev):
    i = pl.program_id(2)
    q = (q_ref[...] * SCALE).astype(BF16)
    kvc = kvc_ref[...].astype(BF16)
    if has_prev:
        kvp = kvp_ref[...].astype(BF16)
    lane = lax.broadcasted_iota(jnp.int32, (Q_TILE, 128), 1)
    st = jnp.zeros((Q_TILE, 128), F32)
    for h in range(GROUP_HEADS):
        ks, vs = slice(h * HEAD_DIM, (h + 1) * HEAD_DIM), slice(GROUP_W + h * HEAD_DIM, GROUP_W + (h + 1) * HEAD_DIM)
        qh = q[:, ks]
        lc = _dot_nt(qh, kvc[:, ks]) + bd_ref[h]
        m = jnp.max(lc, axis=-1, keepdims=True)
        if has_prev:
            lp = jnp.where(i > 0, _dot_nt(qh, kvp[:, ks]) + bp_ref[h], NEG_INF)
            m = jnp.maximum(m, jnp.max(lp, axis=-1, keepdims=True))
        pc = jnp.exp(lc - m)
        s = jnp.sum(pc, axis=-1, keepdims=True)
        o = _dot(pc.astype(BF16), kvc[:, vs])
        if has_prev:
            pp = jnp.exp(lp - m)
            s = s + jnp.sum(pp, axis=-1, keepdims=True)
            o = o + _dot(pp.astype(BF16), kvp[:, vs])
        acc_ref[:, ks] = o
        st = jnp.where(lane == h, m, st)
        st = jnp.where(lane == GROUP_HEADS + h, s, st)
    st_ref[...] = st


def _dil_attn(q, kv, bias_diag, bias_prev, dil):
    b, t, _ = q.shape
    ln = t // dil
    nt = ln // Q_TILE
    qv = q.reshape(b, ln, dil * GROUP_W)
    kvv = kv.reshape(b, ln, dil * 2 * GROUP_W)
    acc, st = pl.pallas_call(
        functools.partial(_dil_attn_kernel, has_prev=nt > 1),
        grid=(b, dil, nt),
        in_specs=[pl.BlockSpec((None, Q_TILE, GROUP_W), lambda bi, r, i: (bi, i, r)),
                  pl.BlockSpec((None, Q_TILE, 2 * GROUP_W), lambda bi, r, i: (bi, i, r)),
                  pl.BlockSpec((None, Q_TILE, 2 * GROUP_W), lambda bi, r, i: (bi, jnp.maximum(i - 1, 0), r)),
                  pl.BlockSpec((GROUP_HEADS, Q_TILE, Q_TILE), lambda bi, r, i: (0, 0, 0)),
                  pl.BlockSpec((GROUP_HEADS, Q_TILE, Q_TILE), lambda bi, r, i: (0, 0, 0))],
        out_specs=[pl.BlockSpec((None, Q_TILE, GROUP_W), lambda bi, r, i: (bi, i, r)),
                   pl.BlockSpec((None, Q_TILE, 128), lambda bi, r, i: (bi, i, r))],
        out_shape=[jax.ShapeDtypeStruct((b, ln, dil * GROUP_W), F32),
                   jax.ShapeDtypeStruct((b, ln, dil * 128), F32)],
        compiler_params=_params("parallel", "parallel", "arbitrary"),
        name=f"dil_attn_d{dil}",
    )(qv, kvv, kvv, bias_diag, bias_prev)
    return acc.reshape(b, t, GROUP_W), st.reshape(b, t, 128)


def _sample_attn_kernel(q_ref, kn0_ref, kn1_ref, kn2_ref, p0_ref, p1a, p1b, p1c, p1d, p2a, p2b, p2c, p2d,
                        bpast_ref, bnew_ref, o_ref, knew, *, n_new):
    tp = q_ref.shape[0]
    knew[...] = jnp.zeros(knew.shape, F32)
    for g, kn_ref in enumerate((kn0_ref, kn1_ref, kn2_ref)):
        knew[g, 0:tp, :] = kn_ref[...]
    past = ((p0_ref,) * n_new, (p1a, p1b, p1c, p1d), (p2a, p2b, p2c, p2d))
    sub = lax.broadcasted_iota(jnp.int32, (8, GROUP_W), 0)
    lane_head = lax.broadcasted_iota(jnp.int32, (8, GROUP_W), 1) // HEAD_DIM
    diag = sub == lane_head
    for t in range(n_new):
        ms, ss, os_ = [], [], []
        for g in range(N_GROUPS):
            qrow = q_ref[t:t + 1, g * GROUP_W:(g + 1) * GROUP_W] * SCALE
            qbd = jnp.where(diag, jnp.broadcast_to(qrow, (8, GROUP_W)), 0.0).astype(BF16)
            kvp = past[g][t][...].astype(BF16)
            kvn = knew[g].astype(BF16)
            lp = _dot_nt(qbd, kvp[:, :GROUP_W]) + bpast_ref[g, t]
            ln = _dot_nt(qbd, kvn[:, :GROUP_W]) + bnew_ref[g, t]
            m = jnp.maximum(jnp.max(lp, axis=-1, keepdims=True), jnp.max(ln, axis=-1, keepdims=True))
            pp = jnp.exp(lp - m)
            pn = jnp.exp(ln - m)
            s = jnp.sum(pp, axis=-1, keepdims=True) + jnp.sum(pn, axis=-1, keepdims=True)
            o = _dot(pp.astype(BF16), kvp[:, GROUP_W:]) + _dot(pn.astype(BF16), kvn[:, GROUP_W:])
            ms.append(m)
            ss.append(s)
            os_.append(o / s)
        mx = jnp.maximum(jnp.maximum(ms[0], ms[1]), ms[2])
        wts = [jnp.exp(ms[g] - mx) * ss[g] for g in range(N_GROUPS)]
        num = wts[0] * os_[0] + wts[1] * os_[1] + wts[2] * os_[2]
        full = num / (wts[0] + wts[1] + wts[2])
        o_ref[t:t + 1, :] = jnp.sum(jnp.where(diag, full, 0.0), axis=0, keepdims=True)
    if n_new < tp:
        o_ref[n_new:tp, :] = jnp.zeros((tp - n_new, GROUP_W), F32)


def _sample_attn(q, kvn, caches, bias_past, bias_new, n_new):
    b, tp, _ = q.shape
    c0 = caches[0].reshape(b, WINDOWS[0], 2 * GROUP_W)
    c1 = caches[1].reshape(b, 128, DILATIONS[1] * 2 * GROUP_W)
    c2 = caches[2].reshape(b, 128, DILATIONS[2] * 2 * GROUP_W)
    past_spec = lambda r: pl.BlockSpec((None, 128, 2 * GROUP_W), lambda bi: (bi, 0, r))
    new_spec = pl.BlockSpec((None, tp, 2 * GROUP_W), lambda bi: (bi, 0, 0))
    return pl.pallas_call(
        functools.partial(_sample_attn_kernel, n_new=n_new),
        grid=(b,),
        in_specs=[pl.BlockSpec((None, tp, A_Q), lambda bi: (bi, 0, 0)), new_spec, new_spec, new_spec,
                  past_spec(0)] + [past_spec(r) for r in range(n_new)] * 2 +
                 [pl.BlockSpec((N_GROUPS, n_new, 8, 128), lambda bi: (0, 0, 0, 0)),
                  pl.BlockSpec((N_GROUPS, n_new, 8, 128), lambda bi: (0, 0, 0, 0))],
        out_specs=pl.BlockSpec((None, tp, GROUP_W), lambda bi: (bi, 0, 0)),
        out_shape=jax.ShapeDtypeStruct((b, tp, GROUP_W), F32),
        scratch_shapes=[pltpu.VMEM((N_GROUPS, 128, 2 * GROUP_W), F32)],
        compiler_params=_params("parallel"),
        name="sample_attn",
    )(q, kvn[0], kvn[1], kvn[2], c0, c1, c1, c1, c1, c2, c2, c2, c2, bias_past, bias_new)


def _cross_attn_kernel(q_ref, kv_ref, o_ref):
    q = (q_ref[...] * SCALE).astype(BF16)
    kv = kv_ref[...].astype(BF16)
    for h in range(MEM_W // HEAD_DIM):
        ks, vs = slice(h * HEAD_DIM, (h + 1) * HEAD_DIM), slice(MEM_W + h * HEAD_DIM, MEM_W + (h + 1) * HEAD_DIM)
        lg = _dot_nt(q[:, ks], kv[:, ks])
        p = jnp.exp(lg - jnp.max(lg, axis=-1, keepdims=True))
        o = _dot(p.astype(BF16), kv[:, vs])
        o_ref[:, ks] = o / jnp.sum(p, axis=-1, keepdims=True)


def _cross_attn(q, kv, tq):
    b, t, _ = q.shape
    return pl.pallas_call(
        _cross_attn_kernel,
        grid=(b, t // tq),
        in_specs=[pl.BlockSpec((None, tq, MEM_W), lambda bi, i: (bi, i, 0)),
                  pl.BlockSpec((None, N_MEM, 2 * MEM_W), lambda bi, i: (bi, 0, 0))],
        out_specs=pl.BlockSpec((None, tq, MEM_W), lambda bi, i: (bi, i, 0)),
        out_shape=jax.ShapeDtypeStruct((b, t, MEM_W), F32),
        compiler_params=_params("parallel", "parallel"),
        name="cross_attn",
    )(q, kv)


def _rwkv_kernel(cols_ref, shift_ref, s0_ref, mu_ref, w0_ref, wup_ref, a0_ref, aup_ref, kk_ref, ka_ref, rk_ref,
                 lnw_ref, lnb_ref, y_ref, sout_ref, shout_ref,
                 carry, state, r_s, k_s, v_s, a_s, b_s, lw_s, *, tt, chunk, t_valid):
    j = pl.program_id(1)

    @pl.when(j == 0)
    def _():
        carry[...] = shift_ref[...]
        state[...] = s0_ref[...]

    cols = cols_ref[...]
    row = lax.broadcasted_iota(jnp.int32, (tt, 1), 0)
    prev = jnp.where(row == 0, carry[...], pltpu.roll(cols, 1, axis=0))
    carry[...] = cols[t_valid - 1:t_valid, :]
    xs = cols + mu_ref[...] * (prev - cols)
    r = xs[:, :RWKV_W]
    k = xs[:, RWKV_W:2 * RWKV_W]
    v = xs[:, 2 * RWKV_W:3 * RWKV_W]
    wd = xs[:, 3 * RWKV_W:3 * RWKV_W + LORA]
    ad = xs[:, 3 * RWKV_W + LORA:]
    w_log = -jax.nn.softplus(-(w0_ref[...] + _dot(jnp.tanh(wd).astype(BF16), wup_ref[...]))) - 0.5
    log_decay = -jnp.exp(w_log)
    a = jax.nn.sigmoid(a0_ref[...] + _dot(ad.astype(BF16), aup_ref[...]))

    blockdiag = (lax.broadcasted_iota(jnp.int32, (128, 128), 0) // HEAD_DIM
                 == lax.broadcasted_iota(jnp.int32, (128, 128), 1) // HEAD_DIM)
    seg = blockdiag.astype(BF16)

    kk = k * kk_ref[...]
    kk_sq = kk * kk
    k2 = k * (1.0 + (a - 1.0) * ka_ref[...])
    live = row < t_valid
    for p in range(RWKV_PAIRS):
        ps = slice(p * 128, (p + 1) * 128)
        nrm = jnp.maximum(jnp.sqrt(_mm(kk_sq[:, ps], seg, pa=3)), 1e-12)
        kkn = kk[:, ps] / nrm
        if t_valid < tt:
            zero = jnp.zeros((tt, 128), F32)
            r_s[:, ps] = r[:, ps]
            k_s[:, ps] = jnp.where(live, k2[:, ps], zero)
            v_s[:, ps] = jnp.where(live, v[:, ps], zero)
            a_s[:, ps] = jnp.where(live, -kkn, zero)
            b_s[:, ps] = jnp.where(live, kkn * a[:, ps], zero)
            lw_s[:, ps] = jnp.where(live, log_decay[:, ps], zero)
        else:
            r_s[:, ps] = r[:, ps]
            k_s[:, ps] = k2[:, ps]
            v_s[:, ps] = v[:, ps]
            a_s[:, ps] = -kkn
            b_s[:, ps] = kkn * a[:, ps]
            lw_s[:, ps] = log_decay[:, ps]

    ci = lax.broadcasted_iota(jnp.int32, (chunk, chunk), 0)
    cj = lax.broadcasted_iota(jnp.int32, (chunk, chunk), 1)
    tri_incl = (ci >= cj).astype(BF16)
    lower = ci > cj
    lower_incl = ci >= cj
    lo_lane = lax.broadcasted_iota(jnp.int32, (chunk, 128), 1) < HEAD_DIM
    levels = int(math.log2(chunk))
    sp = SCAN_PARTS
    prep = (lambda z: z.astype(BF16)) if sp == 1 else (lambda z: z)

    for c in range(tt // chunk):
        rows = slice(c * chunk, (c + 1) * chunk)
        for p in range(RWKV_PAIRS):
            ps = slice(p * 128, (p + 1) * 128)
            lw = lw_s[rows, ps]
            cum = _mm(tri_incl, lw, pb=3)
            p_incl = jnp.exp(cum)
            p_inv = jnp.exp(-cum)
            rr, kc, vc = r_s[rows, ps], k_s[rows, ps], v_s[rows, ps]
            at = a_s[rows, ps] * jnp.exp(cum - lw)
            rt = rr * p_incl
            bt = prep(b_s[rows, ps] * p_inv)
            kt = prep(kc * p_inv)
            vm = prep(vc)
            s_prev = state[p]
            s_mm = prep(s_prev)
            a_st = _mm(prep(at), s_mm, NT, sp, sp)
            r_st = _mm(prep(rt), s_mm, NT, sp, sp)
            us, ys = [], []
            for head in range(2):
                hm = lo_lane if head == 0 else jnp.logical_not(lo_lane)
                atm = prep(jnp.where(hm, at, 0.0))
                rtm = prep(jnp.where(hm, rt, 0.0))
                a_ab = jnp.where(lower, _mm(atm, bt, NT, sp, sp), 0.0)
                a_ak = jnp.where(lower, _mm(atm, kt, NT, sp, sp), 0.0)
                a_rb = jnp.where(lower_incl, _mm(rtm, bt, NT, sp, sp), 0.0)
                a_rk = jnp.where(lower_incl, _mm(rtm, kt, NT, sp, sp), 0.0)
                x = a_st + _mm(prep(a_ak), vm, NN, sp, sp)
                apow = a_ab
                for lvl in range(levels):
                    am = prep(apow)
                    x = x + _mm(am, prep(x), NN, sp, sp)
                    if lvl + 1 < levels:
                        apow = _mm(am, am, NN, sp, sp)
                us.append(x)
                ys.append(r_st + _mm(prep(a_rb), prep(x), NN, sp, sp) + _mm(prep(a_rk), vm, NN, sp, sp))
            u = jnp.where(lo_lane, us[0], us[1])
            y = jnp.where(lo_lane, ys[0], ys[1])
            upd = _mm(prep(u), bt, TN, sp, sp) + _mm(vm, kt, TN, sp, sp)
            state[p] = jnp.where(blockdiag, (s_prev + upd) * p_incl[chunk - 1:chunk, :], 0.0)

            mean = _mm(y, seg, pa=3) * (1.0 / HEAD_DIM)
            dlt = y - mean
            var = _mm(dlt * dlt, seg, pa=3) * (1.0 / HEAD_DIM)
            yn = dlt * lax.rsqrt(var + GN_EPS) * lnw_ref[:, ps] + lnb_ref[:, ps]
            bonus = _mm(rr * kc * rk_ref[:, ps], seg, pa=3)
            y_ref[rows, ps] = yn + bonus * vc

    @pl.when(j == pl.num_programs(1) - 1)
    def _():
        sout_ref[...] = state[...]
        shout_ref[...] = carry[...]


def _rwkv_scan(cols, shift_prev, s0, prm, tt, t_valid):
    b, t, _ = cols.shape
    s = s0.reshape(b, RWKV_PAIRS, 2, HEAD_DIM, HEAD_DIM)
    z = jnp.zeros_like(s[:, :, 0])
    s_bd = jnp.concatenate([jnp.concatenate([s[:, :, 0], z], -1), jnp.concatenate([z, s[:, :, 1]], -1)], -2)
    row = lambda n: pl.BlockSpec((1, n), lambda bi, j: (0, 0))
    vec = lambda a: a.reshape(1, -1)
    y, s_out, sh_out = pl.pallas_call(
        functools.partial(_rwkv_kernel, tt=tt, chunk=SCAN_CHUNK, t_valid=min(t_valid, tt)),
        grid=(b, t // tt),
        in_specs=[pl.BlockSpec((None, tt, C_SHIFT), lambda bi, j: (bi, j, 0)),
                  pl.BlockSpec((None, 1, C_SHIFT), lambda bi, j: (bi, 0, 0)),
                  pl.BlockSpec((None, RWKV_PAIRS, 128, 128), lambda bi, j: (bi, 0, 0, 0)),
                  row(C_SHIFT), row(RWKV_W),
                  pl.BlockSpec((LORA, RWKV_W), lambda bi, j: (0, 0)),
                  row(RWKV_W),
                  pl.BlockSpec((LORA, RWKV_W), lambda bi, j: (0, 0)),
                  row(RWKV_W), row(RWKV_W), row(RWKV_W), row(RWKV_W), row(RWKV_W)],
        out_specs=[pl.BlockSpec((None, tt, RWKV_W), lambda bi, j: (bi, j, 0)),
                   pl.BlockSpec((None, RWKV_PAIRS, 128, 128), lambda bi, j: (bi, 0, 0, 0)),
                   pl.BlockSpec((None, 1, C_SHIFT), lambda bi, j: (bi, 0, 0))],
        out_shape=[jax.ShapeDtypeStruct((b, t, RWKV_W), F32),
                   jax.ShapeDtypeStruct((b, RWKV_PAIRS, 128, 128), F32),
                   jax.ShapeDtypeStruct((b, 1, C_SHIFT), F32)],
        scratch_shapes=[pltpu.VMEM((1, C_SHIFT), F32), pltpu.VMEM((RWKV_PAIRS, 128, 128), F32)]
                       + [pltpu.VMEM((tt, RWKV_W), F32)] * 6,
        compiler_params=_params("parallel", "arbitrary"),
        name="rwkv_scan",
    )(cols, shift_prev.reshape(b, 1, C_SHIFT), s_bd, vec(prm["mu"]), vec(prm["w0"]), prm["w_up"].astype(BF16),
      vec(prm["a0"]), prm["a_up"].astype(BF16), vec(prm["k_k"]), vec(prm["k_a"]), vec(prm["r_k"]),
      vec(prm["ln_w"]), vec(prm["ln_b"]))
    s_fin = jnp.stack([s_out[:, :, :HEAD_DIM, :HEAD_DIM], s_out[:, :, HEAD_DIM:, HEAD_DIM:]], axis=2)
    return y, s_fin.reshape(b, 2 * RWKV_PAIRS, HEAD_DIM, HEAD_DIM), sh_out.reshape(b, C_SHIFT)


def _post_kernel(*refs, n_merge, mix_w):
    x_ref, omem_ref, gate_ref, w_ref, g_ref = refs[:5]
    mix_refs = refs[5:-1]
    y_ref = refs[-1]
    if n_merge:
        er = lax.broadcasted_iota(jnp.int32, (128, 2 * GROUP_W), 0)
        ec = lax.broadcasted_iota(jnp.int32, (128, 2 * GROUP_W), 1)
        expand = (er == ec // HEAD_DIM).astype(BF16)
        ms, ss, os_ = [], [], []
        for g in range(n_merge):
            st = _mm(mix_refs[2 * g + 1][...], expand, pa=3)
            ms.append(st[:, :GROUP_W])
            ss.append(st[:, GROUP_W:])
            os_.append(mix_refs[2 * g][...] / st[:, GROUP_W:])
        mx = functools.reduce(jnp.maximum, ms)
        wts = [jnp.exp(m - mx) * s for m, s in zip(ms, ss)]
        mix = sum(w * o for w, o in zip(wts, os_)) / sum(wts)
    else:
        mix = mix_refs[0][...]
    gate = gate_ref[...]
    act = gate * jax.nn.sigmoid(gate)
    h1 = (mix * act[:, :mix_w]).astype(BF16)
    h2 = (omem_ref[...] * act[:, mix_w:]).astype(BF16)
    out = _dot(h1, w_ref[:mix_w, :]) + _dot(h2, w_ref[mix_w:, :])
    y_ref[...] = x_ref[...] + _rms(out, g_ref[...])


def _post(x, mix, o_mem, gate, w_out, g_post, tm):
    m, d = x.shape
    merge = isinstance(mix, (list, tuple))
    mix_arrays = [a for pair in mix for a in pair] if merge else [mix]
    mix_w = GROUP_W if merge else mix.shape[1]
    gw = gate.shape[1]
    rows = lambda wd: pl.BlockSpec((tm, wd), lambda i: (i, 0))
    return pl.pallas_call(
        functools.partial(_post_kernel, n_merge=len(mix) if merge else 0, mix_w=mix_w),
        grid=(m // tm,),
        in_specs=[rows(d), rows(MEM_W), rows(gw),
                  pl.BlockSpec((gw, d), lambda i: (0, 0)),
                  pl.BlockSpec((1, d), lambda i: (0, 0))] + [rows(a.shape[1]) for a in mix_arrays],
        out_specs=rows(d),
        out_shape=jax.ShapeDtypeStruct((m, d), F32),
        compiler_params=_params("parallel"),
        name="post",
    )(x, o_mem, gate, w_out.astype(BF16), g_post.reshape(1, d), *mix_arrays)


def _t5_bucket(dist):
    max_exact = N_BUCKETS // 2
    d = jnp.maximum(dist, 1).astype(F32)
    large = max_exact + (jnp.log(d / max_exact) / math.log(MAX_DISTANCE / max_exact)
                         * (N_BUCKETS - max_exact)).astype(jnp.int32)
    large = jnp.minimum(large, N_BUCKETS - 1)
    return jnp.where(dist < max_exact, dist, large)


def _group_bias(rel_bias, g):
    dist = DILATIONS[g] * jnp.arange(A_KEYS, dtype=jnp.int32)
    bias = rel_bias[_t5_bucket(dist)]
    return bias[:, g * GROUP_HEADS:(g + 1) * GROUP_HEADS].T.astype(F32)


def _prompt_bias(bias):
    jq = np.arange(Q_TILE)[:, None]
    jk = np.arange(Q_TILE)[None, :]
    diag = jnp.where(jnp.asarray(jq >= jk), bias[:, np.clip(jq - jk, 0, A_KEYS - 1)], NEG_INF)
    prev = jnp.where(jnp.asarray(jk >= jq), bias[:, np.clip(jq + Q_TILE - jk, 0, A_KEYS - 1)], NEG_INF)
    return diag, prev


def _sample_bias(biases, n_new):
    past, new = [], []
    key = np.arange(128)
    for g, bias in enumerate(biases):
        pg, ng = [], []
        for t in range(n_new):
            if g == 0:
                k_past = 128 + t - key
                bp = jnp.where(jnp.asarray(key >= t), bias[:, np.clip(k_past, 0, A_KEYS - 1)], NEG_INF)
                bn = jnp.where(jnp.asarray(key <= t), bias[:, np.clip(t - key, 0, A_KEYS - 1)], NEG_INF)
            else:
                bp = bias[:, 128 - key]
                bn = jnp.where(jnp.asarray(key == t), bias[:, np.zeros(128, np.int64)], NEG_INF)
            pad = jnp.zeros((8 - GROUP_HEADS, 128), F32)
            pg.append(jnp.concatenate([bp, pad], 0))
            ng.append(jnp.concatenate([bn, pad + NEG_INF], 0))
        past.append(jnp.stack(pg))
        new.append(jnp.stack(ng))
    return jnp.stack(past), jnp.stack(new)


def _pieces_a():
    q = [((g * GROUP_W, (g + 1) * GROUP_W),) for g in range(N_GROUPS)]
    kv = [((A_Q + g * GROUP_W, A_Q + (g + 1) * GROUP_W), (2 * A_Q + g * GROUP_W, 2 * A_Q + (g + 1) * GROUP_W))
          for g in range(N_GROUPS)]
    qmem = ((3 * A_Q, 3 * A_Q + MEM_W),)
    gate = ((3 * A_Q + MEM_W, 3 * A_Q + MEM_W + GROUP_W + MEM_W),)
    return tuple(q + kv + [qmem, gate])


def _pieces_b():
    return (((0, C_SHIFT),), ((C_SHIFT, C_SHIFT + MEM_W),), ((C_SHIFT + MEM_W, C_SHIFT + MEM_W + RWKV_W + MEM_W),))


def kernel(x_prompt, x_sample, mem_prompt, cache_mem_kv, cache_win0, cache_win1, cache_win2, state_wkv, state_shift, norm_pre, norm_post, norm_mem, w_mem_kv, rel_bias, w_in_a, w_out_a, w_in_b, w_out_b, rwkv_mu, rwkv_w0, rwkv_w_up, rwkv_a0, rwkv_a_up, rwkv_k_k, rwkv_k_a, rwkv_r_k, rwkv_ln_w, rwkv_ln_b):
    bp, tp, d = x_prompt.shape
    bs, ts, _ = x_sample.shape
    tsp = SAMPLE_PAD_T
    xp = x_prompt.reshape(bp * tp, d)
    xs = jnp.pad(x_sample, ((0, 0), (0, tsp - ts), (0, 0))).reshape(bs * tsp, d)
    mem2d = mem_prompt.reshape(bp * N_MEM, d)
    tm_p, tm_s = 512, bs * tsp

    (mkv0,) = _norm_proj(mem2d, norm_mem[0], w_mem_kv[0], (((0, 2 * MEM_W),),), 512)
    biases = [_group_bias(rel_bias, g) for g in range(N_GROUPS)]
    outs_p = _norm_proj(xp, norm_pre[0], w_in_a[0], _pieces_a(), tm_p)
    outs_s = _norm_proj(xs, norm_pre[0], w_in_a[0], _pieces_a(), tm_s)
    q_p, kv_p, qmem_p, gate_p = outs_p[0:3], outs_p[3:6], outs_p[6], outs_p[7]
    q_s, kv_s, qmem_s, gate_s = outs_s[0:3], outs_s[3:6], outs_s[6], outs_s[7]

    merged = []
    for g in range(N_GROUPS):
        bdiag, bprev = _prompt_bias(biases[g])
        merged.append(_dil_attn(q_p[g].reshape(bp, tp, GROUP_W), kv_p[g].reshape(bp, tp, 2 * GROUP_W),
                                bdiag, bprev, DILATIONS[g]))
    merged = [(a.reshape(bp * tp, GROUP_W), s.reshape(bp * tp, 128)) for a, s in merged]
    omem_p = _cross_attn(qmem_p.reshape(bp, tp, MEM_W), mkv0.reshape(bp, N_MEM, 2 * MEM_W), 512)
    xp1 = _post(xp, merged, omem_p.reshape(bp * tp, MEM_W), gate_p, w_out_a[0], norm_post[0], tm_p)

    bias_past, bias_new = _sample_bias(biases, ts)
    q_s_all = jnp.concatenate(q_s, axis=-1).reshape(bs, tsp, A_Q)
    kvn = [a.reshape(bs, tsp, 2 * GROUP_W) for a in kv_s]
    o_s = _sample_attn(q_s_all, kvn, (cache_win0[0], cache_win1[0], cache_win2[0]), bias_past, bias_new, ts)
    omem_s = _cross_attn(qmem_s.reshape(bs, tsp, MEM_W), cache_mem_kv[0].reshape(bs, N_MEM, 2 * MEM_W), tsp)
    xs1 = _post(xs, o_s.reshape(bs * tsp, GROUP_W), omem_s.reshape(bs * tsp, MEM_W), gate_s, w_out_a[0],
                norm_post[0], tm_s)

    (mkv1,) = _norm_proj(mem2d, norm_mem[1], w_mem_kv[1], (((0, 2 * MEM_W),),), 512)
    cols_p, qmem_p, gate_p = _norm_proj(xp1, norm_pre[1], w_in_b[0], _pieces_b(), tm_p)
    cols_s, qmem_s, gate_s = _norm_proj(xs1, norm_pre[1], w_in_b[0], _pieces_b(), tm_s)
    prm = dict(mu=rwkv_mu[0], w0=rwkv_w0[0], w_up=rwkv_w_up[0], a0=rwkv_a0[0], a_up=rwkv_a_up[0], k_k=rwkv_k_k[0],
               k_a=rwkv_k_a[0], r_k=rwkv_r_k[0], ln_w=rwkv_ln_w[0], ln_b=rwkv_ln_b[0])
    y_p, wkv_p, sh_p = _rwkv_scan(cols_p.reshape(bp, tp, C_SHIFT), jnp.zeros((bp, C_SHIFT), F32),
                                  jnp.zeros((bp, 2 * RWKV_PAIRS, HEAD_DIM, HEAD_DIM), F32), prm, 256, tp)
    cols_s_pad = jnp.pad(cols_s.reshape(bs, tsp, C_SHIFT), ((0, 0), (0, SCAN_CHUNK - tsp), (0, 0)))
    y_s, wkv_s, sh_s = _rwkv_scan(cols_s_pad, state_shift[0], state_wkv[0], prm, SCAN_CHUNK, ts)
    y_s = y_s[:, :tsp].reshape(bs * tsp, RWKV_W)
    omem_p = _cross_attn(qmem_p.reshape(bp, tp, MEM_W), mkv1.reshape(bp, N_MEM, 2 * MEM_W), 512)
    omem_s = _cross_attn(qmem_s.reshape(bs, tsp, MEM_W), cache_mem_kv[1].reshape(bs, N_MEM, 2 * MEM_W), tsp)
    xp2 = _post(xp1, y_p.reshape(bp * tp, RWKV_W), omem_p.reshape(bp * tp, MEM_W), gate_p, w_out_b[0],
                norm_post[1], tm_p)
    xs2 = _post(xs1, y_s, omem_s.reshape(bs * tsp, MEM_W), gate_s, w_out_b[0], norm_post[1], tm_s)

    kv_shape = (2, GROUP_HEADS, HEAD_DIM)
    new_mem_kv = jnp.stack([mkv0, mkv1]).reshape(2, bp, N_MEM, 2, MEM_W // HEAD_DIM, HEAD_DIM)
    win_p = [kv_p[g].reshape(bp, tp, *kv_shape)[None, :, tp - min(WINDOWS[g], tp):] for g in range(N_GROUPS)]
    win_s = [kv_s[g].reshape(bs, tsp, *kv_shape)[None, :, :ts] for g in range(N_GROUPS)]
    return (xp2.reshape(bp, tp, d), xs2.reshape(bs, tsp, d)[:, :ts], new_mem_kv,
            win_p[0], win_p[1], win_p[2], win_s[0], win_s[1], win_s[2],
            wkv_p[None], wkv_s[None], sh_p[None], sh_s[None])
```

```python
import functools
import math

import numpy as np
import jax
import jax.numpy as jnp
from jax import lax
from jax.experimental import pallas as pl
from jax.experimental.pallas import tpu as pltpu

F32 = jnp.float32
BF16 = jnp.bfloat16

D_MODEL = 1024
HEAD_DIM = 64
N_GROUPS = 3
GROUP_HEADS = 4
WINDOWS = (128, 512, 2048)
DILATIONS = (1, 4, 16)
A_HEADS = N_GROUPS * GROUP_HEADS
A_KEYS = 129
GROUP_W = GROUP_HEADS * HEAD_DIM
A_Q = A_HEADS * HEAD_DIM
N_MEM = 256
MEM_W = 256
RWKV_W = 768
RWKV_PAIRS = RWKV_W // 128
LORA = 64
C_SHIFT = 3 * RWKV_W + 2 * LORA
N_BUCKETS = 32
MAX_DISTANCE = WINDOWS[-1]
RMS_EPS = 1e-6
GN_EPS = 64e-5
NEG_INF = -1e30
SCALE = HEAD_DIM ** -0.5

Q_TILE = 128
SCAN_CHUNK = 64
SAMPLE_PAD_T = 8
VMEM_LIMIT = 56 * 1024 * 1024
NN =(((1,), (0,)), ((), ()))
NT = (((1,), (1,)), ((), ()))
TN = (((0,), (0,)), ((), ()))


def _params(*sem):
    return pltpu.CompilerParams(dimension_semantics=sem, vmem_limit_bytes=VMEM_LIMIT)


def _dot(a, b):
    return jnp.dot(a, b, preferred_element_type=F32)


def _dot_nt(a, b):
    return lax.dot_general(a, b, NT, preferred_element_type=F32)


def _split(x, n):
    if x.dtype == BF16:
        return [x]
    parts, rem = [], x
    for i in range(n):
        parts.append(rem.astype(BF16))
        if i + 1 < n:
            rem = rem - parts[-1].astype(F32)
    return parts


def _mm(a, b, dims=NN, pa=1, pb=1):
    pas, pbs = _split(a, pa), _split(b, pb)
    order = max(len(pas), len(pbs))
    out = None
    for i, ai in enumerate(pas):
        for j, bj in enumerate(pbs):
            if i + j < order:
                term = lax.dot_general(ai, bj, dims, preferred_element_type=F32)
                out = term if out is None else out + term
    return out


def _rms(x, g):
    return x * lax.rsqrt(jnp.mean(x * x, axis=-1, keepdims=True) + RMS_EPS) * g


def _norm_proj_kernel(x_ref, g_ref, w_ref, *out_refs, pieces):
    xn = _rms(x_ref[...], g_ref[...]).astype(BF16)
    for out_ref, cols in zip(out_refs, pieces):
        off = 0
        for c0, c1 in cols:
            out_ref[:, off:off + c1 - c0] = _dot(xn, w_ref[:, c0:c1])
            off += c1 - c0


def _norm_proj(x, g, w, pieces, tm):
    m, d = x.shape
    n = w.shape[1]
    widths = [sum(c1 - c0 for c0, c1 in cols) for cols in pieces]
    return pl.pallas_call(
        functools.partial(_norm_proj_kernel, pieces=pieces),
        grid=(m // tm,),
        in_specs=[pl.BlockSpec((tm, d), lambda i: (i, 0)),
                  pl.BlockSpec((1, d), lambda i: (0, 0)),
                  pl.BlockSpec((d, n), lambda i: (0, 0))],
        out_specs=[pl.BlockSpec((tm, wd), lambda i: (i, 0)) for wd in widths],
        out_shape=[jax.ShapeDtypeStruct((m, wd), F32) for wd in widths],
        compiler_params=_params("parallel"),
        name="norm_proj",
    )(x, g.reshape(1, d), w.astype(BF16))


def _dil_attn_kernel(q_ref, kvc_ref, kvp_ref, bias_ref, acc_ref, st_ref, *, has_prev):
    i = pl.program_id(2)
    q = (q_ref[...] * SCALE).astype(BF16)
    kvc = kvc_ref[...].astype(BF16)
    if has_prev:
        kvp = kvp_ref[...].astype(BF16)
    lane = lax.broadcasted_iota(jnp.int32, (Q_TILE, 128), 1)
    st = jnp.zeros((Q_TILE, 128), F32)
    for h in range(GROUP_HEADS):
        ks, vs = slice(h * HEAD_DIM, (h + 1) * HEAD_DIM), slice(GROUP_W + h * HEAD_DIM, GROUP_W + (h + 1) * HEAD_DIM)
        qh = q[:, ks]
        lc = _dot_nt(qh, kvc[:, ks]) + bias_ref[h, :, :Q_TILE]
        m = jnp.max(lc, axis=-1, keepdims=True)
        if has_prev:
            lp = jnp.where(i > 0, _dot_nt(qh, kvp[:, ks]) + bias_ref[h, :, Q_TILE:], NEG_INF)
            m = jnp.maximum(m, jnp.max(lp, axis=-1, keepdims=True))
        pc = jnp.exp(lc - m)
        s = jnp.sum(pc, axis=-1, keepdims=True)
        o = _dot(pc.astype(BF16), kvc[:, vs])
        if has_prev:
            pp = jnp.exp(lp - m)
            s = s + jnp.sum(pp, axis=-1, keepdims=True)
            o = o + _dot(pp.astype(BF16), kvp[:, vs])
        acc_ref[:, ks] = o
        st = jnp.where(lane == h, m, st)
        st = jnp.where(lane == GROUP_HEADS + h, s, st)
    st_ref[...] = st


def _dil_attn(q, kv, bias_tiles, g):
    dil = DILATIONS[g]
    b, t, _ = q.shape
    ln = t // dil
    nt = ln // Q_TILE
    qv = q.reshape(b, ln, dil * GROUP_W)
    kvv = kv.reshape(b, ln, dil * 2 * GROUP_W)
    acc, st = pl.pallas_call(
        functools.partial(_dil_attn_kernel, has_prev=nt > 1),
        grid=(b, dil, nt),
        in_specs=[pl.BlockSpec((None, Q_TILE, GROUP_W), lambda bi, r, i: (bi, i, r)),
                  pl.BlockSpec((None, Q_TILE, 2 * GROUP_W), lambda bi, r, i: (bi, i, r)),
                  pl.BlockSpec((None, Q_TILE, 2 * GROUP_W), lambda bi, r, i: (bi, jnp.maximum(i - 1, 0), r)),
                  pl.BlockSpec((GROUP_HEADS, Q_TILE, 2 * Q_TILE), lambda bi, r, i: (g, 0, 0))],
        out_specs=[pl.BlockSpec((None, Q_TILE, GROUP_W), lambda bi, r, i: (bi, i, r)),
                   pl.BlockSpec((None, Q_TILE, 128), lambda bi, r, i: (bi, i, r))],
        out_shape=[jax.ShapeDtypeStruct((b, ln, dil * GROUP_W), F32),
                   jax.ShapeDtypeStruct((b, ln, dil * 128), F32)],
        compiler_params=_params("parallel", "parallel", "arbitrary"),
        name=f"dil_attn_d{dil}",
    )(qv, kvv, kvv, bias_tiles)
    return acc.reshape(b, t, GROUP_W), st.reshape(b, t, 128)


def _sample_attn_kernel(q_ref, kn0_ref, kn1_ref, kn2_ref, p0_ref, p1a, p1b, p1c, p1d, p2a, p2b, p2c, p2d,
                        bpast_ref, bnew_ref, o_ref, knew, *, n_new):
    tp = q_ref.shape[0]
    knew[...] = jnp.zeros(knew.shape, F32)
    for g, kn_ref in enumerate((kn0_ref, kn1_ref, kn2_ref)):
        knew[g, 0:tp, :] = kn_ref[...]
    past = ((p0_ref,) * n_new, (p1a, p1b, p1c, p1d), (p2a, p2b, p2c, p2d))
    sub = lax.broadcasted_iota(jnp.int32, (8, GROUP_W), 0)
    lane_head = lax.broadcasted_iota(jnp.int32, (8, GROUP_W), 1) // HEAD_DIM
    diag = sub == lane_head
    for t in range(n_new):
        ms, ss, os_ = [], [], []
        for g in range(N_GROUPS):
            qrow = q_ref[t:t + 1, g * GROUP_W:(g + 1) * GROUP_W] * SCALE
            qbd = jnp.where(diag, jnp.broadcast_to(qrow, (8, GROUP_W)), 0.0).astype(BF16)
            kvp = past[g][t][...].astype(BF16)
            kvn = knew[g].astype(BF16)
            lp = _dot_nt(qbd, kvp[:, :GROUP_W]) + bpast_ref[g, t]
            ln = _dot_nt(qbd, kvn[:, :GROUP_W]) + bnew_ref[g, t]
            m = jnp.maximum(jnp.max(lp, axis=-1, keepdims=True), jnp.max(ln, axis=-1, keepdims=True))
            pp = jnp.exp(lp - m)
            pn = jnp.exp(ln - m)
            s = jnp.sum(pp, axis=-1, keepdims=True) + jnp.sum(pn, axis=-1, keepdims=True)
            o = _dot(pp.astype(BF16), kvp[:, GROUP_W:]) + _dot(pn.astype(BF16), kvn[:, GROUP_W:])
            ms.append(m)
            ss.append(s)
            os_.append(o / s)
        mx = jnp.maximum(jnp.maximum(ms[0], ms[1]), ms[2])
        wts = [jnp.exp(ms[g] - mx) * ss[g] for g in range(N_GROUPS)]
        num = wts[0] * os_[0] + wts[1] * os_[1] + wts[2] * os_[2]
        full = num / (wts[0] + wts[1] + wts[2])
        o_ref[t:t + 1, :] = jnp.sum(jnp.where(diag, full, 0.0), axis=0, keepdims=True)
    if n_new < tp:
        o_ref[n_new:tp, :] = jnp.zeros((tp - n_new, GROUP_W), F32)


def _sample_attn(q, kvn, caches, bias_past, bias_new, n_new):
    b, tp, _ = q.shape
    c0 = caches[0].reshape(b, WINDOWS[0], 2 * GROUP_W)
    c1 = caches[1].reshape(b, 128, DILATIONS[1] * 2 * GROUP_W)
    c2 = caches[2].reshape(b, 128, DILATIONS[2] * 2 * GROUP_W)
    past_spec = lambda r: pl.BlockSpec((None, 128, 2 * GROUP_W), lambda bi: (bi, 0, r))
    new_spec = pl.BlockSpec((None, tp, 2 * GROUP_W), lambda bi: (bi, 0, 0))
    return pl.pallas_call(
        functools.partial(_sample_attn_kernel, n_new=n_new),
        grid=(b,),
        in_specs=[pl.BlockSpec((None, tp, A_Q), lambda bi: (bi, 0, 0)), new_spec, new_spec, new_spec,
                  past_spec(0)] + [past_spec(r) for r in range(n_new)] * 2 +
                 [pl.BlockSpec((N_GROUPS, n_new, 8, 128), lambda bi: (0, 0, 0, 0)),
                  pl.BlockSpec((N_GROUPS, n_new, 8, 128), lambda bi: (0, 0, 0, 0))],
        out_specs=pl.BlockSpec((None, tp, GROUP_W), lambda bi: (bi, 0, 0)),
        out_shape=jax.ShapeDtypeStruct((b, tp, GROUP_W), F32),
        scratch_shapes=[pltpu.VMEM((N_GROUPS, 128, 2 * GROUP_W), F32)],
        compiler_params=_params("parallel"),
        name="sample_attn",
    )(q, kvn[0], kvn[1], kvn[2], c0, c1, c1, c1, c1, c2, c2, c2, c2, bias_past, bias_new)


def _cross_attn_kernel(q_ref, kv_ref, o_ref):
    q = (q_ref[...] * SCALE).astype(BF16)
    kv = kv_ref[...].astype(BF16)
    for h in range(MEM_W // HEAD_DIM):
        ks, vs = slice(h * HEAD_DIM, (h + 1) * HEAD_DIM), slice(MEM_W + h * HEAD_DIM, MEM_W + (h + 1) * HEAD_DIM)
        lg = _dot_nt(q[:, ks], kv[:, ks])
        p = jnp.exp(lg - jnp.max(lg, axis=-1, keepdims=True))
        o = _dot(p.astype(BF16), kv[:, vs])
        o_ref[:, ks] = o / jnp.sum(p, axis=-1, keepdims=True)


def _cross_attn(q, kv, tq):
    b, t, _ = q.shape
    return pl.pallas_call(
        _cross_attn_kernel,
        grid=(b, t // tq),
        in_specs=[pl.BlockSpec((None, tq, MEM_W), lambda bi, i: (bi, i, 0)),
                  pl.BlockSpec((None, N_MEM, 2 * MEM_W), lambda bi, i: (bi, 0, 0))],
        out_specs=pl.BlockSpec((None, tq, MEM_W), lambda bi, i: (bi, i, 0)),
        out_shape=jax.ShapeDtypeStruct((b, t, MEM_W), F32),
        compiler_params=_params("parallel", "parallel"),
        name="cross_attn",
    )(q, kv)


def _rwkv_kernel(cols_ref, shift_ref, s0_ref, mu_ref, w0_ref, wup_ref, a0_ref, aup_ref, kk_ref, ka_ref, rk_ref,
                 lnw_ref, lnb_ref, y_ref, sout_ref, shout_ref,
                 carry, state, r_s, k_s, v_s, a_s, b_s, lw_s, *, tt, chunk, t_valid):
    j = pl.program_id(1)

    @pl.when(j == 0)
    def _():
        carry[...] = shift_ref[...]
        state[...] = s0_ref[...]

    cols = cols_ref[...]
    row = lax.broadcasted_iota(jnp.int32, (tt, 1), 0)
    prev = jnp.where(row == 0, carry[...], pltpu.roll(cols, 1, axis=0))
    carry[...] = cols[t_valid - 1:t_valid, :]
    xs = cols + mu_ref[...] * (prev - cols)
    r = xs[:, :RWKV_W]
    k = xs[:, RWKV_W:2 * RWKV_W]
    v = xs[:, 2 * RWKV_W:3 * RWKV_W]
    wd = xs[:, 3 * RWKV_W:3 * RWKV_W + LORA]
    ad = xs[:, 3 * RWKV_W + LORA:]
    w_log = -jax.nn.softplus(-(w0_ref[...] + _dot(jnp.tanh(wd).astype(BF16), wup_ref[...]))) - 0.5
    log_decay = -jnp.exp(w_log)
    a = jax.nn.sigmoid(a0_ref[...] + _dot(ad.astype(BF16), aup_ref[...]))

    blockdiag = (lax.broadcasted_iota(jnp.int32, (128, 128), 0) // HEAD_DIM
                 == lax.broadcasted_iota(jnp.int32, (128, 128), 1) // HEAD_DIM)
    seg = blockdiag.astype(BF16)

    kk = k * kk_ref[...]
    kk_sq = kk * kk
    k2 = k * (1.0 + (a - 1.0) * ka_ref[...])
    live = row < t_valid
    for p in range(RWKV_PAIRS):
        ps = slice(p * 128, (p + 1) * 128)
        nrm = jnp.maximum(jnp.sqrt(_mm(kk_sq[:, ps], seg, pa=2)), 1e-12)
        kkn = kk[:, ps] / nrm
        if t_valid < tt:
            zero = jnp.zeros((tt, 128), F32)
            r_s[:, ps] = r[:, ps]
            k_s[:, ps] = jnp.where(live, k2[:, ps], zero)
            v_s[:, ps] = jnp.where(live, v[:, ps], zero)
            a_s[:, ps] = jnp.where(live, -kkn, zero)
            b_s[:, ps] = jnp.where(live, kkn * a[:, ps], zero)
            lw_s[:, ps] = jnp.where(live, log_decay[:, ps], zero)
        else:
            r_s[:, ps] = r[:, ps]
            k_s[:, ps] = k2[:, ps]
            v_s[:, ps] = v[:, ps]
            a_s[:, ps] = -kkn
            b_s[:, ps] = kkn * a[:, ps]
            lw_s[:, ps] = log_decay[:, ps]

    ci = lax.broadcasted_iota(jnp.int32, (chunk, chunk), 0)
    cj = lax.broadcasted_iota(jnp.int32, (chunk, chunk), 1)
    tri_incl = (ci >= cj).astype(BF16)
    lo_lane = lax.broadcasted_iota(jnp.int32, (chunk, 128), 1) < HEAD_DIM
    levels = int(math.log2(chunk))
    n2 = 2 * chunk
    ri = lax.broadcasted_iota(jnp.int32, (n2, n2), 0)
    rj = lax.broadcasted_iota(jnp.int32, (n2, n2), 1)
    same = ri // chunk == rj // chunk
    strict = jnp.logical_and(same, ri > rj)
    incl = jnp.logical_and(same, ri >= rj)
    own = lax.broadcasted_iota(jnp.int32, (n2, 128), 0) // chunk == lax.broadcasted_iota(jnp.int32, (n2, 128), 1) // HEAD_DIM
    dup = lambda z: jnp.concatenate([z, z], axis=0)
    cat = jnp.concatenate

    def chunk_body(c, _):
        rows = pl.ds(pl.multiple_of(c * chunk, chunk), chunk)
        cum_all = _mm(tri_incl, lw_s[rows, :], pb=3)
        pairs = range(RWKV_PAIRS)
        sl = [slice(p * 128, (p + 1) * 128) for p in pairs]
        cum = [cum_all[:, s] for s in sl]
        p_incl = [jnp.exp(z) for z in cum]
        p_inv = [jnp.exp(-z) for z in cum]
        rr = [r_s[rows, s] for s in sl]
        kc = [k_s[rows, s] for s in sl]
        vc = [v_s[rows, s] for s in sl]
        at2 = [jnp.where(own, dup(a_s[rows, sl[p]] * jnp.exp(cum[p] - lw_s[rows, sl[p]])), 0.0) for p in pairs]
        rt2 = [jnp.where(own, dup(rr[p] * p_incl[p]), 0.0) for p in pairs]
        bt = [(b_s[rows, sl[p]] * p_inv[p]).astype(BF16) for p in pairs]
        kt = [(kc[p] * p_inv[p]).astype(BF16) for p in pairs]
        vb = [z.astype(BF16) for z in vc]
        v2 = [dup(z) for z in vb]
        g = [_mm(cat([at2[p], rt2[p]], 0).astype(BF16), cat([dup(bt[p]), dup(kt[p])], 0), NT) for p in pairs]
        a_ak = [jnp.where(strict, z[:n2, n2:], 0.0).astype(BF16) for z in g]
        apow = [jnp.where(strict, z[:n2, :n2], 0.0).astype(BF16) for z in g]
        a_r = [cat([jnp.where(incl, z[n2:, :n2], 0.0), jnp.where(incl, z[n2:, n2:], 0.0)], 1).astype(BF16)
               for z in g]
        akv = [_mm(a_ak[p], v2[p]) for p in pairs]
        sol = [cat([at2[p], akv[p]], axis=1) for p in pairs]
        for lvl in range(levels):
            sol = [sol[p] + _mm(apow[p], sol[p].astype(BF16)) for p in pairs]
            if lvl + 1 < levels:
                apow = [_mm(z, z).astype(BF16) for z in apow]
        ws = [z[:, :128] for z in sol]
        u0s = [jnp.where(own, z[:, 128:], 0.0) for z in sol]
        zeros2 = jnp.zeros((n2, 128), BF16)
        qy = [_mm(a_r[p], cat([cat([ws[p], u0s[p]], 1).astype(BF16), cat([zeros2, v2[p]], 1)], 0)) for p in pairs]
        s_prev = [state[p] for p in pairs]
        s_b = [z.astype(BF16) for z in s_prev]
        ys = [_mm((rt2[p] + qy[p][:, :128]).astype(BF16), s_b[p], NT) + qy[p][:, 128:] for p in pairs]
        y = [jnp.where(lo_lane, z[:chunk], z[chunk:]) for z in ys]
        zeros1 = jnp.zeros((chunk, 128), BF16)
        lhs = [cat([cat([ws[p][:chunk] + ws[p][chunk:], u0s[p][:chunk] + u0s[p][chunk:]], 1).astype(BF16),
                    cat([zeros1, vb[p]], 1)], 0) for p in pairs]
        wbn = [_mm(lhs[p], cat([bt[p], kt[p]], 0), TN) for p in pairs]
        sw = [_mm(s_b[p], wbn[p][:128].astype(BF16)) for p in pairs]
        for p in pairs:
            s_new = (s_prev[p] + sw[p] + wbn[p][128:]) * p_incl[p][chunk - 1:chunk, :]
            state[p] = jnp.where(blockdiag, s_new, 0.0)

        mb = [_mm(cat([y[p], rr[p] * kc[p] * rk_ref[:, sl[p]]], 0), seg, pa=2) for p in pairs]
        dlt = [y[p] - mb[p][:chunk] * (1.0 / HEAD_DIM) for p in pairs]
        var = [_mm(z * z, seg, pa=2) * (1.0 / HEAD_DIM) for z in dlt]
        for p in pairs:
            yn = dlt[p] * lax.rsqrt(var[p] + GN_EPS) * lnw_ref[:, sl[p]] + lnb_ref[:, sl[p]]
            y_ref[rows, sl[p]] = yn + mb[p][chunk:] * vc[p]
        return 0

    lax.fori_loop(0, tt // chunk, chunk_body, 0)

    @pl.when(j == pl.num_programs(1) - 1)
    def _():
        sout_ref[...] = state[...]
        shout_ref[...] = carry[...]


def _rwkv_scan(cols, shift_prev, s0, prm, tt, t_valid):
    b, t, _ = cols.shape
    s = s0.reshape(b, RWKV_PAIRS, 2, HEAD_DIM, HEAD_DIM)
    z = jnp.zeros_like(s[:, :, 0])
    s_bd = jnp.concatenate([jnp.concatenate([s[:, :, 0], z], -1), jnp.concatenate([z, s[:, :, 1]], -1)], -2)
    row = lambda n: pl.BlockSpec((1, n), lambda bi, j: (0, 0))
    vec = lambda a: a.reshape(1, -1)
    y, s_out, sh_out = pl.pallas_call(
        functools.partial(_rwkv_kernel, tt=tt, chunk=SCAN_CHUNK, t_valid=min(t_valid, tt)),
        grid=(b, t // tt),
        in_specs=[pl.BlockSpec((None, tt, C_SHIFT), lambda bi, j: (bi, j, 0)),
                  pl.BlockSpec((None, 1, C_SHIFT), lambda bi, j: (bi, 0, 0)),
                  pl.BlockSpec((None, RWKV_PAIRS, 128, 128), lambda bi, j: (bi, 0, 0, 0)),
                  row(C_SHIFT), row(RWKV_W),
                  pl.BlockSpec((LORA, RWKV_W), lambda bi, j: (0, 0)),
                  row(RWKV_W),
                  pl.BlockSpec((LORA, RWKV_W), lambda bi, j: (0, 0)),
                  row(RWKV_W), row(RWKV_W), row(RWKV_W), row(RWKV_W), row(RWKV_W)],
        out_specs=[pl.BlockSpec((None, tt, RWKV_W), lambda bi, j: (bi, j, 0)),
                   pl.BlockSpec((None, RWKV_PAIRS, 128, 128), lambda bi, j: (bi, 0, 0, 0)),
                   pl.BlockSpec((None, 1, C_SHIFT), lambda bi, j: (bi, 0, 0))],
        out_shape=[jax.ShapeDtypeStruct((b, t, RWKV_W), F32),
                   jax.ShapeDtypeStruct((b, RWKV_PAIRS, 128, 128), F32),
                   jax.ShapeDtypeStruct((b, 1, C_SHIFT), F32)],
        scratch_shapes=[pltpu.VMEM((1, C_SHIFT), F32), pltpu.VMEM((RWKV_PAIRS, 128, 128), F32)]
                       + [pltpu.VMEM((tt, RWKV_W), F32)] * 6,
        compiler_params=_params("parallel", "arbitrary"),
        name="rwkv_scan",
    )(cols, shift_prev.reshape(b, 1, C_SHIFT), s_bd, vec(prm["mu"]), vec(prm["w0"]), prm["w_up"].astype(BF16),
      vec(prm["a0"]), prm["a_up"].astype(BF16), vec(prm["k_k"]), vec(prm["k_a"]), vec(prm["r_k"]),
      vec(prm["ln_w"]), vec(prm["ln_b"]))
    s_fin = jnp.stack([s_out[:, :, :HEAD_DIM, :HEAD_DIM], s_out[:, :, HEAD_DIM:, HEAD_DIM:]], axis=2)
    return y, s_fin.reshape(b, 2 * RWKV_PAIRS, HEAD_DIM, HEAD_DIM), sh_out.reshape(b, C_SHIFT)


def _post_kernel(*refs, n_merge, mix_w):
    x_ref, omem_ref, gate_ref, w_ref, g_ref = refs[:5]
    mix_refs = refs[5:-1]
    y_ref = refs[-1]
    if n_merge:
        er = lax.broadcasted_iota(jnp.int32, (128, 2 * GROUP_W), 0)
        ec = lax.broadcasted_iota(jnp.int32, (128, 2 * GROUP_W), 1)
        expand = (er == ec // HEAD_DIM).astype(BF16)
        ms, ss, os_ = [], [], []
        for g in range(n_merge):
            st = _mm(mix_refs[2 * g + 1][...], expand, pa=3)
            ms.append(st[:, :GROUP_W])
            ss.append(st[:, GROUP_W:])
            os_.append(mix_refs[2 * g][...] / st[:, GROUP_W:])
        mx = functools.reduce(jnp.maximum, ms)
        wts = [jnp.exp(m - mx) * s for m, s in zip(ms, ss)]
        mix = sum(w * o for w, o in zip(wts, os_)) / sum(wts)
    else:
        mix = mix_refs[0][...]
    gate = gate_ref[...]
    act = gate * jax.nn.sigmoid(gate)
    h1 = (mix * act[:, :mix_w]).astype(BF16)
    h2 = (omem_ref[...] * act[:, mix_w:]).astype(BF16)
    out = _dot(h1, w_ref[:mix_w, :]) + _dot(h2, w_ref[mix_w:, :])
    y_ref[...] = x_ref[...] + _rms(out, g_ref[...])


def _post(x, mix, o_mem, gate, w_out, g_post, tm):
    m, d = x.shape
    merge = isinstance(mix, (list, tuple))
    mix_arrays = [a for pair in mix for a in pair] if merge else [mix]
    mix_w = GROUP_W if merge else mix.shape[1]
    gw = gate.shape[1]
    rows = lambda wd: pl.BlockSpec((tm, wd), lambda i: (i, 0))
    return pl.pallas_call(
        functools.partial(_post_kernel, n_merge=len(mix) if merge else 0, mix_w=mix_w),
        grid=(m // tm,),
        in_specs=[rows(d), rows(MEM_W), rows(gw),
                  pl.BlockSpec((gw, d), lambda i: (0, 0)),
                  pl.BlockSpec((1, d), lambda i: (0, 0))] + [rows(a.shape[1]) for a in mix_arrays],
        out_specs=rows(d),
        out_shape=jax.ShapeDtypeStruct((m, d), F32),
        compiler_params=_params("parallel"),
        name="post",
    )(x, o_mem, gate, w_out.astype(BF16), g_post.reshape(1, d), *mix_arrays)


def _t5_bucket(dist):
    max_exact = N_BUCKETS // 2
    d = jnp.maximum(dist, 1).astype(F32)
    large = max_exact + (jnp.log(d / max_exact) / math.log(MAX_DISTANCE / max_exact)
                         * (N_BUCKETS - max_exact)).astype(jnp.int32)
    large = jnp.minimum(large, N_BUCKETS - 1)
    return jnp.where(dist < max_exact, dist, large)


def _group_bias(rel_bias, g):
    dist = DILATIONS[g] * jnp.arange(A_KEYS, dtype=jnp.int32)
    bias = rel_bias[_t5_bucket(dist)]
    return bias[:, g * GROUP_HEADS:(g + 1) * GROUP_HEADS].T.astype(F32)


def _bias_tiles_kernel(x_ref, o_ref):
    row = lax.broadcasted_iota(jnp.int32, (Q_TILE, 2 * Q_TILE), 0)
    for h in range(o_ref.shape[0]):
        t = jnp.broadcast_to(x_ref[h:h + 1, :], (Q_TILE, 2 * Q_TILE))
        for bit in range(int(math.log2(Q_TILE))):
            t = jnp.where((row >> bit) & 1 == 1, pltpu.roll(t, 1 << bit, axis=1), t)
        o_ref[h] = t


def _prompt_bias(biases):
    bias = jnp.concatenate(biases, axis=0)
    neg = jnp.full((bias.shape[0], Q_TILE - 1), NEG_INF, F32)
    table = jnp.concatenate([bias[:, :1], neg, bias[:, :0:-1]], axis=1)
    n = table.shape[0]
    return pl.pallas_call(
        _bias_tiles_kernel,
        out_shape=jax.ShapeDtypeStruct((n, Q_TILE, 2 * Q_TILE), F32),
        name="bias_tiles",
    )(table)


def _sample_bias(biases, n_new):
    past, new = [], []
    key = np.arange(128)
    for g, bias in enumerate(biases):
        pg, ng = [], []
        for t in range(n_new):
            if g == 0:
                k_past = 128 + t - key
                bp = jnp.where(jnp.asarray(key >= t), bias[:, np.clip(k_past, 0, A_KEYS - 1)], NEG_INF)
                bn = jnp.where(jnp.asarray(key <= t), bias[:, np.clip(t - key, 0, A_KEYS - 1)], NEG_INF)
            else:
                bp = bias[:, 128 - key]
                bn = jnp.where(jnp.asarray(key == t), bias[:, np.zeros(128, np.int64)], NEG_INF)
            pad = jnp.zeros((8 - GROUP_HEADS, 128), F32)
            pg.append(jnp.concatenate([bp, pad], 0))
            ng.append(jnp.concatenate([bn, pad + NEG_INF], 0))
        past.append(jnp.stack(pg))
        new.append(jnp.stack(ng))
    return jnp.stack(past), jnp.stack(new)


def _pieces_a():
    q = [((g * GROUP_W, (g + 1) * GROUP_W),) for g in range(N_GROUPS)]
    kv = [((A_Q + g * GROUP_W, A_Q + (g + 1) * GROUP_W), (2 * A_Q + g * GROUP_W, 2 * A_Q + (g + 1) * GROUP_W))
          for g in range(N_GROUPS)]
    qmem = ((3 * A_Q, 3 * A_Q + MEM_W),)
    gate = ((3 * A_Q + MEM_W, 3 * A_Q + MEM_W + GROUP_W + MEM_W),)
    return tuple(q + kv + [qmem, gate])


def _pieces_b():
    return (((0, C_SHIFT),), ((C_SHIFT, C_SHIFT + MEM_W),), ((C_SHIFT + MEM_W, C_SHIFT + MEM_W + RWKV_W + MEM_W),))


def kernel(x_prompt, x_sample, mem_prompt, cache_mem_kv, cache_win0, cache_win1, cache_win2, state_wkv, state_shift, norm_pre, norm_post, norm_mem, w_mem_kv, rel_bias, w_in_a, w_out_a, w_in_b, w_out_b, rwkv_mu, rwkv_w0, rwkv_w_up, rwkv_a0, rwkv_a_up, rwkv_k_k, rwkv_k_a, rwkv_r_k, rwkv_ln_w, rwkv_ln_b):
    bp, tp, d = x_prompt.shape
    bs, ts, _ = x_sample.shape
    tsp = SAMPLE_PAD_T
    xp = x_prompt.reshape(bp * tp, d)
    xs = jnp.pad(x_sample, ((0, 0), (0, tsp - ts), (0, 0))).reshape(bs * tsp, d)
    mem2d = mem_prompt.reshape(bp * N_MEM, d)
    tm_p, tm_s = 512, bs * tsp

    (mkv0,) = _norm_proj(mem2d, norm_mem[0], w_mem_kv[0], (((0, 2 * MEM_W),),), 512)
    biases = [_group_bias(rel_bias, g) for g in range(N_GROUPS)]
    outs_p = _norm_proj(xp, norm_pre[0], w_in_a[0], _pieces_a(), tm_p)
    outs_s = _norm_proj(xs, norm_pre[0], w_in_a[0], _pieces_a(), tm_s)
    q_p, kv_p, qmem_p, gate_p = outs_p[0:3], outs_p[3:6], outs_p[6], outs_p[7]
    q_s, kv_s, qmem_s, gate_s = outs_s[0:3], outs_s[3:6], outs_s[6], outs_s[7]

    bias_tiles = _prompt_bias(biases)
    merged = [_dil_attn(q_p[g].reshape(bp, tp, GROUP_W), kv_p[g].reshape(bp, tp, 2 * GROUP_W), bias_tiles, g)
              for g in range(N_GROUPS)]
    merged = [(a.reshape(bp * tp, GROUP_W), s.reshape(bp * tp, 128)) for a, s in merged]
    omem_p = _cross_attn(qmem_p.reshape(bp, tp, MEM_W), mkv0.reshape(bp, N_MEM, 2 * MEM_W), 512)
    xp1 = _post(xp, merged, omem_p.reshape(bp * tp, MEM_W), gate_p, w_out_a[0], norm_post[0], tm_p)

    bias_past, bias_new = _sample_bias(biases, ts)
    q_s_all = jnp.concatenate(q_s, axis=-1).reshape(bs, tsp, A_Q)
    kvn = [a.reshape(bs, tsp, 2 * GROUP_W) for a in kv_s]
    o_s = _sample_attn(q_s_all, kvn, (cache_win0[0], cache_win1[0], cache_win2[0]), bias_past, bias_new, ts)
    omem_s = _cross_attn(qmem_s.reshape(bs, tsp, MEM_W), cache_mem_kv[0].reshape(bs, N_MEM, 2 * MEM_W), tsp)
    xs1 = _post(xs, o_s.reshape(bs * tsp, GROUP_W), omem_s.reshape(bs * tsp, MEM_W), gate_s, w_out_a[0],
                norm_post[0], tm_s)

    (mkv1,) = _norm_proj(mem2d, norm_mem[1], w_mem_kv[1], (((0, 2 * MEM_W),),), 512)
    cols_p, qmem_p, gate_p = _norm_proj(xp1, norm_pre[1], w_in_b[0], _pieces_b(), tm_p)
    cols_s, qmem_s, gate_s = _norm_proj(xs1, norm_pre[1], w_in_b[0], _pieces_b(), tm_s)
    prm = dict(mu=rwkv_mu[0], w0=rwkv_w0[0], w_up=rwkv_w_up[0], a0=rwkv_a0[0], a_up=rwkv_a_up[0], k_k=rwkv_k_k[0],
               k_a=rwkv_k_a[0], r_k=rwkv_r_k[0], ln_w=rwkv_ln_w[0], ln_b=rwkv_ln_b[0])
    y_p, wkv_p, sh_p = _rwkv_scan(cols_p.reshape(bp, tp, C_SHIFT), jnp.zeros((bp, C_SHIFT), F32),
                                  jnp.zeros((bp, 2 * RWKV_PAIRS, HEAD_DIM, HEAD_DIM), F32), prm, 256, tp)
    cols_s_pad = jnp.pad(cols_s.reshape(bs, tsp, C_SHIFT), ((0, 0), (0, SCAN_CHUNK - tsp), (0, 0)))
    y_s, wkv_s, sh_s = _rwkv_scan(cols_s_pad, state_shift[0], state_wkv[0], prm, SCAN_CHUNK, ts)
    y_s = y_s[:, :tsp].reshape(bs * tsp, RWKV_W)
    omem_p = _cross_attn(qmem_p.reshape(bp, tp, MEM_W), mkv1.reshape(bp, N_MEM, 2 * MEM_W), 512)
    omem_s = _cross_attn(qmem_s.reshape(bs, tsp, MEM_W), cache_mem_kv[1].reshape(bs, N_MEM, 2 * MEM_W), tsp)
    xp2 = _post(xp1, y_p.reshape(bp * tp, RWKV_W), omem_p.reshape(bp * tp, MEM_W), gate_p, w_out_b[0],
                norm_post[1], tm_p)
    xs2 = _post(xs1, y_s, omem_s.reshape(bs * tsp, MEM_W), gate_s, w_out_b[0], norm_post[1], tm_s)

    kv_shape = (2, GROUP_HEADS, HEAD_DIM)
    new_mem_kv = jnp.stack([mkv0, mkv1]).reshape(2, bp, N_MEM, 2, MEM_W // HEAD_DIM, HEAD_DIM)
    win_p = [kv_p[g].reshape(bp, tp, *kv_shape)[None, :, tp - min(WINDOWS[g], tp):] for g in range(N_GROUPS)]
    win_s = [kv_s[g].reshape(bs, tsp, *kv_shape)[None, :, :ts] for g in range(N_GROUPS)]
    return (xp2.reshape(bp, tp, d), xs2.reshape(bs, tsp, d)[:, :ts], new_mem_kv,
            win_p[0], win_p[1], win_p[2], win_s[0], win_s[1], win_s[2],
            wkv_p[None], wkv_s[None], sh_p[None], sh_s[None])
```

```python
import functools
import math

import numpy as np
import jax
import jax.numpy as jnp
from jax import lax
from jax.experimental import pallas as pl
from jax.experimental.pallas import tpu as pltpu

F32 = jnp.float32
BF16 = jnp.bfloat16

D_MODEL = 1024
HEAD_DIM = 64
N_GROUPS = 3
GROUP_HEADS = 4
WINDOWS = (128, 512, 2048)
DILATIONS = (1, 4, 16)
A_HEADS = N_GROUPS * GROUP_HEADS
A_KEYS = 129
GROUP_W = GROUP_HEADS * HEAD_DIM
A_Q = A_HEADS * HEAD_DIM
N_MEM = 256
MEM_W = 256
RWKV_W = 768
RWKV_PAIRS = RWKV_W // 128
LORA = 64
C_SHIFT = 3 * RWKV_W + 2 * LORA
N_BUCKETS = 32
MAX_DISTANCE = WINDOWS[-1]
RMS_EPS = 1e-6
GN_EPS = 64e-5
NEG_INF = -1e30
SCALE = HEAD_DIM ** -0.5

Q_TILE = 128
SCAN_CHUNK = 64
SAMPLE_PAD_T = 8
VMEM_LIMIT = 56 * 1024 * 1024
NN =(((1,), (0,)), ((), ()))
NT = (((1,), (1,)), ((), ()))
TN = (((0,), (0,)), ((), ()))


def _params(*sem):
    return pltpu.CompilerParams(dimension_semantics=sem, vmem_limit_bytes=VMEM_LIMIT)


def _dot(a, b):
    return jnp.dot(a, b, preferred_element_type=F32)


def _dot_nt(a, b):
    return lax.dot_general(a, b, NT, preferred_element_type=F32)


def _split(x, n):
    if x.dtype == BF16:
        return [x]
    parts, rem = [], x
    for i in range(n):
        parts.append(rem.astype(BF16))
        if i + 1 < n:
            rem = rem - parts[-1].astype(F32)
    return parts


def _mm(a, b, dims=NN, pa=1, pb=1):
    pas, pbs = _split(a, pa), _split(b, pb)
    order = max(len(pas), len(pbs))
    out = None
    for i, ai in enumerate(pas):
        for j, bj in enumerate(pbs):
            if i + j < order:
                term = lax.dot_general(ai, bj, dims, preferred_element_type=F32)
                out = term if out is None else out + term
    return out


def _rms(x, g):
    return x * lax.rsqrt(jnp.mean(x * x, axis=-1, keepdims=True) + RMS_EPS) * g


def _norm_proj_kernel(x_ref, g_ref, w_ref, *out_refs, pieces, transposed):
    xn = _rms(x_ref[...], g_ref[...]).astype(BF16)
    t_refs = iter(out_refs[len(pieces):])
    for idx, (out_ref, cols) in enumerate(zip(out_refs, pieces)):
        t_ref = next(t_refs) if idx in transposed else None
        off = 0
        for c0, c1 in cols:
            res = _dot(xn, w_ref[:, c0:c1])
            out_ref[:, off:off + c1 - c0] = res
            if t_ref is not None:
                t_ref[off:off + c1 - c0, :] = res.T
            off += c1 - c0


def _norm_proj(x, g, w, pieces, tm, transposed=(), rows_per_batch=None):
    m, d = x.shape
    n = w.shape[1]
    widths = [sum(c1 - c0 for c0, c1 in cols) for cols in pieces]
    out_specs = [pl.BlockSpec((tm, wd), lambda i: (i, 0)) for wd in widths]
    out_shape = [jax.ShapeDtypeStruct((m, wd), F32) for wd in widths]
    if transposed:
        tiles = rows_per_batch // tm
        out_specs += [pl.BlockSpec((None, widths[idx], tm), lambda i: (i // tiles, 0, i % tiles)) for idx in transposed]
        out_shape += [jax.ShapeDtypeStruct((m // rows_per_batch, widths[idx], rows_per_batch), F32)
                      for idx in transposed]
    return pl.pallas_call(
        functools.partial(_norm_proj_kernel, pieces=pieces, transposed=tuple(transposed)),
        grid=(m // tm,),
        in_specs=[pl.BlockSpec((tm, d), lambda i: (i, 0)),
                  pl.BlockSpec((1, d), lambda i: (0, 0)),
                  pl.BlockSpec((d, n), lambda i: (0, 0))],
        out_specs=out_specs,
        out_shape=out_shape,
        compiler_params=_params("parallel"),
        name="norm_proj",
    )(x, g.reshape(1, d), w.astype(BF16))


def _mem_kv_kernel(x_ref, g_ref, wt_ref, o_ref):
    xn = _rms(x_ref[...], g_ref[...]).astype(BF16)
    o_ref[...] = _dot_nt(wt_ref[...], xn)


def _mem_kv(mem, g, w):
    b, n, d = mem.shape
    nl, _, wd = w.shape
    return pl.pallas_call(
        _mem_kv_kernel,
        grid=(nl, b),
        in_specs=[pl.BlockSpec((None, n, d), lambda l, bi: (bi, 0, 0)),
                  pl.BlockSpec((None, 1, d), lambda l, bi: (l, 0, 0)),
                  pl.BlockSpec((None, wd, d), lambda l, bi: (l, 0, 0))],
        out_specs=pl.BlockSpec((None, None, wd, n), lambda l, bi: (l, bi, 0, 0)),
        out_shape=jax.ShapeDtypeStruct((nl, b, wd, n), F32),
        compiler_params=_params("parallel", "parallel"),
        name="mem_kv",
    )(mem, g.reshape(nl, 1, d), jnp.swapaxes(w, 1, 2).astype(BF16))


def _dil_attn_kernel(q_ref, kvc_ref, kvp_ref, bias_ref, acc_ref, st_ref, *, has_prev):
    i = pl.program_id(2)
    q = (q_ref[...] * SCALE).astype(BF16)
    kvc = kvc_ref[...].astype(BF16)
    if has_prev:
        kvp = kvp_ref[...].astype(BF16)
    lane = lax.broadcasted_iota(jnp.int32, (Q_TILE, 128), 1)
    st = jnp.zeros((Q_TILE, 128), F32)
    for h in range(GROUP_HEADS):
        ks, vs = slice(h * HEAD_DIM, (h + 1) * HEAD_DIM), slice(GROUP_W + h * HEAD_DIM, GROUP_W + (h + 1) * HEAD_DIM)
        qh = q[:, ks]
        lc = _dot_nt(qh, kvc[:, ks]) + bias_ref[h, :, :Q_TILE]
        m = jnp.max(lc, axis=-1, keepdims=True)
        if has_prev:
            lp = jnp.where(i > 0, _dot_nt(qh, kvp[:, ks]) + bias_ref[h, :, Q_TILE:], NEG_INF)
            m = jnp.maximum(m, jnp.max(lp, axis=-1, keepdims=True))
        pc = jnp.exp(lc - m)
        s = jnp.sum(pc, axis=-1, keepdims=True)
        o = _dot(pc.astype(BF16), kvc[:, vs])
        if has_prev:
            pp = jnp.exp(lp - m)
            s = s + jnp.sum(pp, axis=-1, keepdims=True)
            o = o + _dot(pp.astype(BF16), kvp[:, vs])
        acc_ref[:, ks] = o
        st = jnp.where(lane == h, m, st)
        st = jnp.where(lane == GROUP_HEADS + h, s, st)
    st_ref[...] = st


def _dil_attn(q, kv, bias_tiles, g):
    dil = DILATIONS[g]
    b, t, _ = q.shape
    ln = t // dil
    nt = ln // Q_TILE
    qv = q.reshape(b, ln, dil * GROUP_W)
    kvv = kv.reshape(b, ln, dil * 2 * GROUP_W)
    acc, st = pl.pallas_call(
        functools.partial(_dil_attn_kernel, has_prev=nt > 1),
        grid=(b, dil, nt),
        in_specs=[pl.BlockSpec((None, Q_TILE, GROUP_W), lambda bi, r, i: (bi, i, r)),
                  pl.BlockSpec((None, Q_TILE, 2 * GROUP_W), lambda bi, r, i: (bi, i, r)),
                  pl.BlockSpec((None, Q_TILE, 2 * GROUP_W), lambda bi, r, i: (bi, jnp.maximum(i - 1, 0), r)),
                  pl.BlockSpec((GROUP_HEADS, Q_TILE, 2 * Q_TILE), lambda bi, r, i: (g, 0, 0))],
        out_specs=[pl.BlockSpec((None, Q_TILE, GROUP_W), lambda bi, r, i: (bi, i, r)),
                   pl.BlockSpec((None, Q_TILE, 128), lambda bi, r, i: (bi, i, r))],
        out_shape=[jax.ShapeDtypeStruct((b, ln, dil * GROUP_W), F32),
                   jax.ShapeDtypeStruct((b, ln, dil * 128), F32)],
        compiler_params=_params("parallel", "parallel", "arbitrary"),
        name=f"dil_attn_d{dil}",
    )(qv, kvv, kvv, bias_tiles)
    return acc.reshape(b, t, GROUP_W), st.reshape(b, t, 128)


def _sample_attn_kernel(q_ref, kn0_ref, kn1_ref, kn2_ref, c0_ref, c1_ref, c2_ref, b0_ref, b1_ref, b2_ref,
                        bnew_ref, o_ref, *, n_new):
    q = (q_ref[...] * SCALE).astype(BF16)
    qf = q.astype(F32)
    groups = ((kn0_ref, c0_ref, b0_ref), (kn1_ref, c1_ref, b1_ref), (kn2_ref, c2_ref, b2_ref))
    new_rows = []
    for g, (kn_ref, _, _) in enumerate(groups):
        kn = kn_ref[...]
        shifted = [kn] + [pltpu.roll(kn, s, axis=0) for s in range(1, n_new if g == 0 else 1)]
        new_rows.append([z.astype(BF16).astype(F32) for z in shifted])
    for h in range(GROUP_HEADS):
        ks, vs = slice(h * HEAD_DIM, (h + 1) * HEAD_DIM), slice(GROUP_W + h * HEAD_DIM, GROUP_W + (h + 1) * HEAD_DIM)
        ms, ss, os_ = [], [], []
        for g, (_, c_ref, b_ref) in enumerate(groups):
            qs = slice(g * GROUP_W + h * HEAD_DIM, g * GROUP_W + (h + 1) * HEAD_DIM)
            lp = _dot(q[:, qs], c_ref[0, h].astype(BF16)) + b_ref[h]
            m = jnp.max(lp, axis=-1, keepdims=True)
            lns = []
            for s, rows in enumerate(new_rows[g]):
                ln = jnp.sum(qf[:, qs] * rows[:, ks], axis=-1, keepdims=True) + bnew_ref[g, s][:, h:h + 1]
                lns.append(ln)
                m = jnp.maximum(m, ln)
            pp = jnp.exp(lp - m)
            den = jnp.sum(pp, axis=-1, keepdims=True)
            o = _dot_nt(pp.astype(BF16), c_ref[1, h].astype(BF16))
            for ln, rows in zip(lns, new_rows[g]):
                pn = jnp.exp(ln - m)
                den = den + pn
                o = o + pn.astype(BF16).astype(F32) * rows[:, vs]
            ms.append(m)
            ss.append(den)
            os_.append(o / den)
        mx = jnp.maximum(jnp.maximum(ms[0], ms[1]), ms[2])
        wts = [jnp.exp(ms[g] - mx) * ss[g] for g in range(N_GROUPS)]
        num = wts[0] * os_[0] + wts[1] * os_[1] + wts[2] * os_[2]
        o_ref[:, ks] = num / (wts[0] + wts[1] + wts[2])


def _sample_attn(q, kvn, caches, bias_past, bias_new, n_new):
    b, tp, _ = q.shape
    new_spec = pl.BlockSpec((None, tp, 2 * GROUP_W), lambda bi: (bi, 0, 0))
    cache_spec = lambda w: pl.BlockSpec((None, 2, GROUP_HEADS, HEAD_DIM, w), lambda bi: (bi, 0, 0, 0, 0))
    bias_spec = lambda w: pl.BlockSpec((GROUP_HEADS, tp, w), lambda bi: (0, 0, 0))
    return pl.pallas_call(
        functools.partial(_sample_attn_kernel, n_new=n_new),
        grid=(b,),
        in_specs=[pl.BlockSpec((None, tp, A_Q), lambda bi: (bi, 0, 0)), new_spec, new_spec, new_spec]
                 + [cache_spec(w) for w in WINDOWS] + [bias_spec(w) for w in WINDOWS]
                 + [pl.BlockSpec((N_GROUPS, n_new, tp, 128), lambda bi: (0, 0, 0, 0))],
        out_specs=pl.BlockSpec((None, tp, GROUP_W), lambda bi: (bi, 0, 0)),
        out_shape=jax.ShapeDtypeStruct((b, tp, GROUP_W), F32),
        compiler_params=_params("parallel"),
        name="sample_attn",
    )(q, kvn[0], kvn[1], kvn[2], *caches, *bias_past, bias_new)


def _cross_attn_kernel(q_ref, kvt_ref, o_ref):
    q = (q_ref[...] * SCALE).astype(BF16)
    kvt = kvt_ref[...].astype(BF16)
    for h in range(MEM_W // HEAD_DIM):
        ks, vs = slice(h * HEAD_DIM, (h + 1) * HEAD_DIM), slice(MEM_W + h * HEAD_DIM, MEM_W + (h + 1) * HEAD_DIM)
        lg = _dot(q[:, ks], kvt[ks, :])
        p = jnp.exp(lg - jnp.max(lg, axis=-1, keepdims=True))
        o = _dot_nt(p.astype(BF16), kvt[vs, :])
        o_ref[:, ks] = o / jnp.sum(p, axis=-1, keepdims=True)


def _cross_attn(q, kvt, layer, tq):
    b, t, _ = q.shape
    return pl.pallas_call(
        _cross_attn_kernel,
        grid=(b, t // tq),
        in_specs=[pl.BlockSpec((None, tq, MEM_W), lambda bi, i: (bi, i, 0)),
                  pl.BlockSpec((None, None, 2 * MEM_W, N_MEM), lambda bi, i: (layer, bi, 0, 0))],
        out_specs=pl.BlockSpec((None, tq, MEM_W), lambda bi, i: (bi, i, 0)),
        out_shape=jax.ShapeDtypeStruct((b, t, MEM_W), F32),
        compiler_params=_params("parallel", "parallel"),
        name="cross_attn",
    )(q, kvt)


def _rwkv_kernel(cols_ref, shift_ref, s0_ref, mu_ref, w0_ref, wup_ref, a0_ref, aup_ref, kk_ref, ka_ref, rk_ref,
                 lnw_ref, lnb_ref, y_ref, sout_ref, shout_ref,
                 carry, state, r_s, k_s, v_s, a_s, b_s, lw_s, *, tt, chunk, t_valid):
    j = pl.program_id(1)

    @pl.when(j == 0)
    def _():
        carry[...] = shift_ref[...]
        state[...] = s0_ref[...]

    cols = cols_ref[...]
    row = lax.broadcasted_iota(jnp.int32, (tt, 1), 0)
    prev = jnp.where(row == 0, carry[...], pltpu.roll(cols, 1, axis=0))
    carry[...] = cols[t_valid - 1:t_valid, :]
    xs = cols + mu_ref[...] * (prev - cols)
    r = xs[:, :RWKV_W]
    k = xs[:, RWKV_W:2 * RWKV_W]
    v = xs[:, 2 * RWKV_W:3 * RWKV_W]
    wd = xs[:, 3 * RWKV_W:3 * RWKV_W + LORA]
    ad = xs[:, 3 * RWKV_W + LORA:]
    w_log = -jax.nn.softplus(-(w0_ref[...] + _dot(jnp.tanh(wd).astype(BF16), wup_ref[...]))) - 0.5
    log_decay = -jnp.exp(w_log)
    a = jax.nn.sigmoid(a0_ref[...] + _dot(ad.astype(BF16), aup_ref[...]))

    blockdiag = (lax.broadcasted_iota(jnp.int32, (128, 128), 0) // HEAD_DIM
                 == lax.broadcasted_iota(jnp.int32, (128, 128), 1) // HEAD_DIM)
    seg = blockdiag.astype(BF16)

    kk = k * kk_ref[...]
    kk_sq = kk * kk
    k2 = k * (1.0 + (a - 1.0) * ka_ref[...])
    live = row < t_valid
    for p in range(RWKV_PAIRS):
        ps = slice(p * 128, (p + 1) * 128)
        nrm = jnp.maximum(jnp.sqrt(_mm(kk_sq[:, ps], seg, pa=2)), 1e-12)
        kkn = kk[:, ps] / nrm
        if t_valid < tt:
            zero = jnp.zeros((tt, 128), F32)
            r_s[:, ps] = r[:, ps]
            k_s[:, ps] = jnp.where(live, k2[:, ps], zero)
            v_s[:, ps] = jnp.where(live, v[:, ps], zero)
            a_s[:, ps] = jnp.where(live, -kkn, zero)
            b_s[:, ps] = jnp.where(live, kkn * a[:, ps], zero)
            lw_s[:, ps] = jnp.where(live, log_decay[:, ps], zero)
        else:
            r_s[:, ps] = r[:, ps]
            k_s[:, ps] = k2[:, ps]
            v_s[:, ps] = v[:, ps]
            a_s[:, ps] = -kkn
            b_s[:, ps] = kkn * a[:, ps]
            lw_s[:, ps] = log_decay[:, ps]

    ci = lax.broadcasted_iota(jnp.int32, (chunk, chunk), 0)
    cj = lax.broadcasted_iota(jnp.int32, (chunk, chunk), 1)
    tri_incl = (ci >= cj).astype(BF16)
    lo_lane = lax.broadcasted_iota(jnp.int32, (chunk, 128), 1) < HEAD_DIM
    levels = int(math.log2(chunk))
    n2 = 2 * chunk
    ri = lax.broadcasted_iota(jnp.int32, (n2, n2), 0)
    rj = lax.broadcasted_iota(jnp.int32, (n2, n2), 1)
    same = ri // chunk == rj // chunk
    strict = jnp.logical_and(same, ri > rj)
    incl = jnp.logical_and(same, ri >= rj)
    own = lax.broadcasted_iota(jnp.int32, (n2, 128), 0) // chunk == lax.broadcasted_iota(jnp.int32, (n2, 128), 1) // HEAD_DIM
    dup = lambda z: jnp.concatenate([z, z], axis=0)
    cat = jnp.concatenate

    def chunk_body(c, _):
        rows = pl.ds(pl.multiple_of(c * chunk, chunk), chunk)
        cum_all = _mm(tri_incl, lw_s[rows, :], pb=3)
        pairs = range(RWKV_PAIRS)
        sl = [slice(p * 128, (p + 1) * 128) for p in pairs]
        cum = [cum_all[:, s] for s in sl]
        p_incl = [jnp.exp(z) for z in cum]
        p_inv = [jnp.exp(-z) for z in cum]
        rr = [r_s[rows, s] for s in sl]
        kc = [k_s[rows, s] for s in sl]
        vc = [v_s[rows, s] for s in sl]
        at2 = [jnp.where(own, dup(a_s[rows, sl[p]] * jnp.exp(cum[p] - lw_s[rows, sl[p]])), 0.0) for p in pairs]
        rt2 = [jnp.where(own, dup(rr[p] * p_incl[p]), 0.0) for p in pairs]
        bt = [(b_s[rows, sl[p]] * p_inv[p]).astype(BF16) for p in pairs]
        kt = [(kc[p] * p_inv[p]).astype(BF16) for p in pairs]
        vb = [z.astype(BF16) for z in vc]
        v2 = [dup(z) for z in vb]
        g = [_mm(cat([at2[p], rt2[p]], 0).astype(BF16), cat([dup(bt[p]), dup(kt[p])], 0), NT) for p in pairs]
        a_ak = [jnp.where(strict, z[:n2, n2:], 0.0).astype(BF16) for z in g]
        apow = [jnp.where(strict, z[:n2, :n2], 0.0).astype(BF16) for z in g]
        a_r = [cat([jnp.where(incl, z[n2:, :n2], 0.0), jnp.where(incl, z[n2:, n2:], 0.0)], 1).astype(BF16)
               for z in g]
        akv = [_mm(a_ak[p], v2[p]) for p in pairs]
        sol = [cat([at2[p], akv[p]], axis=1) for p in pairs]
        for lvl in range(levels):
            sol = [sol[p] + _mm(apow[p], sol[p].astype(BF16)) for p in pairs]
            if lvl + 1 < levels:
                apow = [_mm(z, z).astype(BF16) for z in apow]
        ws = [z[:, :128] for z in sol]
        u0s = [jnp.where(own, z[:, 128:], 0.0) for z in sol]
        zeros2 = jnp.zeros((n2, 128), BF16)
        qy = [_mm(a_r[p], cat([cat([ws[p], u0s[p]], 1).astype(BF16), cat([zeros2, v2[p]], 1)], 0)) for p in pairs]
        s_prev = [state[p] for p in pairs]
        s_b = [z.astype(BF16) for z in s_prev]
        ys = [_mm((rt2[p] + qy[p][:, :128]).astype(BF16), s_b[p], NT) + qy[p][:, 128:] for p in pairs]
        y = [jnp.where(lo_lane, z[:chunk], z[chunk:]) for z in ys]
        zeros1 = jnp.zeros((chunk, 128), BF16)
        lhs = [cat([cat([ws[p][:chunk] + ws[p][chunk:], u0s[p][:chunk] + u0s[p][chunk:]], 1).astype(BF16),
                    cat([zeros1, vb[p]], 1)], 0) for p in pairs]
        wbn = [_mm(lhs[p], cat([bt[p], kt[p]], 0), TN) for p in pairs]
        sw = [_mm(s_b[p], wbn[p][:128].astype(BF16)) for p in pairs]
        for p in pairs:
            s_new = (s_prev[p] + sw[p] + wbn[p][128:]) * p_incl[p][chunk - 1:chunk, :]
            state[p] = jnp.where(blockdiag, s_new, 0.0)

        mb = [_mm(cat([y[p], rr[p] * kc[p] * rk_ref[:, sl[p]]], 0), seg, pa=2) for p in pairs]
        dlt = [y[p] - mb[p][:chunk] * (1.0 / HEAD_DIM) for p in pairs]
        var = [_mm(z * z, seg, pa=2) * (1.0 / HEAD_DIM) for z in dlt]
        for p in pairs:
            yn = dlt[p] * lax.rsqrt(var[p] + GN_EPS) * lnw_ref[:, sl[p]] + lnb_ref[:, sl[p]]
            y_ref[rows, sl[p]] = yn + mb[p][chunk:] * vc[p]
        return 0

    lax.fori_loop(0, tt // chunk, chunk_body, 0)

    @pl.when(j == pl.num_programs(1) - 1)
    def _():
        sout_ref[...] = state[...]
        shout_ref[...] = carry[...]


def _rwkv_scan(cols, shift_prev, s0, prm, tt, t_valid):
    b, t, _ = cols.shape
    s = s0.reshape(b, RWKV_PAIRS, 2, HEAD_DIM, HEAD_DIM)
    z = jnp.zeros_like(s[:, :, 0])
    s_bd = jnp.concatenate([jnp.concatenate([s[:, :, 0], z], -1), jnp.concatenate([z, s[:, :, 1]], -1)], -2)
    row = lambda n: pl.BlockSpec((1, n), lambda bi, j: (0, 0))
    vec = lambda a: a.reshape(1, -1)
    y, s_out, sh_out = pl.pallas_call(
        functools.partial(_rwkv_kernel, tt=tt, chunk=SCAN_CHUNK, t_valid=min(t_valid, tt)),
        grid=(b, t // tt),
        in_specs=[pl.BlockSpec((None, tt, C_SHIFT), lambda bi, j: (bi, j, 0)),
                  pl.BlockSpec((None, 1, C_SHIFT), lambda bi, j: (bi, 0, 0)),
                  pl.BlockSpec((None, RWKV_PAIRS, 128, 128), lambda bi, j: (bi, 0, 0, 0)),
                  row(C_SHIFT), row(RWKV_W),
                  pl.BlockSpec((LORA, RWKV_W), lambda bi, j: (0, 0)),
                  row(RWKV_W),
                  pl.BlockSpec((LORA, RWKV_W), lambda bi, j: (0, 0)),
                  row(RWKV_W), row(RWKV_W), row(RWKV_W), row(RWKV_W), row(RWKV_W)],
        out_specs=[pl.BlockSpec((None, tt, RWKV_W), lambda bi, j: (bi, j, 0)),
                   pl.BlockSpec((None, RWKV_PAIRS, 128, 128), lambda bi, j: (bi, 0, 0, 0)),
                   pl.BlockSpec((None, 1, C_SHIFT), lambda bi, j: (bi, 0, 0))],
        out_shape=[jax.ShapeDtypeStruct((b, t, RWKV_W), F32),
                   jax.ShapeDtypeStruct((b, RWKV_PAIRS, 128, 128), F32),
                   jax.ShapeDtypeStruct((b, 1, C_SHIFT), F32)],
        scratch_shapes=[pltpu.VMEM((1, C_SHIFT), F32), pltpu.VMEM((RWKV_PAIRS, 128, 128), F32)]
                       + [pltpu.VMEM((tt, RWKV_W), F32)] * 6,
        compiler_params=_params("parallel", "arbitrary"),
        name="rwkv_scan",
    )(cols, shift_prev.reshape(b, 1, C_SHIFT), s_bd, vec(prm["mu"]), vec(prm["w0"]), prm["w_up"].astype(BF16),
      vec(prm["a0"]), prm["a_up"].astype(BF16), vec(prm["k_k"]), vec(prm["k_a"]), vec(prm["r_k"]),
      vec(prm["ln_w"]), vec(prm["ln_b"]))
    s_fin = jnp.stack([s_out[:, :, :HEAD_DIM, :HEAD_DIM], s_out[:, :, HEAD_DIM:, HEAD_DIM:]], axis=2)
    return y, s_fin.reshape(b, 2 * RWKV_PAIRS, HEAD_DIM, HEAD_DIM), sh_out.reshape(b, C_SHIFT)


def _post_kernel(*refs, n_merge, mix_w):
    x_ref, omem_ref, gate_ref, w_ref, g_ref = refs[:5]
    mix_refs = refs[5:-1]
    y_ref = refs[-1]
    if n_merge:
        er = lax.broadcasted_iota(jnp.int32, (128, 2 * GROUP_W), 0)
        ec = lax.broadcasted_iota(jnp.int32, (128, 2 * GROUP_W), 1)
        expand = (er == ec // HEAD_DIM).astype(BF16)
        ms, ss, os_ = [], [], []
        for g in range(n_merge):
            st = _mm(mix_refs[2 * g + 1][...], expand, pa=3)
            ms.append(st[:, :GROUP_W])
            ss.append(st[:, GROUP_W:])
            os_.append(mix_refs[2 * g][...] / st[:, GROUP_W:])
        mx = functools.reduce(jnp.maximum, ms)
        wts = [jnp.exp(m - mx) * s for m, s in zip(ms, ss)]
        mix = sum(w * o for w, o in zip(wts, os_)) / sum(wts)
    else:
        mix = mix_refs[0][...]
    gate = gate_ref[...]
    act = gate * jax.nn.sigmoid(gate)
    h1 = (mix * act[:, :mix_w]).astype(BF16)
    h2 = (omem_ref[...] * act[:, mix_w:]).astype(BF16)
    out = _dot(h1, w_ref[:mix_w, :]) + _dot(h2, w_ref[mix_w:, :])
    y_ref[...] = x_ref[...] + _rms(out, g_ref[...])


def _post(x, mix, o_mem, gate, w_out, g_post, tm):
    m, d = x.shape
    merge = isinstance(mix, (list, tuple))
    mix_arrays = [a for pair in mix for a in pair] if merge else [mix]
    mix_w = GROUP_W if merge else mix.shape[1]
    gw = gate.shape[1]
    rows = lambda wd: pl.BlockSpec((tm, wd), lambda i: (i, 0))
    return pl.pallas_call(
        functools.partial(_post_kernel, n_merge=len(mix) if merge else 0, mix_w=mix_w),
        grid=(m // tm,),
        in_specs=[rows(d), rows(MEM_W), rows(gw),
                  pl.BlockSpec((gw, d), lambda i: (0, 0)),
                  pl.BlockSpec((1, d), lambda i: (0, 0))] + [rows(a.shape[1]) for a in mix_arrays],
        out_specs=rows(d),
        out_shape=jax.ShapeDtypeStruct((m, d), F32),
        compiler_params=_params("parallel"),
        name="post",
    )(x, o_mem, gate, w_out.astype(BF16), g_post.reshape(1, d), *mix_arrays)


def _t5_bucket(dist):
    max_exact = N_BUCKETS // 2
    d = jnp.maximum(dist, 1).astype(F32)
    large = max_exact + (jnp.log(d / max_exact) / math.log(MAX_DISTANCE / max_exact)
                         * (N_BUCKETS - max_exact)).astype(jnp.int32)
    large = jnp.minimum(large, N_BUCKETS - 1)
    return jnp.where(dist < max_exact, dist, large)


def _group_bias(rel_bias, g):
    dist = DILATIONS[g] * jnp.arange(A_KEYS, dtype=jnp.int32)
    bias = rel_bias[_t5_bucket(dist)]
    return bias[:, g * GROUP_HEADS:(g + 1) * GROUP_HEADS].T.astype(F32)


def _bias_tiles_kernel(x_ref, o_ref):
    row = lax.broadcasted_iota(jnp.int32, (Q_TILE, 2 * Q_TILE), 0)
    for h in range(o_ref.shape[0]):
        t = jnp.broadcast_to(x_ref[h:h + 1, :], (Q_TILE, 2 * Q_TILE))
        for bit in range(int(math.log2(Q_TILE))):
            t = jnp.where((row >> bit) & 1 == 1, pltpu.roll(t, 1 << bit, axis=1), t)
        o_ref[h] = t


def _prompt_bias(biases):
    bias = jnp.concatenate(biases, axis=0)
    neg = jnp.full((bias.shape[0], Q_TILE - 1), NEG_INF, F32)
    table = jnp.concatenate([bias[:, :1], neg, bias[:, :0:-1]], axis=1)
    n = table.shape[0]
    return pl.pallas_call(
        _bias_tiles_kernel,
        out_shape=jax.ShapeDtypeStruct((n, Q_TILE, 2 * Q_TILE), F32),
        name="bias_tiles",
    )(table)


def _sample_bias(biases, n_new, tp):
    t = np.arange(tp)[:, None]
    real = t < n_new
    past = []
    for g, bias in enumerate(biases):
        delta = WINDOWS[g] + t - np.arange(WINDOWS[g])[None, :]
        k = delta // DILATIONS[g]
        valid = (delta % DILATIONS[g] == 0) & (k <= A_KEYS - 1)
        tab = jnp.where(jnp.asarray(valid), bias[:, np.clip(k, 0, A_KEYS - 1)], NEG_INF)
        past.append(jnp.where(jnp.asarray(real)[None], tab, 0.0))
    s = np.arange(n_new)[None, :, None]
    g_idx = np.arange(N_GROUPS)[:, None, None]
    ok = (s <= t[None, :, 0]) & real[None, :, 0] & ((g_idx == 0) | (s == 0))
    vals = jnp.stack([bias[:, :n_new].T for bias in biases])
    new = jnp.where(jnp.asarray(ok)[..., None], vals[:, :, None, :], NEG_INF)
    return past, jnp.pad(new, ((0, 0), (0, 0), (0, 0), (0, 128 - GROUP_HEADS)))


def _pieces_a():
    q = [((g * GROUP_W, (g + 1) * GROUP_W),) for g in range(N_GROUPS)]
    kv = [((A_Q + g * GROUP_W, A_Q + (g + 1) * GROUP_W), (2 * A_Q + g * GROUP_W, 2 * A_Q + (g + 1) * GROUP_W))
          for g in range(N_GROUPS)]
    qmem = ((3 * A_Q, 3 * A_Q + MEM_W),)
    gate = ((3 * A_Q + MEM_W, 3 * A_Q + MEM_W + GROUP_W + MEM_W),)
    return tuple(q + kv + [qmem, gate])


def _pieces_b():
    return (((0, C_SHIFT),), ((C_SHIFT, C_SHIFT + MEM_W),), ((C_SHIFT + MEM_W, C_SHIFT + MEM_W + RWKV_W + MEM_W),))


def kernel(x_prompt, x_sample, mem_prompt, cache_mem_kv, cache_win0, cache_win1, cache_win2, state_wkv, state_shift, norm_pre, norm_post, norm_mem, w_mem_kv, rel_bias, w_in_a, w_out_a, w_in_b, w_out_b, rwkv_mu, rwkv_w0, rwkv_w_up, rwkv_a0, rwkv_a_up, rwkv_k_k, rwkv_k_a, rwkv_r_k, rwkv_ln_w, rwkv_ln_b):
    bp, tp, d = x_prompt.shape
    bs, ts, _ = x_sample.shape
    tsp = SAMPLE_PAD_T
    xp = x_prompt.reshape(bp * tp, d)
    xs = jnp.pad(x_sample, ((0, 0), (0, tsp - ts), (0, 0))).reshape(bs * tsp, d)
    tm_p, tm_s = 512, bs * tsp
    time_minor = lambda c: jnp.moveaxis(c, -4, -1)
    caches = [time_minor(c[0]) for c in (cache_win0, cache_win1, cache_win2)]
    mkv_s = time_minor(cache_mem_kv).reshape(cache_mem_kv.shape[0], bs, 2 * MEM_W, N_MEM)
    mkv_p = _mem_kv(mem_prompt, norm_mem, w_mem_kv)

    biases = [_group_bias(rel_bias, g) for g in range(N_GROUPS)]
    outs_p = _norm_proj(xp, norm_pre[0], w_in_a[0], _pieces_a(), tm_p, transposed=(3, 4, 5), rows_per_batch=tp)
    outs_s = _norm_proj(xs, norm_pre[0], w_in_a[0], _pieces_a(), tm_s)
    q_p, kv_p, qmem_p, gate_p, kvt_p = outs_p[0:3], outs_p[3:6], outs_p[6], outs_p[7], outs_p[8:11]
    q_s, kv_s, qmem_s, gate_s = outs_s[0:3], outs_s[3:6], outs_s[6], outs_s[7]

    bias_tiles = _prompt_bias(biases)
    merged = [_dil_attn(q_p[g].reshape(bp, tp, GROUP_W), kv_p[g].reshape(bp, tp, 2 * GROUP_W), bias_tiles, g)
              for g in range(N_GROUPS)]
    merged = [(a.reshape(bp * tp, GROUP_W), s.reshape(bp * tp, 128)) for a, s in merged]
    omem_p = _cross_attn(qmem_p.reshape(bp, tp, MEM_W), mkv_p, 0, 512)
    xp1 = _post(xp, merged, omem_p.reshape(bp * tp, MEM_W), gate_p, w_out_a[0], norm_post[0], tm_p)

    bias_past, bias_new = _sample_bias(biases, ts, tsp)
    q_s_all = jnp.concatenate(q_s, axis=-1).reshape(bs, tsp, A_Q)
    kvn = [a.reshape(bs, tsp, 2 * GROUP_W) for a in kv_s]
    o_s = _sample_attn(q_s_all, kvn, caches, bias_past, bias_new, ts)
    omem_s = _cross_attn(qmem_s.reshape(bs, tsp, MEM_W), mkv_s, 0, tsp)
    xs1 = _post(xs, o_s.reshape(bs * tsp, GROUP_W), omem_s.reshape(bs * tsp, MEM_W), gate_s, w_out_a[0],
                norm_post[0], tm_s)

    cols_p, qmem_p, gate_p = _norm_proj(xp1, norm_pre[1], w_in_b[0], _pieces_b(), tm_p)
    cols_s, qmem_s, gate_s = _norm_proj(xs1, norm_pre[1], w_in_b[0], _pieces_b(), tm_s)
    prm = dict(mu=rwkv_mu[0], w0=rwkv_w0[0], w_up=rwkv_w_up[0], a0=rwkv_a0[0], a_up=rwkv_a_up[0], k_k=rwkv_k_k[0],
               k_a=rwkv_k_a[0], r_k=rwkv_r_k[0], ln_w=rwkv_ln_w[0], ln_b=rwkv_ln_b[0])
    y_p, wkv_p, sh_p = _rwkv_scan(cols_p.reshape(bp, tp, C_SHIFT), jnp.zeros((bp, C_SHIFT), F32),
                                  jnp.zeros((bp, 2 * RWKV_PAIRS, HEAD_DIM, HEAD_DIM), F32), prm, 256, tp)
    cols_s_pad = jnp.pad(cols_s.reshape(bs, tsp, C_SHIFT), ((0, 0), (0, SCAN_CHUNK - tsp), (0, 0)))
    y_s, wkv_s, sh_s = _rwkv_scan(cols_s_pad, state_shift[0], state_wkv[0], prm, SCAN_CHUNK, ts)
    y_s = y_s[:, :tsp].reshape(bs * tsp, RWKV_W)
    omem_p = _cross_attn(qmem_p.reshape(bp, tp, MEM_W), mkv_p, 1, 512)
    omem_s = _cross_attn(qmem_s.reshape(bs, tsp, MEM_W), mkv_s, 1, tsp)
    xp2 = _post(xp1, y_p.reshape(bp * tp, RWKV_W), omem_p.reshape(bp * tp, MEM_W), gate_p, w_out_b[0],
                norm_post[1], tm_p)
    xs2 = _post(xs1, y_s, omem_s.reshape(bs * tsp, MEM_W), gate_s, w_out_b[0], norm_post[1], tm_s)

    kv_shape = (2, GROUP_HEADS, HEAD_DIM)
    time_major = lambda c: jnp.moveaxis(c, -1, -4)
    new_mem_kv = time_major(mkv_p.reshape(mkv_p.shape[0], bp, 2, MEM_W // HEAD_DIM, HEAD_DIM, N_MEM))
    win_p = [time_major(kvt_p[g].reshape(bp, *kv_shape, tp)[..., tp - min(WINDOWS[g], tp):])[None]
             for g in range(N_GROUPS)]
    win_s = [kv_s[g].reshape(bs, tsp, *kv_shape)[None, :, :ts] for g in range(N_GROUPS)]
    return (xp2.reshape(bp, tp, d), xs2.reshape(bs, tsp, d)[:, :ts], new_mem_kv,
            win_p[0], win_p[1], win_p[2], win_s[0], win_s[1], win_s[2],
            wkv_p[None], wkv_s[None], sh_p[None], sh_s[None])
```

```python
import functools
import math

import numpy as np
import jax
import jax.numpy as jnp
from jax import lax
from jax.experimental import pallas as pl
from jax.experimental.pallas import tpu as pltpu

F32 = jnp.float32
BF16 = jnp.bfloat16

D_MODEL = 1024
HEAD_DIM = 64
N_GROUPS = 3
GROUP_HEADS = 4
WINDOWS = (128, 512, 2048)
DILATIONS = (1, 4, 16)
A_HEADS = N_GROUPS * GROUP_HEADS
A_KEYS = 129
GROUP_W = GROUP_HEADS * HEAD_DIM
A_Q = A_HEADS * HEAD_DIM
N_MEM = 256
MEM_W = 256
RWKV_W = 768
RWKV_PAIRS = RWKV_W // 128
LORA = 64
C_SHIFT = 3 * RWKV_W + 2 * LORA
N_BUCKETS = 32
MAX_DISTANCE = WINDOWS[-1]
RMS_EPS = 1e-6
GN_EPS = 64e-5
NEG_INF = -1e30
SCALE = HEAD_DIM ** -0.5

Q_TILE = 128
DIL_SEGMENTS = 4
SCAN_CHUNK = 64
SAMPLE_PAD_T = 8
VMEM_LIMIT = 56 * 1024 * 1024
NN =(((1,), (0,)), ((), ()))
NT = (((1,), (1,)), ((), ()))
TN = (((0,), (0,)), ((), ()))


def _params(*sem):
    return pltpu.CompilerParams(dimension_semantics=sem, vmem_limit_bytes=VMEM_LIMIT)


def _dot(a, b):
    return jnp.dot(a, b, preferred_element_type=F32)


def _dot_nt(a, b):
    return lax.dot_general(a, b, NT, preferred_element_type=F32)


def _split(x, n):
    if x.dtype == BF16:
        return [x]
    parts, rem = [], x
    for i in range(n):
        parts.append(rem.astype(BF16))
        if i + 1 < n:
            rem = rem - parts[-1].astype(F32)
    return parts


def _mm(a, b, dims=NN, pa=1, pb=1):
    pas, pbs = _split(a, pa), _split(b, pb)
    order = max(len(pas), len(pbs))
    out = None
    for i, ai in enumerate(pas):
        for j, bj in enumerate(pbs):
            if i + j < order:
                term = lax.dot_general(ai, bj, dims, preferred_element_type=F32)
                out = term if out is None else out + term
    return out


def _rms(x, g):
    return x * lax.rsqrt(jnp.mean(x * x, axis=-1, keepdims=True) + RMS_EPS) * g


def _norm_proj_kernel(x_ref, g_ref, w_ref, *out_refs, pieces, transposed):
    xn = _rms(x_ref[...], g_ref[...]).astype(BF16)
    t_refs = iter(out_refs[len(pieces):])
    for idx, (out_ref, cols) in enumerate(zip(out_refs, pieces)):
        t_ref = next(t_refs) if idx in transposed else None
        off = 0
        for c0, c1 in cols:
            res = _dot(xn, w_ref[:, c0:c1])
            out_ref[:, off:off + c1 - c0] = res
            if t_ref is not None:
                t_ref[off:off + c1 - c0, :] = res.T
            off += c1 - c0


def _norm_proj(x, g, w, pieces, tm, transposed=(), rows_per_batch=None):
    m, d = x.shape
    n = w.shape[1]
    widths = [sum(c1 - c0 for c0, c1 in cols) for cols in pieces]
    out_specs = [pl.BlockSpec((tm, wd), lambda i: (i, 0)) for wd in widths]
    out_shape = [jax.ShapeDtypeStruct((m, wd), F32) for wd in widths]
    if transposed:
        tiles = rows_per_batch // tm
        out_specs += [pl.BlockSpec((None, widths[idx], tm), lambda i: (i // tiles, 0, i % tiles)) for idx in transposed]
        out_shape += [jax.ShapeDtypeStruct((m // rows_per_batch, widths[idx], rows_per_batch), F32)
                      for idx in transposed]
    return pl.pallas_call(
        functools.partial(_norm_proj_kernel, pieces=pieces, transposed=tuple(transposed)),
        grid=(m // tm,),
        in_specs=[pl.BlockSpec((tm, d), lambda i: (i, 0)),
                  pl.BlockSpec((1, d), lambda i: (0, 0)),
                  pl.BlockSpec((d, n), lambda i: (0, 0))],
        out_specs=out_specs,
        out_shape=out_shape,
        compiler_params=_params("parallel"),
        name="norm_proj",
    )(x, g.reshape(1, d), w.astype(BF16))


def _mem_kv_kernel(x_ref, g_ref, wt_ref, o_ref):
    xn = _rms(x_ref[...], g_ref[...]).astype(BF16)
    o_ref[...] = _dot_nt(wt_ref[...], xn)


def _mem_kv(mem, g, w):
    b, n, d = mem.shape
    nl, _, wd = w.shape
    return pl.pallas_call(
        _mem_kv_kernel,
        grid=(nl, b),
        in_specs=[pl.BlockSpec((None, n, d), lambda l, bi: (bi, 0, 0)),
                  pl.BlockSpec((None, 1, d), lambda l, bi: (l, 0, 0)),
                  pl.BlockSpec((None, wd, d), lambda l, bi: (l, 0, 0))],
        out_specs=pl.BlockSpec((None, None, wd, n), lambda l, bi: (l, bi, 0, 0)),
        out_shape=jax.ShapeDtypeStruct((nl, b, wd, n), F32),
        compiler_params=_params("parallel", "parallel"),
        name="mem_kv",
    )(mem, g.reshape(nl, 1, d), jnp.swapaxes(w, 1, 2).astype(BF16))


def _dil_attn_kernel(q_ref, kvc_ref, kvp_ref, bias_ref, acc_ref, st_ref, *, nr, tiles, prev_block):
    i = pl.program_id(2)
    q = (q_ref[...] * SCALE).astype(BF16)
    kvc = kvc_ref[...].astype(BF16)
    kvp = kvp_ref[...].astype(BF16) if prev_block else None
    chains = [(rr, tt, h) for rr in range(nr) for tt in range(tiles) for h in range(GROUP_HEADS)]
    n = len(chains)
    rows = lambda tt: slice(tt * Q_TILE, (tt + 1) * Q_TILE)

    def keys(rr, tt, h, off):
        lanes = slice(rr * 2 * GROUP_W + off + h * HEAD_DIM, rr * 2 * GROUP_W + off + (h + 1) * HEAD_DIM)
        return kvp[:, lanes] if tt < 0 else kvc[rows(tt), lanes]

    has_prev = [tt > 0 or prev_block for _, tt, _ in chains]
    qh = [q[rows(tt), rr * GROUP_W + h * HEAD_DIM:rr * GROUP_W + (h + 1) * HEAD_DIM] for rr, tt, h in chains]
    lc = [_dot_nt(qh[c], keys(rr, tt, h, 0)) + bias_ref[h, :, :Q_TILE] for c, (rr, tt, h) in enumerate(chains)]
    lp = [None] * n
    for c, (rr, tt, h) in enumerate(chains):
        if has_prev[c]:
            z = _dot_nt(qh[c], keys(rr, tt - 1, h, 0)) + bias_ref[h, :, Q_TILE:]
            lp[c] = jnp.where(i > 0, z, NEG_INF) if tt == 0 else z
    m = [jnp.max(z, axis=-1, keepdims=True) for z in lc]
    m = [jnp.maximum(m[c], jnp.max(lp[c], axis=-1, keepdims=True)) if has_prev[c] else m[c] for c in range(n)]
    pc = [jnp.exp(lc[c] - m[c]) for c in range(n)]
    pp = [jnp.exp(lp[c] - m[c]) if has_prev[c] else None for c in range(n)]
    den = [jnp.sum(z, axis=-1, keepdims=True) for z in pc]
    den = [den[c] + jnp.sum(pp[c], axis=-1, keepdims=True) if has_prev[c] else den[c] for c in range(n)]
    o = [_dot(pc[c].astype(BF16), keys(rr, tt, h, GROUP_W)) for c, (rr, tt, h) in enumerate(chains)]
    o = [o[c] + _dot(pp[c].astype(BF16), keys(rr, tt - 1, h, GROUP_W)) if has_prev[c] else o[c]
         for c, (rr, tt, h) in enumerate(chains)]
    lane = lax.broadcasted_iota(jnp.int32, (Q_TILE, 128), 1)
    st = None
    for c, (rr, tt, h) in enumerate(chains):
        acc_ref[rows(tt), rr * GROUP_W + h * HEAD_DIM:rr * GROUP_W + (h + 1) * HEAD_DIM] = o[c]
        st = jnp.zeros((Q_TILE, 128), F32) if h == 0 else st
        st = jnp.where(lane == h, m[c], st)
        st = jnp.where(lane == GROUP_HEADS + h, den[c], st)
        if h == GROUP_HEADS - 1:
            st_ref[rows(tt), rr * 128:(rr + 1) * 128] = st


def _dil_attn(q, kv, bias_tiles, g):
    dil = DILATIONS[g]
    b, t, _ = q.shape
    ln = t // dil
    tiles = min(ln // Q_TILE, DIL_SEGMENTS)
    nr = min(dil, DIL_SEGMENTS // tiles)
    nblk = ln // (tiles * Q_TILE)
    qv = q.reshape(b, ln, dil * GROUP_W)
    kvv = kv.reshape(b, ln, dil * 2 * GROUP_W)
    acc, st = pl.pallas_call(
        functools.partial(_dil_attn_kernel, nr=nr, tiles=tiles, prev_block=nblk > 1),
        grid=(b, dil // nr, nblk),
        in_specs=[pl.BlockSpec((None, tiles * Q_TILE, nr * GROUP_W), lambda bi, r, i: (bi, i, r)),
                  pl.BlockSpec((None, tiles * Q_TILE, nr * 2 * GROUP_W), lambda bi, r, i: (bi, i, r)),
                  pl.BlockSpec((None, Q_TILE, nr * 2 * GROUP_W),
                               lambda bi, r, i: (bi, jnp.maximum(tiles * i - 1, 0), r)),
                  pl.BlockSpec((GROUP_HEADS, Q_TILE, 2 * Q_TILE), lambda bi, r, i: (g, 0, 0))],
        out_specs=[pl.BlockSpec((None, tiles * Q_TILE, nr * GROUP_W), lambda bi, r, i: (bi, i, r)),
                   pl.BlockSpec((None, tiles * Q_TILE, nr * 128), lambda bi, r, i: (bi, i, r))],
        out_shape=[jax.ShapeDtypeStruct((b, ln, dil * GROUP_W), F32),
                   jax.ShapeDtypeStruct((b, ln, dil * 128), F32)],
        compiler_params=_params("parallel", "parallel", "arbitrary"),
        name=f"dil_attn_d{dil}",
    )(qv, kvv, kvv, bias_tiles)
    return acc.reshape(b, t, GROUP_W), st.reshape(b, t, 128)


def _sample_attn_kernel(q_ref, kn0_ref, kn1_ref, kn2_ref, c0_ref, c1_ref, c2_ref, b0_ref, b1_ref, b2_ref,
                        bnew_ref, o_ref, *, n_new):
    q = (q_ref[...] * SCALE).astype(BF16)
    qf = q.astype(F32)
    groups = ((kn0_ref, c0_ref, b0_ref), (kn1_ref, c1_ref, b1_ref), (kn2_ref, c2_ref, b2_ref))
    new_rows = []
    for g, (kn_ref, _, _) in enumerate(groups):
        kn = kn_ref[...]
        shifted = [kn] + [pltpu.roll(kn, s, axis=0) for s in range(1, n_new if g == 0 else 1)]
        new_rows.append([z.astype(BF16).astype(F32) for z in shifted])
    for h in range(GROUP_HEADS):
        ks, vs = slice(h * HEAD_DIM, (h + 1) * HEAD_DIM), slice(GROUP_W + h * HEAD_DIM, GROUP_W + (h + 1) * HEAD_DIM)
        ms, ss, os_ = [], [], []
        for g, (_, c_ref, b_ref) in enumerate(groups):
            qs = slice(g * GROUP_W + h * HEAD_DIM, g * GROUP_W + (h + 1) * HEAD_DIM)
            lp = _dot(q[:, qs], c_ref[0, h].astype(BF16)) + b_ref[h]
            m = jnp.max(lp, axis=-1, keepdims=True)
            lns = []
            for s, rows in enumerate(new_rows[g]):
                ln = jnp.sum(qf[:, qs] * rows[:, ks], axis=-1, keepdims=True) + bnew_ref[g, s][:, h:h + 1]
                lns.append(ln)
                m = jnp.maximum(m, ln)
            pp = jnp.exp(lp - m)
            den = jnp.sum(pp, axis=-1, keepdims=True)
            o = _dot_nt(pp.astype(BF16), c_ref[1, h].astype(BF16))
            for ln, rows in zip(lns, new_rows[g]):
                pn = jnp.exp(ln - m)
                den = den + pn
                o = o + pn.astype(BF16).astype(F32) * rows[:, vs]
            ms.append(m)
            ss.append(den)
            os_.append(o / den)
        mx = jnp.maximum(jnp.maximum(ms[0], ms[1]), ms[2])
        wts = [jnp.exp(ms[g] - mx) * ss[g] for g in range(N_GROUPS)]
        num = wts[0] * os_[0] + wts[1] * os_[1] + wts[2] * os_[2]
        o_ref[:, ks] = num / (wts[0] + wts[1] + wts[2])


def _sample_attn(q, kvn, caches, bias_past, bias_new, n_new):
    b, tp, _ = q.shape
    new_spec = pl.BlockSpec((None, tp, 2 * GROUP_W), lambda bi: (bi, 0, 0))
    cache_spec = lambda w: pl.BlockSpec((None, 2, GROUP_HEADS, HEAD_DIM, w), lambda bi: (bi, 0, 0, 0, 0))
    bias_spec = lambda w: pl.BlockSpec((GROUP_HEADS, tp, w), lambda bi: (0, 0, 0))
    return pl.pallas_call(
        functools.partial(_sample_attn_kernel, n_new=n_new),
        grid=(b,),
        in_specs=[pl.BlockSpec((None, tp, A_Q), lambda bi: (bi, 0, 0)), new_spec, new_spec, new_spec]
                 + [cache_spec(w) for w in WINDOWS] + [bias_spec(w) for w in WINDOWS]
                 + [pl.BlockSpec((N_GROUPS, n_new, tp, 128), lambda bi: (0, 0, 0, 0))],
        out_specs=pl.BlockSpec((None, tp, GROUP_W), lambda bi: (bi, 0, 0)),
        out_shape=jax.ShapeDtypeStruct((b, tp, GROUP_W), F32),
        compiler_params=_params("parallel"),
        name="sample_attn",
    )(q, kvn[0], kvn[1], kvn[2], *caches, *bias_past, bias_new)


def _cross_attn_kernel(q_ref, kvt_ref, o_ref):
    q = (q_ref[...] * SCALE).astype(BF16)
    kvt = kvt_ref[...].astype(BF16)
    for h in range(MEM_W // HEAD_DIM):
        ks, vs = slice(h * HEAD_DIM, (h + 1) * HEAD_DIM), slice(MEM_W + h * HEAD_DIM, MEM_W + (h + 1) * HEAD_DIM)
        lg = _dot(q[:, ks], kvt[ks, :])
        p = jnp.exp(lg - jnp.max(lg, axis=-1, keepdims=True))
        o = _dot_nt(p.astype(BF16), kvt[vs, :])
        o_ref[:, ks] = o / jnp.sum(p, axis=-1, keepdims=True)


def _cross_attn(q, kvt, layer, tq):
    b, t, _ = q.shape
    return pl.pallas_call(
        _cross_attn_kernel,
        grid=(b, t // tq),
        in_specs=[pl.BlockSpec((None, tq, MEM_W), lambda bi, i: (bi, i, 0)),
                  pl.BlockSpec((None, None, 2 * MEM_W, N_MEM), lambda bi, i: (layer, bi, 0, 0))],
        out_specs=pl.BlockSpec((None, tq, MEM_W), lambda bi, i: (bi, i, 0)),
        out_shape=jax.ShapeDtypeStruct((b, t, MEM_W), F32),
        compiler_params=_params("parallel", "parallel"),
        name="cross_attn",
    )(q, kvt)


def _rwkv_kernel(cols_ref, shift_ref, s0_ref, mu_ref, w0_ref, wup_ref, a0_ref, aup_ref, kk_ref, ka_ref, rk_ref,
                 lnw_ref, lnb_ref, y_ref, sout_ref, shout_ref,
                 carry, state, r_s, k_s, v_s, a_s, b_s, lw_s, *, tt, chunk, t_valid):
    j = pl.program_id(1)

    @pl.when(j == 0)
    def _():
        carry[...] = shift_ref[...]
        state[...] = s0_ref[...]

    cols = cols_ref[...]
    row = lax.broadcasted_iota(jnp.int32, (tt, 1), 0)
    prev = jnp.where(row == 0, carry[...], pltpu.roll(cols, 1, axis=0))
    carry[...] = cols[t_valid - 1:t_valid, :]
    xs = cols + mu_ref[...] * (prev - cols)
    r = xs[:, :RWKV_W]
    k = xs[:, RWKV_W:2 * RWKV_W]
    v = xs[:, 2 * RWKV_W:3 * RWKV_W]
    wd = xs[:, 3 * RWKV_W:3 * RWKV_W + LORA]
    ad = xs[:, 3 * RWKV_W + LORA:]
    w_log = -jax.nn.softplus(-(w0_ref[...] + _dot(jnp.tanh(wd).astype(BF16), wup_ref[...]))) - 0.5
    log_decay = -jnp.exp(w_log)
    a = jax.nn.sigmoid(a0_ref[...] + _dot(ad.astype(BF16), aup_ref[...]))

    blockdiag = (lax.broadcasted_iota(jnp.int32, (128, 128), 0) // HEAD_DIM
                 == lax.broadcasted_iota(jnp.int32, (128, 128), 1) // HEAD_DIM)
    seg = blockdiag.astype(BF16)

    kk = k * kk_ref[...]
    kk_sq = kk * kk
    k2 = k * (1.0 + (a - 1.0) * ka_ref[...])
    live = row < t_valid
    for p in range(RWKV_PAIRS):
        ps = slice(p * 128, (p + 1) * 128)
        nrm = jnp.maximum(jnp.sqrt(_mm(kk_sq[:, ps], seg, pa=2)), 1e-12)
        kkn = kk[:, ps] / nrm
        if t_valid < tt:
            zero = jnp.zeros((tt, 128), F32)
            r_s[:, ps] = r[:, ps]
            k_s[:, ps] = jnp.where(live, k2[:, ps], zero)
            v_s[:, ps] = jnp.where(live, v[:, ps], zero)
            a_s[:, ps] = jnp.where(live, -kkn, zero)
            b_s[:, ps] = jnp.where(live, kkn * a[:, ps], zero)
            lw_s[:, ps] = jnp.where(live, log_decay[:, ps], zero)
        else:
            r_s[:, ps] = r[:, ps]
            k_s[:, ps] = k2[:, ps]
            v_s[:, ps] = v[:, ps]
            a_s[:, ps] = -kkn
            b_s[:, ps] = kkn * a[:, ps]
            lw_s[:, ps] = log_decay[:, ps]

    ci = lax.broadcasted_iota(jnp.int32, (chunk, chunk), 0)
    cj = lax.broadcasted_iota(jnp.int32, (chunk, chunk), 1)
    tri_incl = (ci >= cj).astype(BF16)
    lo_lane = lax.broadcasted_iota(jnp.int32, (chunk, 128), 1) < HEAD_DIM
    levels = int(math.log2(chunk))
    n2 = 2 * chunk
    ri = lax.broadcasted_iota(jnp.int32, (n2, n2), 0)
    rj = lax.broadcasted_iota(jnp.int32, (n2, n2), 1)
    same = ri // chunk == rj // chunk
    strict = jnp.logical_and(same, ri > rj)
    incl = jnp.logical_and(same, ri >= rj)
    own = lax.broadcasted_iota(jnp.int32, (n2, 128), 0) // chunk == lax.broadcasted_iota(jnp.int32, (n2, 128), 1) // HEAD_DIM
    dup = lambda z: jnp.concatenate([z, z], axis=0)
    cat = jnp.concatenate

    def chunk_body(c, _):
        rows = pl.ds(pl.multiple_of(c * chunk, chunk), chunk)
        cum_all = _mm(tri_incl, lw_s[rows, :], pb=3)
        pairs = range(RWKV_PAIRS)
        sl = [slice(p * 128, (p + 1) * 128) for p in pairs]
        cum = [cum_all[:, s] for s in sl]
        p_incl = [jnp.exp(z) for z in cum]
        p_inv = [jnp.exp(-z) for z in cum]
        rr = [r_s[rows, s] for s in sl]
        kc = [k_s[rows, s] for s in sl]
        vc = [v_s[rows, s] for s in sl]
        at2 = [jnp.where(own, dup(a_s[rows, sl[p]] * jnp.exp(cum[p] - lw_s[rows, sl[p]])), 0.0) for p in pairs]
        rt2 = [jnp.where(own, dup(rr[p] * p_incl[p]), 0.0) for p in pairs]
        bt = [(b_s[rows, sl[p]] * p_inv[p]).astype(BF16) for p in pairs]
        kt = [(kc[p] * p_inv[p]).astype(BF16) for p in pairs]
        vb = [z.astype(BF16) for z in vc]
        v2 = [dup(z) for z in vb]
        g = [_mm(cat([at2[p], rt2[p]], 0).astype(BF16), cat([dup(bt[p]), dup(kt[p])], 0), NT) for p in pairs]
        a_ak = [jnp.where(strict, z[:n2, n2:], 0.0).astype(BF16) for z in g]
        apow = [jnp.where(strict, z[:n2, :n2], 0.0).astype(BF16) for z in g]
        a_r = [cat([jnp.where(incl, z[n2:, :n2], 0.0), jnp.where(incl, z[n2:, n2:], 0.0)], 1).astype(BF16)
               for z in g]
        akv = [_mm(a_ak[p], v2[p]) for p in pairs]
        sol = [cat([at2[p], akv[p]], axis=1) for p in pairs]
        for lvl in range(levels):
            sol = [sol[p] + _mm(apow[p], sol[p].astype(BF16)) for p in pairs]
            if lvl + 1 < levels:
                apow = [_mm(z, z).astype(BF16) for z in apow]
        ws = [z[:, :128] for z in sol]
        u0s = [jnp.where(own, z[:, 128:], 0.0) for z in sol]
        zeros2 = jnp.zeros((n2, 128), BF16)
        qy = [_mm(a_r[p], cat([cat([ws[p], u0s[p]], 1).astype(BF16), cat([zeros2, v2[p]], 1)], 0)) for p in pairs]
        s_prev = [state[p] for p in pairs]
        s_b = [z.astype(BF16) for z in s_prev]
        ys = [_mm((rt2[p] + qy[p][:, :128]).astype(BF16), s_b[p], NT) + qy[p][:, 128:] for p in pairs]
        y = [jnp.where(lo_lane, z[:chunk], z[chunk:]) for z in ys]
        zeros1 = jnp.zeros((chunk, 128), BF16)
        lhs = [cat([cat([ws[p][:chunk] + ws[p][chunk:], u0s[p][:chunk] + u0s[p][chunk:]], 1).astype(BF16),
                    cat([zeros1, vb[p]], 1)], 0) for p in pairs]
        wbn = [_mm(lhs[p], cat([bt[p], kt[p]], 0), TN) for p in pairs]
        sw = [_mm(s_b[p], wbn[p][:128].astype(BF16)) for p in pairs]
        for p in pairs:
            s_new = (s_prev[p] + sw[p] + wbn[p][128:]) * p_incl[p][chunk - 1:chunk, :]
            state[p] = jnp.where(blockdiag, s_new, 0.0)

        mb = [_mm(cat([y[p], rr[p] * kc[p] * rk_ref[:, sl[p]]], 0), seg, pa=2) for p in pairs]
        dlt = [y[p] - mb[p][:chunk] * (1.0 / HEAD_DIM) for p in pairs]
        var = [_mm(z * z, seg, pa=2) * (1.0 / HEAD_DIM) for z in dlt]
        for p in pairs:
            yn = dlt[p] * lax.rsqrt(var[p] + GN_EPS) * lnw_ref[:, sl[p]] + lnb_ref[:, sl[p]]
            y_ref[rows, sl[p]] = yn + mb[p][chunk:] * vc[p]
        return 0

    n_chunks = tt // chunk
    lax.fori_loop(0, n_chunks, chunk_body, 0, unroll=2 if n_chunks % 2 == 0 else 1)

    @pl.when(j == pl.num_programs(1) - 1)
    def _():
        sout_ref[...] = state[...]
        shout_ref[...] = carry[...]


def _rwkv_scan(cols, shift_prev, s0, prm, tt, t_valid):
    b, t, _ = cols.shape
    s = s0.reshape(b, RWKV_PAIRS, 2, HEAD_DIM, HEAD_DIM)
    z = jnp.zeros_like(s[:, :, 0])
    s_bd = jnp.concatenate([jnp.concatenate([s[:, :, 0], z], -1), jnp.concatenate([z, s[:, :, 1]], -1)], -2)
    row = lambda n: pl.BlockSpec((1, n), lambda bi, j: (0, 0))
    vec = lambda a: a.reshape(1, -1)
    y, s_out, sh_out = pl.pallas_call(
        functools.partial(_rwkv_kernel, tt=tt, chunk=SCAN_CHUNK, t_valid=min(t_valid, tt)),
        grid=(b, t // tt),
        in_specs=[pl.BlockSpec((None, tt, C_SHIFT), lambda bi, j: (bi, j, 0)),
                  pl.BlockSpec((None, 1, C_SHIFT), lambda bi, j: (bi, 0, 0)),
                  pl.BlockSpec((None, RWKV_PAIRS, 128, 128), lambda bi, j: (bi, 0, 0, 0)),
                  row(C_SHIFT), row(RWKV_W),
                  pl.BlockSpec((LORA, RWKV_W), lambda bi, j: (0, 0)),
                  row(RWKV_W),
                  pl.BlockSpec((LORA, RWKV_W), lambda bi, j: (0, 0)),
                  row(RWKV_W), row(RWKV_W), row(RWKV_W), row(RWKV_W), row(RWKV_W)],
        out_specs=[pl.BlockSpec((None, tt, RWKV_W), lambda bi, j: (bi, j, 0)),
                   pl.BlockSpec((None, RWKV_PAIRS, 128, 128), lambda bi, j: (bi, 0, 0, 0)),
                   pl.BlockSpec((None, 1, C_SHIFT), lambda bi, j: (bi, 0, 0))],
        out_shape=[jax.ShapeDtypeStruct((b, t, RWKV_W), F32),
                   jax.ShapeDtypeStruct((b, RWKV_PAIRS, 128, 128), F32),
                   jax.ShapeDtypeStruct((b, 1, C_SHIFT), F32)],
        scratch_shapes=[pltpu.VMEM((1, C_SHIFT), F32), pltpu.VMEM((RWKV_PAIRS, 128, 128), F32)]
                       + [pltpu.VMEM((tt, RWKV_W), F32)] * 6,
        compiler_params=_params("parallel", "arbitrary"),
        name="rwkv_scan",
    )(cols, shift_prev.reshape(b, 1, C_SHIFT), s_bd, vec(prm["mu"]), vec(prm["w0"]), prm["w_up"].astype(BF16),
      vec(prm["a0"]), prm["a_up"].astype(BF16), vec(prm["k_k"]), vec(prm["k_a"]), vec(prm["r_k"]),
      vec(prm["ln_w"]), vec(prm["ln_b"]))
    s_fin = jnp.stack([s_out[:, :, :HEAD_DIM, :HEAD_DIM], s_out[:, :, HEAD_DIM:, HEAD_DIM:]], axis=2)
    return y, s_fin.reshape(b, 2 * RWKV_PAIRS, HEAD_DIM, HEAD_DIM), sh_out.reshape(b, C_SHIFT)


def _post_kernel(*refs, n_merge, mix_w):
    x_ref, omem_ref, gate_ref, w_ref, g_ref = refs[:5]
    mix_refs = refs[5:-1]
    y_ref = refs[-1]
    if n_merge:
        er = lax.broadcasted_iota(jnp.int32, (128, 2 * GROUP_W), 0)
        ec = lax.broadcasted_iota(jnp.int32, (128, 2 * GROUP_W), 1)
        expand = (er == ec // HEAD_DIM).astype(BF16)
        ms, ss, os_ = [], [], []
        for g in range(n_merge):
            st = _mm(mix_refs[2 * g + 1][...], expand, pa=3)
            ms.append(st[:, :GROUP_W])
            ss.append(st[:, GROUP_W:])
            os_.append(mix_refs[2 * g][...] / st[:, GROUP_W:])
        mx = functools.reduce(jnp.maximum, ms)
        wts = [jnp.exp(m - mx) * s for m, s in zip(ms, ss)]
        mix = sum(w * o for w, o in zip(wts, os_)) / sum(wts)
    else:
        mix = mix_refs[0][...]
    gate = gate_ref[...]
    act = gate * jax.nn.sigmoid(gate)
    h1 = (mix * act[:, :mix_w]).astype(BF16)
    h2 = (omem_ref[...] * act[:, mix_w:]).astype(BF16)
    out = _dot(h1, w_ref[:mix_w, :]) + _dot(h2, w_ref[mix_w:, :])
    y_ref[...] = x_ref[...] + _rms(out, g_ref[...])


def _post(x, mix, o_mem, gate, w_out, g_post, tm):
    m, d = x.shape
    merge = isinstance(mix, (list, tuple))
    mix_arrays = [a for pair in mix for a in pair] if merge else [mix]
    mix_w = GROUP_W if merge else mix.shape[1]
    gw = gate.shape[1]
    rows = lambda wd: pl.BlockSpec((tm, wd), lambda i: (i, 0))
    return pl.pallas_call(
        functools.partial(_post_kernel, n_merge=len(mix) if merge else 0, mix_w=mix_w),
        grid=(m // tm,),
        in_specs=[rows(d), rows(MEM_W), rows(gw),
                  pl.BlockSpec((gw, d), lambda i: (0, 0)),
                  pl.BlockSpec((1, d), lambda i: (0, 0))] + [rows(a.shape[1]) for a in mix_arrays],
        out_specs=rows(d),
        out_shape=jax.ShapeDtypeStruct((m, d), F32),
        compiler_params=_params("parallel"),
        name="post",
    )(x, o_mem, gate, w_out.astype(BF16), g_post.reshape(1, d), *mix_arrays)


def _t5_bucket(dist):
    max_exact = N_BUCKETS // 2
    d = jnp.maximum(dist, 1).astype(F32)
    large = max_exact + (jnp.log(d / max_exact) / math.log(MAX_DISTANCE / max_exact)
                         * (N_BUCKETS - max_exact)).astype(jnp.int32)
    large = jnp.minimum(large, N_BUCKETS - 1)
    return jnp.where(dist < max_exact, dist, large)


def _group_bias(rel_bias, g):
    dist = DILATIONS[g] * jnp.arange(A_KEYS, dtype=jnp.int32)
    bias = rel_bias[_t5_bucket(dist)]
    return bias[:, g * GROUP_HEADS:(g + 1) * GROUP_HEADS].T.astype(F32)


def _bias_tiles_kernel(x_ref, o_ref):
    row = lax.broadcasted_iota(jnp.int32, (Q_TILE, 2 * Q_TILE), 0)
    for h in range(o_ref.shape[0]):
        t = jnp.broadcast_to(x_ref[h:h + 1, :], (Q_TILE, 2 * Q_TILE))
        for bit in range(int(math.log2(Q_TILE))):
            t = jnp.where((row >> bit) & 1 == 1, pltpu.roll(t, 1 << bit, axis=1), t)
        o_ref[h] = t


def _prompt_bias(biases):
    bias = jnp.concatenate(biases, axis=0)
    neg = jnp.full((bias.shape[0], Q_TILE - 1), NEG_INF, F32)
    table = jnp.concatenate([bias[:, :1], neg, bias[:, :0:-1]], axis=1)
    n = table.shape[0]
    return pl.pallas_call(
        _bias_tiles_kernel,
        out_shape=jax.ShapeDtypeStruct((n, Q_TILE, 2 * Q_TILE), F32),
        name="bias_tiles",
    )(table)


def _sample_bias(biases, n_new, tp):
    t = np.arange(tp)[:, None]
    real = t < n_new
    past = []
    for g, bias in enumerate(biases):
        w, dil = WINDOWS[g], DILATIONS[g]
        spread = jnp.concatenate([bias[:, :0:-1, None], jnp.full((GROUP_HEADS, w // dil, dil - 1), NEG_INF, F32)],
                                 axis=-1).reshape(GROUP_HEADS, w)
        rows = [jnp.concatenate([jnp.full((GROUP_HEADS, tt), NEG_INF, F32), spread[:, :w - tt]], axis=-1)
                for tt in range(n_new)] + [jnp.zeros((GROUP_HEADS, w), F32)] * (tp - n_new)
        past.append(jnp.stack(rows, axis=1))
    s = np.arange(n_new)[None, :, None]
    g_idx = np.arange(N_GROUPS)[:, None, None]
    ok = (s <= t[None, :, 0]) & real[None, :, 0] & ((g_idx == 0) | (s == 0))
    vals = jnp.stack([bias[:, :n_new].T for bias in biases])
    new = jnp.where(jnp.asarray(ok)[..., None], vals[:, :, None, :], NEG_INF)
    return past, jnp.pad(new, ((0, 0), (0, 0), (0, 0), (0, 128 - GROUP_HEADS)))


def _pieces_a():
    q = [((g * GROUP_W, (g + 1) * GROUP_W),) for g in range(N_GROUPS)]
    kv = [((A_Q + g * GROUP_W, A_Q + (g + 1) * GROUP_W), (2 * A_Q + g * GROUP_W, 2 * A_Q + (g + 1) * GROUP_W))
          for g in range(N_GROUPS)]
    qmem = ((3 * A_Q, 3 * A_Q + MEM_W),)
    gate = ((3 * A_Q + MEM_W, 3 * A_Q + MEM_W + GROUP_W + MEM_W),)
    return tuple(q + kv + [qmem, gate])


def _pieces_b():
    return (((0, C_SHIFT),), ((C_SHIFT, C_SHIFT + MEM_W),), ((C_SHIFT + MEM_W, C_SHIFT + MEM_W + RWKV_W + MEM_W),))


def kernel(x_prompt, x_sample, mem_prompt, cache_mem_kv, cache_win0, cache_win1, cache_win2, state_wkv, state_shift, norm_pre, norm_post, norm_mem, w_mem_kv, rel_bias, w_in_a, w_out_a, w_in_b, w_out_b, rwkv_mu, rwkv_w0, rwkv_w_up, rwkv_a0, rwkv_a_up, rwkv_k_k, rwkv_k_a, rwkv_r_k, rwkv_ln_w, rwkv_ln_b):
    bp, tp, d = x_prompt.shape
    bs, ts, _ = x_sample.shape
    tsp = SAMPLE_PAD_T
    xp = x_prompt.reshape(bp * tp, d)
    xs = jnp.pad(x_sample, ((0, 0), (0, tsp - ts), (0, 0))).reshape(bs * tsp, d)
    tm_p, tm_s = 512, bs * tsp
    time_minor = lambda c: jnp.moveaxis(c, -4, -1)
    caches = [time_minor(c[0]) for c in (cache_win0, cache_win1, cache_win2)]
    mkv_s = time_minor(cache_mem_kv).reshape(cache_mem_kv.shape[0], bs, 2 * MEM_W, N_MEM)
    mkv_p = _mem_kv(mem_prompt, norm_mem, w_mem_kv)

    biases = [_group_bias(rel_bias, g) for g in range(N_GROUPS)]
    outs_p = _norm_proj(xp, norm_pre[0], w_in_a[0], _pieces_a(), tm_p, transposed=(3, 4, 5), rows_per_batch=tp)
    outs_s = _norm_proj(xs, norm_pre[0], w_in_a[0], _pieces_a(), tm_s)
    q_p, kv_p, qmem_p, gate_p, kvt_p = outs_p[0:3], outs_p[3:6], outs_p[6], outs_p[7], outs_p[8:11]
    q_s, kv_s, qmem_s, gate_s = outs_s[0:3], outs_s[3:6], outs_s[6], outs_s[7]

    bias_tiles = _prompt_bias(biases)
    merged = [_dil_attn(q_p[g].reshape(bp, tp, GROUP_W), kv_p[g].reshape(bp, tp, 2 * GROUP_W), bias_tiles, g)
              for g in range(N_GROUPS)]
    merged = [(a.reshape(bp * tp, GROUP_W), s.reshape(bp * tp, 128)) for a, s in merged]
    omem_p = _cross_attn(qmem_p.reshape(bp, tp, MEM_W), mkv_p, 0, 512)
    xp1 = _post(xp, merged, omem_p.reshape(bp * tp, MEM_W), gate_p, w_out_a[0], norm_post[0], tm_p)

    bias_past, bias_new = _sample_bias(biases, ts, tsp)
    q_s_all = jnp.concatenate(q_s, axis=-1).reshape(bs, tsp, A_Q)
    kvn = [a.reshape(bs, tsp, 2 * GROUP_W) for a in kv_s]
    o_s = _sample_attn(q_s_all, kvn, caches, bias_past, bias_new, ts)
    omem_s = _cross_attn(qmem_s.reshape(bs, tsp, MEM_W), mkv_s, 0, tsp)
    xs1 = _post(xs, o_s.reshape(bs * tsp, GROUP_W), omem_s.reshape(bs * tsp, MEM_W), gate_s, w_out_a[0],
                norm_post[0], tm_s)

    cols_p, qmem_p, gate_p = _norm_proj(xp1, norm_pre[1], w_in_b[0], _pieces_b(), tm_p)
    cols_s, qmem_s, gate_s = _norm_proj(xs1, norm_pre[1], w_in_b[0], _pieces_b(), tm_s)
    prm = dict(mu=rwkv_mu[0], w0=rwkv_w0[0], w_up=rwkv_w_up[0], a0=rwkv_a0[0], a_up=rwkv_a_up[0], k_k=rwkv_k_k[0],
               k_a=rwkv_k_a[0], r_k=rwkv_r_k[0], ln_w=rwkv_ln_w[0], ln_b=rwkv_ln_b[0])
    y_p, wkv_p, sh_p = _rwkv_scan(cols_p.reshape(bp, tp, C_SHIFT), jnp.zeros((bp, C_SHIFT), F32),
                                  jnp.zeros((bp, 2 * RWKV_PAIRS, HEAD_DIM, HEAD_DIM), F32), prm, 256, tp)
    cols_s_pad = jnp.pad(cols_s.reshape(bs, tsp, C_SHIFT), ((0, 0), (0, SCAN_CHUNK - tsp), (0, 0)))
    y_s, wkv_s, sh_s = _rwkv_scan(cols_s_pad, state_shift[0], state_wkv[0], prm, SCAN_CHUNK, ts)
    y_s = y_s[:, :tsp].reshape(bs * tsp, RWKV_W)
    omem_p = _cross_attn(qmem_p.reshape(bp, tp, MEM_W), mkv_p, 1, 512)
    omem_s = _cross_attn(qmem_s.reshape(bs, tsp, MEM_W), mkv_s, 1, tsp)
    xp2 = _post(xp1, y_p.reshape(bp * tp, RWKV_W), omem_p.reshape(bp * tp, MEM_W), gate_p, w_out_b[0],
                norm_post[1], tm_p)
    xs2 = _post(xs1, y_s, omem_s.reshape(bs * tsp, MEM_W), gate_s, w_out_b[0], norm_post[1], tm_s)

    kv_shape = (2, GROUP_HEADS, HEAD_DIM)
    time_major = lambda c: jnp.moveaxis(c, -1, -4)
    new_mem_kv = time_major(mkv_p.reshape(mkv_p.shape[0], bp, 2, MEM_W // HEAD_DIM, HEAD_DIM, N_MEM))
    win_p = [time_major(kvt_p[g].reshape(bp, *kv_shape, tp)[..., tp - min(WINDOWS[g], tp):])[None]
             for g in range(N_GROUPS)]
    win_s = [kv_s[g].reshape(bs, tsp, *kv_shape)[None, :, :ts] for g in range(N_GROUPS)]
    return (xp2.reshape(bp, tp, d), xs2.reshape(bs, tsp, d)[:, :ts], new_mem_kv,
            win_p[0], win_p[1], win_p[2], win_s[0], win_s[1], win_s[2],
            wkv_p[None], wkv_s[None], sh_p[None], sh_s[None])
```

```python
import functools
import math

import numpy as np
import jax
import jax.numpy as jnp
from jax import lax
from jax.experimental import pallas as pl
from jax.experimental.pallas import tpu as pltpu

F32 = jnp.float32
BF16 = jnp.bfloat16

D_MODEL = 1024
HEAD_DIM = 64
N_GROUPS = 3
GROUP_HEADS = 4
WINDOWS = (128, 512, 2048)
DILATIONS = (1, 4, 16)
A_HEADS = N_GROUPS * GROUP_HEADS
A_KEYS = 129
GROUP_W = GROUP_HEADS * HEAD_DIM
A_Q = A_HEADS * HEAD_DIM
N_MEM = 256
MEM_W = 256
RWKV_W = 768
RWKV_PAIRS = RWKV_W // 128
LORA = 64
C_SHIFT = 3 * RWKV_W + 2 * LORA
N_BUCKETS = 32
MAX_DISTANCE = WINDOWS[-1]
RMS_EPS = 1e-6
GN_EPS = 64e-5
NEG_INF = -1e30
SCALE = HEAD_DIM ** -0.5

Q_TILE = 128
DIL_SEGMENTS = 4
SCAN_CHUNK = 64
SCAN_INTERLEAVE = 4
SAMPLE_PAD_T = 8
VMEM_LIMIT = 56 * 1024 * 1024
NN =(((1,), (0,)), ((), ()))
NT = (((1,), (1,)), ((), ()))
TN = (((0,), (0,)), ((), ()))


def _params(*sem):
    return pltpu.CompilerParams(dimension_semantics=sem, vmem_limit_bytes=VMEM_LIMIT)


def _dot(a, b):
    return jnp.dot(a, b, preferred_element_type=F32)


def _dot_nt(a, b):
    return lax.dot_general(a, b, NT, preferred_element_type=F32)


def _split(x, n):
    if x.dtype == BF16:
        return [x]
    parts, rem = [], x
    for i in range(n):
        parts.append(rem.astype(BF16))
        if i + 1 < n:
            rem = rem - parts[-1].astype(F32)
    return parts


def _mm(a, b, dims=NN, pa=1, pb=1):
    pas, pbs = _split(a, pa), _split(b, pb)
    order = max(len(pas), len(pbs))
    out = None
    for i, ai in enumerate(pas):
        for j, bj in enumerate(pbs):
            if i + j < order:
                term = lax.dot_general(ai, bj, dims, preferred_element_type=F32)
                out = term if out is None else out + term
    return out


def _rms(x, g):
    return x * lax.rsqrt(jnp.mean(x * x, axis=-1, keepdims=True) + RMS_EPS) * g


def _norm_proj_kernel(x_ref, g_ref, w_ref, *out_refs, pieces, transposed):
    xn = _rms(x_ref[...], g_ref[...]).astype(BF16)
    t_refs = iter(out_refs[len(pieces):])
    for idx, (out_ref, cols) in enumerate(zip(out_refs, pieces)):
        t_ref = next(t_refs) if idx in transposed else None
        off = 0
        for c0, c1 in cols:
            res = _dot(xn, w_ref[:, c0:c1])
            out_ref[:, off:off + c1 - c0] = res
            if t_ref is not None:
                t_ref[off:off + c1 - c0, :] = res.T
            off += c1 - c0


def _norm_proj(x, g, w, pieces, tm, transposed=(), rows_per_batch=None):
    m, d = x.shape
    n = w.shape[1]
    widths = [sum(c1 - c0 for c0, c1 in cols) for cols in pieces]
    out_specs = [pl.BlockSpec((tm, wd), lambda i: (i, 0)) for wd in widths]
    out_shape = [jax.ShapeDtypeStruct((m, wd), F32) for wd in widths]
    if transposed:
        tiles = rows_per_batch // tm
        out_specs += [pl.BlockSpec((None, widths[idx], tm), lambda i: (i // tiles, 0, i % tiles)) for idx in transposed]
        out_shape += [jax.ShapeDtypeStruct((m // rows_per_batch, widths[idx], rows_per_batch), F32)
                      for idx in transposed]
    return pl.pallas_call(
        functools.partial(_norm_proj_kernel, pieces=pieces, transposed=tuple(transposed)),
        grid=(m // tm,),
        in_specs=[pl.BlockSpec((tm, d), lambda i: (i, 0)),
                  pl.BlockSpec((1, d), lambda i: (0, 0)),
                  pl.BlockSpec((d, n), lambda i: (0, 0))],
        out_specs=out_specs,
        out_shape=out_shape,
        compiler_params=_params("parallel"),
        name="norm_proj",
    )(x, g.reshape(1, d), w.astype(BF16))


def _mem_kv_kernel(x_ref, g_ref, wt_ref, o_ref):
    xn = _rms(x_ref[...], g_ref[...]).astype(BF16)
    o_ref[...] = _dot_nt(wt_ref[...], xn)


def _mem_kv(mem, g, w):
    b, n, d = mem.shape
    nl, _, wd = w.shape
    return pl.pallas_call(
        _mem_kv_kernel,
        grid=(nl, b),
        in_specs=[pl.BlockSpec((None, n, d), lambda l, bi: (bi, 0, 0)),
                  pl.BlockSpec((None, 1, d), lambda l, bi: (l, 0, 0)),
                  pl.BlockSpec((None, wd, d), lambda l, bi: (l, 0, 0))],
        out_specs=pl.BlockSpec((None, None, wd, n), lambda l, bi: (l, bi, 0, 0)),
        out_shape=jax.ShapeDtypeStruct((nl, b, wd, n), F32),
        compiler_params=_params("parallel", "parallel"),
        name="mem_kv",
    )(mem, g.reshape(nl, 1, d), jnp.swapaxes(w, 1, 2).astype(BF16))


def _dil_attn_kernel(q_ref, kvc_ref, kvp_ref, bias_ref, acc_ref, st_ref, *, nr, tiles, prev_block):
    i = pl.program_id(2)
    q = (q_ref[...] * SCALE).astype(BF16)
    kvc = kvc_ref[...].astype(BF16)
    kvp = kvp_ref[...].astype(BF16) if prev_block else None
    chains = [(rr, tt, h) for rr in range(nr) for tt in range(tiles) for h in range(GROUP_HEADS)]
    n = len(chains)
    rows = lambda tt: slice(tt * Q_TILE, (tt + 1) * Q_TILE)

    def keys(rr, tt, h, off):
        lanes = slice(rr * 2 * GROUP_W + off + h * HEAD_DIM, rr * 2 * GROUP_W + off + (h + 1) * HEAD_DIM)
        return kvp[:, lanes] if tt < 0 else kvc[rows(tt), lanes]

    has_prev = [tt > 0 or prev_block for _, tt, _ in chains]
    qh = [q[rows(tt), rr * GROUP_W + h * HEAD_DIM:rr * GROUP_W + (h + 1) * HEAD_DIM] for rr, tt, h in chains]
    lc = [_dot_nt(qh[c], keys(rr, tt, h, 0)) + bias_ref[h, :, :Q_TILE] for c, (rr, tt, h) in enumerate(chains)]
    lp = [None] * n
    for c, (rr, tt, h) in enumerate(chains):
        if has_prev[c]:
            z = _dot_nt(qh[c], keys(rr, tt - 1, h, 0)) + bias_ref[h, :, Q_TILE:]
            lp[c] = jnp.where(i > 0, z, NEG_INF) if tt == 0 else z
    m = [jnp.max(z, axis=-1, keepdims=True) for z in lc]
    m = [jnp.maximum(m[c], jnp.max(lp[c], axis=-1, keepdims=True)) if has_prev[c] else m[c] for c in range(n)]
    pc = [jnp.exp(lc[c] - m[c]) for c in range(n)]
    pp = [jnp.exp(lp[c] - m[c]) if has_prev[c] else None for c in range(n)]
    den = [jnp.sum(z, axis=-1, keepdims=True) for z in pc]
    den = [den[c] + jnp.sum(pp[c], axis=-1, keepdims=True) if has_prev[c] else den[c] for c in range(n)]
    o = [_dot(pc[c].astype(BF16), keys(rr, tt, h, GROUP_W)) for c, (rr, tt, h) in enumerate(chains)]
    o = [o[c] + _dot(pp[c].astype(BF16), keys(rr, tt - 1, h, GROUP_W)) if has_prev[c] else o[c]
         for c, (rr, tt, h) in enumerate(chains)]
    lane = lax.broadcasted_iota(jnp.int32, (Q_TILE, 128), 1)
    st = None
    for c, (rr, tt, h) in enumerate(chains):
        acc_ref[rows(tt), rr * GROUP_W + h * HEAD_DIM:rr * GROUP_W + (h + 1) * HEAD_DIM] = o[c]
        st = jnp.zeros((Q_TILE, 128), F32) if h == 0 else st
        st = jnp.where(lane == h, m[c], st)
        st = jnp.where(lane == GROUP_HEADS + h, den[c], st)
        if h == GROUP_HEADS - 1:
            st_ref[rows(tt), rr * 128:(rr + 1) * 128] = st


def _dil_attn(q, kv, bias_tiles, g):
    dil = DILATIONS[g]
    b, t, _ = q.shape
    ln = t // dil
    tiles = min(ln // Q_TILE, DIL_SEGMENTS)
    nr = min(dil, DIL_SEGMENTS // tiles)
    nblk = ln // (tiles * Q_TILE)
    qv = q.reshape(b, ln, dil * GROUP_W)
    kvv = kv.reshape(b, ln, dil * 2 * GROUP_W)
    acc, st = pl.pallas_call(
        functools.partial(_dil_attn_kernel, nr=nr, tiles=tiles, prev_block=nblk > 1),
        grid=(b, dil // nr, nblk),
        in_specs=[pl.BlockSpec((None, tiles * Q_TILE, nr * GROUP_W), lambda bi, r, i: (bi, i, r)),
                  pl.BlockSpec((None, tiles * Q_TILE, nr * 2 * GROUP_W), lambda bi, r, i: (bi, i, r)),
                  pl.BlockSpec((None, Q_TILE, nr * 2 * GROUP_W),
                               lambda bi, r, i: (bi, jnp.maximum(tiles * i - 1, 0), r)),
                  pl.BlockSpec((GROUP_HEADS, Q_TILE, 2 * Q_TILE), lambda bi, r, i: (g, 0, 0))],
        out_specs=[pl.BlockSpec((None, tiles * Q_TILE, nr * GROUP_W), lambda bi, r, i: (bi, i, r)),
                   pl.BlockSpec((None, tiles * Q_TILE, nr * 128), lambda bi, r, i: (bi, i, r))],
        out_shape=[jax.ShapeDtypeStruct((b, ln, dil * GROUP_W), F32),
                   jax.ShapeDtypeStruct((b, ln, dil * 128), F32)],
        compiler_params=_params("parallel", "parallel", "arbitrary"),
        name=f"dil_attn_d{dil}",
    )(qv, kvv, kvv, bias_tiles)
    return acc.reshape(b, t, GROUP_W), st.reshape(b, t, 128)


def _sample_attn_kernel(q_ref, kn0_ref, kn1_ref, kn2_ref, c0_ref, c1_ref, c2_ref, b0_ref, b1_ref, b2_ref,
                        bnew_ref, o_ref, *, n_new):
    q = (q_ref[...] * SCALE).astype(BF16)
    qf = q.astype(F32)
    groups = ((kn0_ref, c0_ref, b0_ref), (kn1_ref, c1_ref, b1_ref), (kn2_ref, c2_ref, b2_ref))
    new_rows = []
    for g, (kn_ref, _, _) in enumerate(groups):
        kn = kn_ref[...]
        shifted = [kn] + [pltpu.roll(kn, s, axis=0) for s in range(1, n_new if g == 0 else 1)]
        new_rows.append([z.astype(BF16).astype(F32) for z in shifted])
    chains = [(h, g) for h in range(GROUP_HEADS) for g in range(N_GROUPS)]
    ids = range(len(chains))
    ks = lambda h: slice(h * HEAD_DIM, (h + 1) * HEAD_DIM)
    vs = lambda h: slice(GROUP_W + h * HEAD_DIM, GROUP_W + (h + 1) * HEAD_DIM)
    qs = lambda h, g: slice(g * GROUP_W + h * HEAD_DIM, g * GROUP_W + (h + 1) * HEAD_DIM)
    lp = [_dot(q[:, qs(h, g)], groups[g][1][0, h].astype(BF16)) + groups[g][2][h] for h, g in chains]
    lns = [[jnp.sum(qf[:, qs(h, g)] * rows[:, ks(h)], axis=-1, keepdims=True) + bnew_ref[g, s][:, h:h + 1]
            for s, rows in enumerate(new_rows[g])] for h, g in chains]
    m = [functools.reduce(jnp.maximum, [jnp.max(lp[c], axis=-1, keepdims=True)] + lns[c]) for c in ids]
    pp = [jnp.exp(lp[c] - m[c]) for c in ids]
    pn = [[jnp.exp(ln - m[c]) for ln in lns[c]] for c in ids]
    den = [jnp.sum(pp[c], axis=-1, keepdims=True) + sum(pn[c]) for c in ids]
    o = [_dot_nt(pp[c].astype(BF16), groups[g][1][1, h].astype(BF16)) for c, (h, g) in enumerate(chains)]
    o = [o[c] + sum(z.astype(BF16).astype(F32) * rows[:, vs(h)] for z, rows in zip(pn[c], new_rows[g]))
         for c, (h, g) in enumerate(chains)]
    for h in range(GROUP_HEADS):
        cs = [h * N_GROUPS + g for g in range(N_GROUPS)]
        mx = functools.reduce(jnp.maximum, [m[c] for c in cs])
        wts = [jnp.exp(m[c] - mx) * den[c] for c in cs]
        num = sum(w * (o[c] / den[c]) for w, c in zip(wts, cs))
        o_ref[:, ks(h)] = num / sum(wts)


def _sample_attn(q, kvn, caches, bias_past, bias_new, n_new):
    b, tp, _ = q.shape
    new_spec = pl.BlockSpec((None, tp, 2 * GROUP_W), lambda bi: (bi, 0, 0))
    cache_spec = lambda w: pl.BlockSpec((None, 2, GROUP_HEADS, HEAD_DIM, w), lambda bi: (bi, 0, 0, 0, 0))
    bias_spec = lambda w: pl.BlockSpec((GROUP_HEADS, tp, w), lambda bi: (0, 0, 0))
    return pl.pallas_call(
        functools.partial(_sample_attn_kernel, n_new=n_new),
        grid=(b,),
        in_specs=[pl.BlockSpec((None, tp, A_Q), lambda bi: (bi, 0, 0)), new_spec, new_spec, new_spec]
                 + [cache_spec(w) for w in WINDOWS] + [bias_spec(w) for w in WINDOWS]
                 + [pl.BlockSpec((N_GROUPS, n_new, tp, 128), lambda bi: (0, 0, 0, 0))],
        out_specs=pl.BlockSpec((None, tp, GROUP_W), lambda bi: (bi, 0, 0)),
        out_shape=jax.ShapeDtypeStruct((b, tp, GROUP_W), F32),
        compiler_params=_params("parallel"),
        name="sample_attn",
    )(q, kvn[0], kvn[1], kvn[2], *caches, *bias_past, bias_new)


def _cross_attn_kernel(q_ref, kvt_ref, o_ref, *, sub):
    nb, tq, _ = q_ref.shape
    chains = [(bi, slice(r0, r0 + sub), h) for bi in range(nb) for r0 in range(0, tq, sub)
              for h in range(MEM_W // HEAD_DIM)]
    q = [(q_ref[bi] * SCALE).astype(BF16) for bi in range(nb)]
    kvt = [kvt_ref[bi].astype(BF16) for bi in range(nb)]
    ks = lambda h: slice(h * HEAD_DIM, (h + 1) * HEAD_DIM)
    vs = lambda h: slice(MEM_W + h * HEAD_DIM, MEM_W + (h + 1) * HEAD_DIM)
    lg = [_dot(q[bi][rw, ks(h)], kvt[bi][ks(h), :]) for bi, rw, h in chains]
    p = [jnp.exp(z - jnp.max(z, axis=-1, keepdims=True)) for z in lg]
    den = [jnp.sum(z, axis=-1, keepdims=True) for z in p]
    o = [_dot_nt(p[c].astype(BF16), kvt[bi][vs(h), :]) for c, (bi, rw, h) in enumerate(chains)]
    for c, (bi, rw, h) in enumerate(chains):
        o_ref[bi, rw, ks(h)] = o[c] / den[c]


def _cross_attn(q, kvt, layer, nb, tq):
    b, t, _ = q.shape
    return pl.pallas_call(
        functools.partial(_cross_attn_kernel, sub=min(tq, Q_TILE)),
        grid=(b // nb, t // tq),
        in_specs=[pl.BlockSpec((nb, tq, MEM_W), lambda bi, i: (bi, i, 0)),
                  pl.BlockSpec((None, nb, 2 * MEM_W, N_MEM), lambda bi, i: (layer, bi, 0, 0))],
        out_specs=pl.BlockSpec((nb, tq, MEM_W), lambda bi, i: (bi, i, 0)),
        out_shape=jax.ShapeDtypeStruct((b, t, MEM_W), F32),
        compiler_params=_params("parallel", "parallel"),
        name="cross_attn",
    )(q, kvt)


def _rwkv_kernel(cols_ref, shift_ref, s0_ref, mu_ref, w0_ref, wup_ref, a0_ref, aup_ref, kk_ref, ka_ref, rk_ref,
                 lnw_ref, lnb_ref, y_ref, sout_ref, shout_ref,
                 carry, state, r_s, k_s, v_s, a_s, b_s, lw_s, *, tt, chunk, t_valid):
    j = pl.program_id(1)

    @pl.when(j == 0)
    def _():
        carry[...] = shift_ref[...]
        state[...] = s0_ref[...]

    cols = cols_ref[...]
    row = lax.broadcasted_iota(jnp.int32, (tt, 1), 0)
    prev = jnp.where(row == 0, carry[...], pltpu.roll(cols, 1, axis=0))
    carry[...] = cols[t_valid - 1:t_valid, :]
    xs = cols + mu_ref[...] * (prev - cols)
    r = xs[:, :RWKV_W]
    k = xs[:, RWKV_W:2 * RWKV_W]
    v = xs[:, 2 * RWKV_W:3 * RWKV_W]
    wd = xs[:, 3 * RWKV_W:3 * RWKV_W + LORA]
    ad = xs[:, 3 * RWKV_W + LORA:]
    z = w0_ref[...] + _dot(jnp.tanh(wd).astype(BF16), wup_ref[...])
    log_decay = -math.exp(-0.5) * jax.nn.sigmoid(z)
    a = jax.nn.sigmoid(a0_ref[...] + _dot(ad.astype(BF16), aup_ref[...]))

    blockdiag = (lax.broadcasted_iota(jnp.int32, (128, 128), 0) // HEAD_DIM
                 == lax.broadcasted_iota(jnp.int32, (128, 128), 1) // HEAD_DIM)

    def head_sum(x):
        lo = lax.broadcasted_iota(jnp.int32, x.shape, 1) < HEAD_DIM
        s0 = jnp.sum(jnp.where(lo, x, 0.0), axis=-1, keepdims=True)
        s1 = jnp.sum(jnp.where(lo, 0.0, x), axis=-1, keepdims=True)
        return jnp.where(lo, s0, s1)

    kk = k * kk_ref[...]
    kk_sq = kk * kk
    k2 = k * (1.0 + (a - 1.0) * ka_ref[...])
    live = row < t_valid
    for p in range(RWKV_PAIRS):
        ps = slice(p * 128, (p + 1) * 128)
        nrm = jnp.maximum(jnp.sqrt(head_sum(kk_sq[:, ps])), 1e-12)
        kkn = kk[:, ps] / nrm
        if t_valid < tt:
            zero = jnp.zeros((tt, 128), F32)
            r_s[:, ps] = r[:, ps]
            k_s[:, ps] = jnp.where(live, k2[:, ps], zero)
            v_s[:, ps] = jnp.where(live, v[:, ps], zero)
            a_s[:, ps] = jnp.where(live, -kkn, zero)
            b_s[:, ps] = jnp.where(live, kkn * a[:, ps], zero)
            lw_s[:, ps] = jnp.where(live, log_decay[:, ps], zero)
        else:
            r_s[:, ps] = r[:, ps]
            k_s[:, ps] = k2[:, ps]
            v_s[:, ps] = v[:, ps]
            a_s[:, ps] = -kkn
            b_s[:, ps] = kkn * a[:, ps]
            lw_s[:, ps] = log_decay[:, ps]

    ci = lax.broadcasted_iota(jnp.int32, (chunk, chunk), 0)
    cj = lax.broadcasted_iota(jnp.int32, (chunk, chunk), 1)
    tri_incl = (ci >= cj).astype(BF16)
    lo_lane = lax.broadcasted_iota(jnp.int32, (chunk, 128), 1) < HEAD_DIM
    levels = max(1, math.ceil(math.log2(min(chunk, t_valid))))
    n2 = 2 * chunk
    ri = lax.broadcasted_iota(jnp.int32, (n2, n2), 0)
    rj = lax.broadcasted_iota(jnp.int32, (n2, n2), 1)
    same = ri // chunk == rj // chunk
    strict = jnp.logical_and(same, ri > rj)
    incl = jnp.logical_and(same, ri >= rj)
    own = lax.broadcasted_iota(jnp.int32, (n2, 128), 0) // chunk == lax.broadcasted_iota(jnp.int32, (n2, 128), 1) // HEAD_DIM
    dup = lambda z: jnp.concatenate([z, z], axis=0)
    cat = jnp.concatenate

    n_chunks = tt // chunk
    group = SCAN_INTERLEAVE if n_chunks % SCAN_INTERLEAVE == 0 else 1

    def chunk_body(ci, _):
        rows = [pl.ds(pl.multiple_of((ci * group + cc) * chunk, chunk), chunk) for cc in range(group)]
        sl = [slice(p * 128, (p + 1) * 128) for p in range(RWKV_PAIRS)]
        chains = [(rows[cc], sl[p]) for cc in range(group) for p in range(RWKV_PAIRS)]
        ids = range(len(chains))
        cum_all = [_mm(tri_incl, lw_s[rw, :], pb=3) for rw in rows]
        cum = [cum_all[cc][:, s] for cc in range(group) for s in sl]
        p_incl = [jnp.exp(z) for z in cum]
        p_inv = [jnp.exp(-z) for z in cum]
        rr = [r_s[rw, s] for rw, s in chains]
        kc = [k_s[rw, s] for rw, s in chains]
        vc = [v_s[rw, s] for rw, s in chains]
        at2 = [jnp.where(own, dup(a_s[rw, s] * jnp.exp(cum[c] - lw_s[rw, s])), 0.0) for c, (rw, s) in enumerate(chains)]
        rt2 = [jnp.where(own, dup(rr[c] * p_incl[c]), 0.0) for c in ids]
        bt = [(b_s[rw, s] * p_inv[c]).astype(BF16) for c, (rw, s) in enumerate(chains)]
        kt = [(kc[c] * p_inv[c]).astype(BF16) for c in ids]
        vb = [z.astype(BF16) for z in vc]
        v2 = [dup(z) for z in vb]
        g = [_mm(cat([at2[c], rt2[c]], 0).astype(BF16), cat([dup(bt[c]), dup(kt[c])], 0), NT) for c in ids]
        a_ak = [jnp.where(strict, z[:n2, n2:], 0.0).astype(BF16) for z in g]
        apow = [jnp.where(strict, z[:n2, :n2], 0.0).astype(BF16) for z in g]
        a_r = [cat([jnp.where(incl, z[n2:, :n2], 0.0), jnp.where(incl, z[n2:, n2:], 0.0)], 1).astype(BF16)
               for z in g]
        akv = [_mm(a_ak[c], v2[c]) for c in ids]
        sol = [cat([at2[c], akv[c]], axis=1) for c in ids]
        for lvl in range(levels):
            sol = [sol[c] + _mm(apow[c], sol[c].astype(BF16)) for c in ids]
            if lvl + 1 < levels:
                apow = [_mm(z, z).astype(BF16) for z in apow]
        ws = [z[:, :128] for z in sol]
        u0s = [jnp.where(own, z[:, 128:], 0.0) for z in sol]
        zeros2 = jnp.zeros((n2, 128), BF16)
        qy = [_mm(a_r[c], cat([cat([ws[c], u0s[c]], 1).astype(BF16), cat([zeros2, v2[c]], 1)], 0)) for c in ids]
        qs = [(rt2[c] + qy[c][:, :128]).astype(BF16) for c in ids]
        zeros1 = jnp.zeros((chunk, 128), BF16)
        lhs = [cat([cat([ws[c][:chunk] + ws[c][chunk:], u0s[c][:chunk] + u0s[c][chunk:]], 1).astype(BF16),
                    cat([zeros1, vb[c]], 1)], 0) for c in ids]
        wbn = [_mm(lhs[c], cat([bt[c], kt[c]], 0), TN) for c in ids]
        wb = [z[:128].astype(BF16) for z in wbn]
        s_cur = [state[p] for p in range(RWKV_PAIRS)]
        ys = []
        for cc in range(group):
            base = cc * RWKV_PAIRS
            s_b = [z.astype(BF16) for z in s_cur]
            ys += [_mm(qs[base + p], s_b[p], NT) + qy[base + p][:, 128:] for p in range(RWKV_PAIRS)]
            sw = [_mm(s_b[p], wb[base + p]) for p in range(RWKV_PAIRS)]
            s_cur = [jnp.where(blockdiag, (s_cur[p] + sw[p] + wbn[base + p][128:])
                               * p_incl[base + p][chunk - 1:chunk, :], 0.0) for p in range(RWKV_PAIRS)]
        for p in range(RWKV_PAIRS):
            state[p] = s_cur[p]
        y = [jnp.where(lo_lane, z[:chunk], z[chunk:]) for z in ys]

        bonus = [head_sum(rr[c] * kc[c] * rk_ref[:, s]) for c, (_, s) in enumerate(chains)]
        dlt = [y[c] - head_sum(y[c]) * (1.0 / HEAD_DIM) for c in ids]
        var = [head_sum(z * z) * (1.0 / HEAD_DIM) for z in dlt]
        for c, (rw, s) in enumerate(chains):
            yn = dlt[c] * lax.rsqrt(var[c] + GN_EPS) * lnw_ref[:, s] + lnb_ref[:, s]
            y_ref[rw, s] = yn + bonus[c] * vc[c]
        return 0

    lax.fori_loop(0, n_chunks // group, chunk_body, 0)

    @pl.when(j == pl.num_programs(1) - 1)
    def _():
        sout_ref[...] = state[...]
        shout_ref[...] = carry[...]


def _rwkv_scan(cols, shift_prev, s0, prm, tt, t_valid):
    b, t, _ = cols.shape
    s = s0.reshape(b, RWKV_PAIRS, 2, HEAD_DIM, HEAD_DIM)
    z = jnp.zeros_like(s[:, :, 0])
    s_bd = jnp.concatenate([jnp.concatenate([s[:, :, 0], z], -1), jnp.concatenate([z, s[:, :, 1]], -1)], -2)
    row = lambda n: pl.BlockSpec((1, n), lambda bi, j: (0, 0))
    vec = lambda a: a.reshape(1, -1)
    y, s_out, sh_out = pl.pallas_call(
        functools.partial(_rwkv_kernel, tt=tt, chunk=SCAN_CHUNK, t_valid=min(t_valid, tt)),
        grid=(b, t // tt),
        in_specs=[pl.BlockSpec((None, tt, C_SHIFT), lambda bi, j: (bi, j, 0)),
                  pl.BlockSpec((None, 1, C_SHIFT), lambda bi, j: (bi, 0, 0)),
                  pl.BlockSpec((None, RWKV_PAIRS, 128, 128), lambda bi, j: (bi, 0, 0, 0)),
                  row(C_SHIFT), row(RWKV_W),
                  pl.BlockSpec((LORA, RWKV_W), lambda bi, j: (0, 0)),
                  row(RWKV_W),
                  pl.BlockSpec((LORA, RWKV_W), lambda bi, j: (0, 0)),
                  row(RWKV_W), row(RWKV_W), row(RWKV_W), row(RWKV_W), row(RWKV_W)],
        out_specs=[pl.BlockSpec((None, tt, RWKV_W), lambda bi, j: (bi, j, 0)),
                   pl.BlockSpec((None, RWKV_PAIRS, 128, 128), lambda bi, j: (bi, 0, 0, 0)),
                   pl.BlockSpec((None, 1, C_SHIFT), lambda bi, j: (bi, 0, 0))],
        out_shape=[jax.ShapeDtypeStruct((b, t, RWKV_W), F32),
                   jax.ShapeDtypeStruct((b, RWKV_PAIRS, 128, 128), F32),
                   jax.ShapeDtypeStruct((b, 1, C_SHIFT), F32)],
        scratch_shapes=[pltpu.VMEM((1, C_SHIFT), F32), pltpu.VMEM((RWKV_PAIRS, 128, 128), F32)]
                       + [pltpu.VMEM((tt, RWKV_W), F32)] * 6,
        compiler_params=_params("parallel", "arbitrary"),
        name="rwkv_scan",
    )(cols, shift_prev.reshape(b, 1, C_SHIFT), s_bd, vec(prm["mu"]), vec(prm["w0"]), prm["w_up"].astype(BF16),
      vec(prm["a0"]), prm["a_up"].astype(BF16), vec(prm["k_k"]), vec(prm["k_a"]), vec(prm["r_k"]),
      vec(prm["ln_w"]), vec(prm["ln_b"]))
    s_fin = jnp.stack([s_out[:, :, :HEAD_DIM, :HEAD_DIM], s_out[:, :, HEAD_DIM:, HEAD_DIM:]], axis=2)
    return y, s_fin.reshape(b, 2 * RWKV_PAIRS, HEAD_DIM, HEAD_DIM), sh_out.reshape(b, C_SHIFT)


def _post_kernel(*refs, n_merge, mix_w):
    x_ref, omem_ref, gate_ref, w_ref, g_ref = refs[:5]
    mix_refs = refs[5:-1]
    y_ref = refs[-1]
    if n_merge:
        er = lax.broadcasted_iota(jnp.int32, (128, 2 * GROUP_W), 0)
        ec = lax.broadcasted_iota(jnp.int32, (128, 2 * GROUP_W), 1)
        expand = (er == ec // HEAD_DIM).astype(BF16)
        ms, ss, os_ = [], [], []
        for g in range(n_merge):
            st = _mm(mix_refs[2 * g + 1][...], expand, pa=3)
            ms.append(st[:, :GROUP_W])
            ss.append(st[:, GROUP_W:])
            os_.append(mix_refs[2 * g][...] / st[:, GROUP_W:])
        mx = functools.reduce(jnp.maximum, ms)
        wts = [jnp.exp(m - mx) * s for m, s in zip(ms, ss)]
        mix = sum(w * o for w, o in zip(wts, os_)) / sum(wts)
    else:
        mix = mix_refs[0][...]
    gate = gate_ref[...]
    act = gate * jax.nn.sigmoid(gate)
    h1 = (mix * act[:, :mix_w]).astype(BF16)
    h2 = (omem_ref[...] * act[:, mix_w:]).astype(BF16)
    out = _dot(h1, w_ref[:mix_w, :]) + _dot(h2, w_ref[mix_w:, :])
    y_ref[...] = x_ref[...] + _rms(out, g_ref[...])


def _post(x, mix, o_mem, gate, w_out, g_post, tm):
    m, d = x.shape
    merge = isinstance(mix, (list, tuple))
    mix_arrays = [a for pair in mix for a in pair] if merge else [mix]
    mix_w = GROUP_W if merge else mix.shape[1]
    gw = gate.shape[1]
    rows = lambda wd: pl.BlockSpec((tm, wd), lambda i: (i, 0))
    return pl.pallas_call(
        functools.partial(_post_kernel, n_merge=len(mix) if merge else 0, mix_w=mix_w),
        grid=(m // tm,),
        in_specs=[rows(d), rows(MEM_W), rows(gw),
                  pl.BlockSpec((gw, d), lambda i: (0, 0)),
                  pl.BlockSpec((1, d), lambda i: (0, 0))] + [rows(a.shape[1]) for a in mix_arrays],
        out_specs=rows(d),
        out_shape=jax.ShapeDtypeStruct((m, d), F32),
        compiler_params=_params("parallel"),
        name="post",
    )(x, o_mem, gate, w_out.astype(BF16), g_post.reshape(1, d), *mix_arrays)


def _t5_bucket(dist):
    max_exact = N_BUCKETS // 2
    d = jnp.maximum(dist, 1).astype(F32)
    large = max_exact + (jnp.log(d / max_exact) / math.log(MAX_DISTANCE / max_exact)
                         * (N_BUCKETS - max_exact)).astype(jnp.int32)
    large = jnp.minimum(large, N_BUCKETS - 1)
    return jnp.where(dist < max_exact, dist, large)


def _group_bias(rel_bias, g):
    dist = DILATIONS[g] * jnp.arange(A_KEYS, dtype=jnp.int32)
    bias = rel_bias[_t5_bucket(dist)]
    return bias[:, g * GROUP_HEADS:(g + 1) * GROUP_HEADS].T.astype(F32)


def _bias_tiles_kernel(x_ref, o_ref):
    row = lax.broadcasted_iota(jnp.int32, (Q_TILE, 2 * Q_TILE), 0)
    for h in range(o_ref.shape[0]):
        t = jnp.broadcast_to(x_ref[h:h + 1, :], (Q_TILE, 2 * Q_TILE))
        for bit in range(int(math.log2(Q_TILE))):
            t = jnp.where((row >> bit) & 1 == 1, pltpu.roll(t, 1 << bit, axis=1), t)
        o_ref[h] = t


def _prompt_bias(biases):
    bias = jnp.concatenate(biases, axis=0)
    neg = jnp.full((bias.shape[0], Q_TILE - 1), NEG_INF, F32)
    table = jnp.concatenate([bias[:, :1], neg, bias[:, :0:-1]], axis=1)
    n = table.shape[0]
    return pl.pallas_call(
        _bias_tiles_kernel,
        out_shape=jax.ShapeDtypeStruct((n, Q_TILE, 2 * Q_TILE), F32),
        name="bias_tiles",
    )(table)


def _sample_bias(biases, n_new, tp):
    t = np.arange(tp)[:, None]
    real = t < n_new
    past = []
    for g, bias in enumerate(biases):
        w, dil = WINDOWS[g], DILATIONS[g]
        spread = jnp.concatenate([bias[:, :0:-1, None], jnp.full((GROUP_HEADS, w // dil, dil - 1), NEG_INF, F32)],
                                 axis=-1).reshape(GROUP_HEADS, w)
        rows = [jnp.concatenate([jnp.full((GROUP_HEADS, tt), NEG_INF, F32), spread[:, :w - tt]], axis=-1)
                for tt in range(n_new)] + [jnp.zeros((GROUP_HEADS, w), F32)] * (tp - n_new)
        past.append(jnp.stack(rows, axis=1))
    s = np.arange(n_new)[None, :, None]
    g_idx = np.arange(N_GROUPS)[:, None, None]
    ok = (s <= t[None, :, 0]) & real[None, :, 0] & ((g_idx == 0) | (s == 0))
    vals = jnp.stack([bias[:, :n_new].T for bias in biases])
    new = jnp.where(jnp.asarray(ok)[..., None], vals[:, :, None, :], NEG_INF)
    return past, jnp.pad(new, ((0, 0), (0, 0), (0, 0), (0, 128 - GROUP_HEADS)))


def _pieces_a():
    q = [((g * GROUP_W, (g + 1) * GROUP_W),) for g in range(N_GROUPS)]
    kv = [((A_Q + g * GROUP_W, A_Q + (g + 1) * GROUP_W), (2 * A_Q + g * GROUP_W, 2 * A_Q + (g + 1) * GROUP_W))
          for g in range(N_GROUPS)]
    qmem = ((3 * A_Q, 3 * A_Q + MEM_W),)
    gate = ((3 * A_Q + MEM_W, 3 * A_Q + MEM_W + GROUP_W + MEM_W),)
    return tuple(q + kv + [qmem, gate])


def _pieces_b():
    return (((0, C_SHIFT),), ((C_SHIFT, C_SHIFT + MEM_W),), ((C_SHIFT + MEM_W, C_SHIFT + MEM_W + RWKV_W + MEM_W),))


def kernel(x_prompt, x_sample, mem_prompt, cache_mem_kv, cache_win0, cache_win1, cache_win2, state_wkv, state_shift, norm_pre, norm_post, norm_mem, w_mem_kv, rel_bias, w_in_a, w_out_a, w_in_b, w_out_b, rwkv_mu, rwkv_w0, rwkv_w_up, rwkv_a0, rwkv_a_up, rwkv_k_k, rwkv_k_a, rwkv_r_k, rwkv_ln_w, rwkv_ln_b):
    bp, tp, d = x_prompt.shape
    bs, ts, _ = x_sample.shape
    tsp = SAMPLE_PAD_T
    xp = x_prompt.reshape(bp * tp, d)
    xs = jnp.pad(x_sample, ((0, 0), (0, tsp - ts), (0, 0))).reshape(bs * tsp, d)
    tm_p, tm_s = 512, bs * tsp
    time_minor = lambda c: jnp.moveaxis(c, -4, -1)
    caches = [time_minor(c[0]) for c in (cache_win0, cache_win1, cache_win2)]
    mkv_s = time_minor(cache_mem_kv).reshape(cache_mem_kv.shape[0], bs, 2 * MEM_W, N_MEM)
    mkv_p = _mem_kv(mem_prompt, norm_mem, w_mem_kv)

    biases = [_group_bias(rel_bias, g) for g in range(N_GROUPS)]
    outs_p = _norm_proj(xp, norm_pre[0], w_in_a[0], _pieces_a(), tm_p, transposed=(3, 4, 5), rows_per_batch=tp)
    outs_s = _norm_proj(xs, norm_pre[0], w_in_a[0], _pieces_a(), tm_s)
    q_p, kv_p, qmem_p, gate_p, kvt_p = outs_p[0:3], outs_p[3:6], outs_p[6], outs_p[7], outs_p[8:11]
    q_s, kv_s, qmem_s, gate_s = outs_s[0:3], outs_s[3:6], outs_s[6], outs_s[7]

    bias_tiles = _prompt_bias(biases)
    merged = [_dil_attn(q_p[g].reshape(bp, tp, GROUP_W), kv_p[g].reshape(bp, tp, 2 * GROUP_W), bias_tiles, g)
              for g in range(N_GROUPS)]
    merged = [(a.reshape(bp * tp, GROUP_W), s.reshape(bp * tp, 128)) for a, s in merged]
    omem_p = _cross_attn(qmem_p.reshape(bp, tp, MEM_W), mkv_p, 0, 1, 512)
    xp1 = _post(xp, merged, omem_p.reshape(bp * tp, MEM_W), gate_p, w_out_a[0], norm_post[0], tm_p)

    bias_past, bias_new = _sample_bias(biases, ts, tsp)
    q_s_all = jnp.concatenate(q_s, axis=-1).reshape(bs, tsp, A_Q)
    kvn = [a.reshape(bs, tsp, 2 * GROUP_W) for a in kv_s]
    o_s = _sample_attn(q_s_all, kvn, caches, bias_past, bias_new, ts)
    omem_s = _cross_attn(qmem_s.reshape(bs, tsp, MEM_W), mkv_s, 0, 4, tsp)
    xs1 = _post(xs, o_s.reshape(bs * tsp, GROUP_W), omem_s.reshape(bs * tsp, MEM_W), gate_s, w_out_a[0],
                norm_post[0], tm_s)

    cols_p, qmem_p, gate_p = _norm_proj(xp1, norm_pre[1], w_in_b[0], _pieces_b(), tm_p)
    cols_s, qmem_s, gate_s = _norm_proj(xs1, norm_pre[1], w_in_b[0], _pieces_b(), tm_s)
    prm = dict(mu=rwkv_mu[0], w0=rwkv_w0[0], w_up=rwkv_w_up[0], a0=rwkv_a0[0], a_up=rwkv_a_up[0], k_k=rwkv_k_k[0],
               k_a=rwkv_k_a[0], r_k=rwkv_r_k[0], ln_w=rwkv_ln_w[0], ln_b=rwkv_ln_b[0])
    y_p, wkv_p, sh_p = _rwkv_scan(cols_p.reshape(bp, tp, C_SHIFT), jnp.zeros((bp, C_SHIFT), F32),
                                  jnp.zeros((bp, 2 * RWKV_PAIRS, HEAD_DIM, HEAD_DIM), F32), prm, 256, tp)
    cols_s_pad = jnp.pad(cols_s.reshape(bs, tsp, C_SHIFT), ((0, 0), (0, SCAN_CHUNK - tsp), (0, 0)))
    y_s, wkv_s, sh_s = _rwkv_scan(cols_s_pad, state_shift[0], state_wkv[0], prm, SCAN_CHUNK, ts)
    y_s = y_s[:, :tsp].reshape(bs * tsp, RWKV_W)
    omem_p = _cross_attn(qmem_p.reshape(bp, tp, MEM_W), mkv_p, 1, 1, 512)
    omem_s = _cross_attn(qmem_s.reshape(bs, tsp, MEM_W), mkv_s, 1, 4, tsp)
    xp2 = _post(xp1, y_p.reshape(bp * tp, RWKV_W), omem_p.reshape(bp * tp, MEM_W), gate_p, w_out_b[0],
                norm_post[1], tm_p)
    xs2 = _post(xs1, y_s, omem_s.reshape(bs * tsp, MEM_W), gate_s, w_out_b[0], norm_post[1], tm_s)

    kv_shape = (2, GROUP_HEADS, HEAD_DIM)
    time_major = lambda c: jnp.moveaxis(c, -1, -4)
    new_mem_kv = time_major(mkv_p.reshape(mkv_p.shape[0], bp, 2, MEM_W // HEAD_DIM, HEAD_DIM, N_MEM))
    win_p = [time_major(kvt_p[g].reshape(bp, *kv_shape, tp)[..., tp - min(WINDOWS[g], tp):])[None]
             for g in range(N_GROUPS)]
    win_s = [kv_s[g].reshape(bs, tsp, *kv_shape)[None, :, :ts] for g in range(N_GROUPS)]
    return (xp2.reshape(bp, tp, d), xs2.reshape(bs, tsp, d)[:, :ts], new_mem_kv,
            win_p[0], win_p[1], win_p[2], win_s[0], win_s[1], win_s[2],
            wkv_p[None], wkv_s[None], sh_p[None], sh_s[None])
```

```python
import functools
import math

import numpy as np
import jax
import jax.numpy as jnp
from jax import lax
from jax.experimental import pallas as pl
from jax.experimental.pallas import tpu as pltpu

F32 = jnp.float32
BF16 = jnp.bfloat16

D_MODEL = 1024
HEAD_DIM = 64
N_GROUPS = 3
GROUP_HEADS = 4
WINDOWS = (128, 512, 2048)
DILATIONS = (1, 4, 16)
A_HEADS = N_GROUPS * GROUP_HEADS
A_KEYS = 129
GROUP_W = GROUP_HEADS * HEAD_DIM
A_Q = A_HEADS * HEAD_DIM
N_MEM = 256
MEM_W = 256
RWKV_W = 768
RWKV_PAIRS = RWKV_W // 128
LORA = 64
C_SHIFT = 3 * RWKV_W + 2 * LORA
N_BUCKETS = 32
MAX_DISTANCE = WINDOWS[-1]
RMS_EPS = 1e-6
GN_EPS = 64e-5
NEG_INF = -1e30
SCALE = HEAD_DIM ** -0.5

Q_TILE = 128
DIL_SEGMENTS = 4
SCAN_CHUNK = 64
SCAN_INTERLEAVE = 4
SAMPLE_PAD_T = 8
VMEM_LIMIT = 56 * 1024 * 1024
NN =(((1,), (0,)), ((), ()))
NT = (((1,), (1,)), ((), ()))
TN = (((0,), (0,)), ((), ()))


def _params(*sem):
    return pltpu.CompilerParams(dimension_semantics=sem, vmem_limit_bytes=VMEM_LIMIT)


def _dot(a, b):
    return jnp.dot(a, b, preferred_element_type=F32)


def _dot_nt(a, b):
    return lax.dot_general(a, b, NT, preferred_element_type=F32)


def _split(x, n):
    if x.dtype == BF16:
        return [x]
    parts, rem = [], x
    for i in range(n):
        parts.append(rem.astype(BF16))
        if i + 1 < n:
            rem = rem - parts[-1].astype(F32)
    return parts


def _mm(a, b, dims=NN, pa=1, pb=1):
    pas, pbs = _split(a, pa), _split(b, pb)
    order = max(len(pas), len(pbs))
    out = None
    for i, ai in enumerate(pas):
        for j, bj in enumerate(pbs):
            if i + j < order:
                term = lax.dot_general(ai, bj, dims, preferred_element_type=F32)
                out = term if out is None else out + term
    return out


def _rms(x, g):
    return x * lax.rsqrt(jnp.mean(x * x, axis=-1, keepdims=True) + RMS_EPS) * g


def _to_classes(dst_ref, stage, val, dil, width, off):
    tm = val.shape[0]
    for j in range(val.shape[1] // 128):
        stage[...] = val[:, 128 * j:128 * (j + 1)]
        for r in range(dil):
            lane0 = r * width + off + 128 * j
            dst_ref[:, lane0:lane0 + 128] = stage[pl.ds(r, tm // dil, stride=dil), :]


def _from_classes(src_ref, stage, dil, width):
    tm = stage.shape[0]
    chunks = []
    for j in range(width // 128):
        for r in range(dil):
            lane0 = r * width + 128 * j
            stage[pl.ds(r, tm // dil, stride=dil), :] = src_ref[:, lane0:lane0 + 128]
        chunks.append(stage[...])
    return jnp.concatenate(chunks, axis=1)


def _norm_proj_kernel(x_ref, g_ref, w_ref, *refs, pieces, transposed, dils):
    stage = refs[-1]
    out_refs = refs[:-1]
    xn = _rms(x_ref[...], g_ref[...]).astype(BF16)
    t_refs = iter(out_refs[len(pieces):])
    for idx, (out_ref, cols) in enumerate(zip(out_refs, pieces)):
        t_ref = next(t_refs) if idx in transposed else None
        width = sum(c1 - c0 for c0, c1 in cols)
        off = 0
        for c0, c1 in cols:
            res = _dot(xn, w_ref[:, c0:c1])
            if dils[idx] > 1:
                _to_classes(out_ref, stage, res, dils[idx], width, off)
            else:
                out_ref[:, off:off + c1 - c0] = res
            if t_ref is not None:
                t_ref[off:off + c1 - c0, :] = res.T
            off += c1 - c0


def _norm_proj(x, g, w, pieces, tm, transposed=(), dils=None, rows_per_batch=None):
    m, d = x.shape
    n = w.shape[1]
    dils = tuple(dils) if dils else (1,) * len(pieces)
    widths = [sum(c1 - c0 for c0, c1 in cols) for cols in pieces]
    tiles = rows_per_batch // tm if rows_per_batch else None
    out_specs, out_shape = [], []
    for wd, dil in zip(widths, dils):
        if dil > 1:
            out_specs.append(pl.BlockSpec((None, tm // dil, dil * wd), lambda i: (i // tiles, i % tiles, 0)))
            out_shape.append(jax.ShapeDtypeStruct((m // rows_per_batch, rows_per_batch // dil, dil * wd), F32))
        else:
            out_specs.append(pl.BlockSpec((tm, wd), lambda i: (i, 0)))
            out_shape.append(jax.ShapeDtypeStruct((m, wd), F32))
    out_specs += [pl.BlockSpec((None, widths[idx], tm), lambda i: (i // tiles, 0, i % tiles)) for idx in transposed]
    out_shape += [jax.ShapeDtypeStruct((m // rows_per_batch, widths[idx], rows_per_batch), F32) for idx in transposed]
    return pl.pallas_call(
        functools.partial(_norm_proj_kernel, pieces=pieces, transposed=tuple(transposed), dils=dils),
        grid=(m // tm,),
        in_specs=[pl.BlockSpec((tm, d), lambda i: (i, 0)),
                  pl.BlockSpec((1, d), lambda i: (0, 0)),
                  pl.BlockSpec((d, n), lambda i: (0, 0))],
        out_specs=out_specs,
        out_shape=out_shape,
        scratch_shapes=[pltpu.VMEM((tm, 128), F32)],
        compiler_params=_params("parallel"),
        name="norm_proj",
    )(x, g.reshape(1, d), w.astype(BF16))


def _mem_kv_kernel(x_ref, g_ref, wt_ref, o_ref):
    xn = _rms(x_ref[...], g_ref[...]).astype(BF16)
    o_ref[...] = _dot_nt(wt_ref[...], xn)


def _mem_kv(mem, g, w):
    b, n, d = mem.shape
    nl, _, wd = w.shape
    return pl.pallas_call(
        _mem_kv_kernel,
        grid=(nl, b),
        in_specs=[pl.BlockSpec((None, n, d), lambda l, bi: (bi, 0, 0)),
                  pl.BlockSpec((None, 1, d), lambda l, bi: (l, 0, 0)),
                  pl.BlockSpec((None, wd, d), lambda l, bi: (l, 0, 0))],
        out_specs=pl.BlockSpec((None, None, wd, n), lambda l, bi: (l, bi, 0, 0)),
        out_shape=jax.ShapeDtypeStruct((nl, b, wd, n), F32),
        compiler_params=_params("parallel", "parallel"),
        name="mem_kv",
    )(mem, g.reshape(nl, 1, d), jnp.swapaxes(w, 1, 2).astype(BF16))


def _dil_attn_kernel(q_ref, kvc_ref, kvp_ref, bias_ref, acc_ref, st_ref, *, nr, tiles, prev_block):
    i = pl.program_id(2)
    q = (q_ref[...] * SCALE).astype(BF16)
    kvc = kvc_ref[...].astype(BF16)
    kvp = kvp_ref[...].astype(BF16) if prev_block else None
    chains = [(rr, tt, h) for rr in range(nr) for tt in range(tiles) for h in range(GROUP_HEADS)]
    n = len(chains)
    rows = lambda tt: slice(tt * Q_TILE, (tt + 1) * Q_TILE)

    def keys(rr, tt, h, off):
        lanes = slice(rr * 2 * GROUP_W + off + h * HEAD_DIM, rr * 2 * GROUP_W + off + (h + 1) * HEAD_DIM)
        return kvp[:, lanes] if tt < 0 else kvc[rows(tt), lanes]

    has_prev = [tt > 0 or prev_block for _, tt, _ in chains]
    qh = [q[rows(tt), rr * GROUP_W + h * HEAD_DIM:rr * GROUP_W + (h + 1) * HEAD_DIM] for rr, tt, h in chains]
    lc = [_dot_nt(qh[c], keys(rr, tt, h, 0)) + bias_ref[h, :, :Q_TILE] for c, (rr, tt, h) in enumerate(chains)]
    lp = [None] * n
    for c, (rr, tt, h) in enumerate(chains):
        if has_prev[c]:
            z = _dot_nt(qh[c], keys(rr, tt - 1, h, 0)) + bias_ref[h, :, Q_TILE:]
            lp[c] = jnp.where(i > 0, z, NEG_INF) if tt == 0 else z
    m = [jnp.max(z, axis=-1, keepdims=True) for z in lc]
    m = [jnp.maximum(m[c], jnp.max(lp[c], axis=-1, keepdims=True)) if has_prev[c] else m[c] for c in range(n)]
    pc = [jnp.exp(lc[c] - m[c]) for c in range(n)]
    pp = [jnp.exp(lp[c] - m[c]) if has_prev[c] else None for c in range(n)]
    den = [jnp.sum(z, axis=-1, keepdims=True) for z in pc]
    den = [den[c] + jnp.sum(pp[c], axis=-1, keepdims=True) if has_prev[c] else den[c] for c in range(n)]
    o = [_dot(pc[c].astype(BF16), keys(rr, tt, h, GROUP_W)) for c, (rr, tt, h) in enumerate(chains)]
    o = [o[c] + _dot(pp[c].astype(BF16), keys(rr, tt - 1, h, GROUP_W)) if has_prev[c] else o[c]
         for c, (rr, tt, h) in enumerate(chains)]
    lane = lax.broadcasted_iota(jnp.int32, (Q_TILE, 128), 1)
    st = None
    for c, (rr, tt, h) in enumerate(chains):
        acc_ref[rows(tt), rr * GROUP_W + h * HEAD_DIM:rr * GROUP_W + (h + 1) * HEAD_DIM] = o[c]
        st = jnp.zeros((Q_TILE, 128), F32) if h == 0 else st
        st = jnp.where(lane == h, m[c], st)
        st = jnp.where(lane == GROUP_HEADS + h, den[c], st)
        if h == GROUP_HEADS - 1:
            st_ref[rows(tt), rr * 128:(rr + 1) * 128] = st


def _dil_attn(qv, kvv, bias_tiles, g):
    dil = DILATIONS[g]
    b, ln, _ = qv.shape
    tiles = min(ln // Q_TILE, DIL_SEGMENTS)
    nr = min(dil, DIL_SEGMENTS // tiles)
    nblk = ln // (tiles * Q_TILE)
    return pl.pallas_call(
        functools.partial(_dil_attn_kernel, nr=nr, tiles=tiles, prev_block=nblk > 1),
        grid=(b, dil // nr, nblk),
        in_specs=[pl.BlockSpec((None, tiles * Q_TILE, nr * GROUP_W), lambda bi, r, i: (bi, i, r)),
                  pl.BlockSpec((None, tiles * Q_TILE, nr * 2 * GROUP_W), lambda bi, r, i: (bi, i, r)),
                  pl.BlockSpec((None, Q_TILE, nr * 2 * GROUP_W),
                               lambda bi, r, i: (bi, jnp.maximum(tiles * i - 1, 0), r)),
                  pl.BlockSpec((GROUP_HEADS, Q_TILE, 2 * Q_TILE), lambda bi, r, i: (g, 0, 0))],
        out_specs=[pl.BlockSpec((None, tiles * Q_TILE, nr * GROUP_W), lambda bi, r, i: (bi, i, r)),
                   pl.BlockSpec((None, tiles * Q_TILE, nr * 128), lambda bi, r, i: (bi, i, r))],
        out_shape=[jax.ShapeDtypeStruct((b, ln, dil * GROUP_W), F32),
                   jax.ShapeDtypeStruct((b, ln, dil * 128), F32)],
        compiler_params=_params("parallel", "parallel", "arbitrary"),
        name=f"dil_attn_d{dil}",
    )(qv, kvv, kvv, bias_tiles)


def _sample_attn_kernel(q_ref, kn0_ref, kn1_ref, kn2_ref, c0_ref, c1_ref, c2_ref, b0_ref, b1_ref, b2_ref,
                        bnew_ref, o_ref, *, n_new):
    q = (q_ref[...] * SCALE).astype(BF16)
    qf = q.astype(F32)
    groups = ((kn0_ref, c0_ref, b0_ref), (kn1_ref, c1_ref, b1_ref), (kn2_ref, c2_ref, b2_ref))
    new_rows = []
    for g, (kn_ref, _, _) in enumerate(groups):
        kn = kn_ref[...]
        shifted = [kn] + [pltpu.roll(kn, s, axis=0) for s in range(1, n_new if g == 0 else 1)]
        new_rows.append([z.astype(BF16).astype(F32) for z in shifted])
    chains = [(h, g) for h in range(GROUP_HEADS) for g in range(N_GROUPS)]
    ids = range(len(chains))
    ks = lambda h: slice(h * HEAD_DIM, (h + 1) * HEAD_DIM)
    vs = lambda h: slice(GROUP_W + h * HEAD_DIM, GROUP_W + (h + 1) * HEAD_DIM)
    qs = lambda h, g: slice(g * GROUP_W + h * HEAD_DIM, g * GROUP_W + (h + 1) * HEAD_DIM)
    lp = [_dot(q[:, qs(h, g)], groups[g][1][0, h].astype(BF16)) + groups[g][2][h] for h, g in chains]
    lns = [[jnp.sum(qf[:, qs(h, g)] * rows[:, ks(h)], axis=-1, keepdims=True) + bnew_ref[g, s][:, h:h + 1]
            for s, rows in enumerate(new_rows[g])] for h, g in chains]
    m = [functools.reduce(jnp.maximum, [jnp.max(lp[c], axis=-1, keepdims=True)] + lns[c]) for c in ids]
    pp = [jnp.exp(lp[c] - m[c]) for c in ids]
    pn = [[jnp.exp(ln - m[c]) for ln in lns[c]] for c in ids]
    den = [jnp.sum(pp[c], axis=-1, keepdims=True) + sum(pn[c]) for c in ids]
    o = [_dot_nt(pp[c].astype(BF16), groups[g][1][1, h].astype(BF16)) for c, (h, g) in enumerate(chains)]
    o = [o[c] + sum(z.astype(BF16).astype(F32) * rows[:, vs(h)] for z, rows in zip(pn[c], new_rows[g]))
         for c, (h, g) in enumerate(chains)]
    for h in range(GROUP_HEADS):
        cs = [h * N_GROUPS + g for g in range(N_GROUPS)]
        mx = functools.reduce(jnp.maximum, [m[c] for c in cs])
        wts = [jnp.exp(m[c] - mx) * den[c] for c in cs]
        num = sum(w * (o[c] / den[c]) for w, c in zip(wts, cs))
        o_ref[:, ks(h)] = num / sum(wts)


def _sample_attn(q, kvn, caches, bias_past, bias_new, n_new):
    b, tp, _ = q.shape
    new_spec = pl.BlockSpec((None, tp, 2 * GROUP_W), lambda bi: (bi, 0, 0))
    cache_spec = lambda w: pl.BlockSpec((None, 2, GROUP_HEADS, HEAD_DIM, w), lambda bi: (bi, 0, 0, 0, 0))
    bias_spec = lambda w: pl.BlockSpec((GROUP_HEADS, tp, w), lambda bi: (0, 0, 0))
    return pl.pallas_call(
        functools.partial(_sample_attn_kernel, n_new=n_new),
        grid=(b,),
        in_specs=[pl.BlockSpec((None, tp, A_Q), lambda bi: (bi, 0, 0)), new_spec, new_spec, new_spec]
                 + [cache_spec(w) for w in WINDOWS] + [bias_spec(w) for w in WINDOWS]
                 + [pl.BlockSpec((N_GROUPS, n_new, tp, 128), lambda bi: (0, 0, 0, 0))],
        out_specs=pl.BlockSpec((None, tp, GROUP_W), lambda bi: (bi, 0, 0)),
        out_shape=jax.ShapeDtypeStruct((b, tp, GROUP_W), F32),
        compiler_params=_params("parallel"),
        name="sample_attn",
    )(q, kvn[0], kvn[1], kvn[2], *caches, *bias_past, bias_new)


def _cross_attn_kernel(q_ref, kvt_ref, o_ref, *, sub):
    nb, tq, _ = q_ref.shape
    chains = [(bi, slice(r0, r0 + sub), h) for bi in range(nb) for r0 in range(0, tq, sub)
              for h in range(MEM_W // HEAD_DIM)]
    q = [(q_ref[bi] * SCALE).astype(BF16) for bi in range(nb)]
    kvt = [kvt_ref[bi].astype(BF16) for bi in range(nb)]
    ks = lambda h: slice(h * HEAD_DIM, (h + 1) * HEAD_DIM)
    vs = lambda h: slice(MEM_W + h * HEAD_DIM, MEM_W + (h + 1) * HEAD_DIM)
    lg = [_dot(q[bi][rw, ks(h)], kvt[bi][ks(h), :]) for bi, rw, h in chains]
    p = [jnp.exp(z - jnp.max(z, axis=-1, keepdims=True)) for z in lg]
    den = [jnp.sum(z, axis=-1, keepdims=True) for z in p]
    o = [_dot_nt(p[c].astype(BF16), kvt[bi][vs(h), :]) for c, (bi, rw, h) in enumerate(chains)]
    for c, (bi, rw, h) in enumerate(chains):
        o_ref[bi, rw, ks(h)] = o[c] / den[c]


def _cross_attn(q, kvt, layer, nb, tq):
    b, t, _ = q.shape
    return pl.pallas_call(
        functools.partial(_cross_attn_kernel, sub=min(tq, Q_TILE)),
        grid=(b // nb, t // tq),
        in_specs=[pl.BlockSpec((nb, tq, MEM_W), lambda bi, i: (bi, i, 0)),
                  pl.BlockSpec((None, nb, 2 * MEM_W, N_MEM), lambda bi, i: (layer, bi, 0, 0))],
        out_specs=pl.BlockSpec((nb, tq, MEM_W), lambda bi, i: (bi, i, 0)),
        out_shape=jax.ShapeDtypeStruct((b, t, MEM_W), F32),
        compiler_params=_params("parallel", "parallel"),
        name="cross_attn",
    )(q, kvt)


def _rwkv_kernel(cols_ref, shift_ref, s0_ref, mu_ref, w0_ref, wup_ref, a0_ref, aup_ref, kk_ref, ka_ref, rk_ref,
                 lnw_ref, lnb_ref, y_ref, sout_ref, shout_ref,
                 carry, state, r_s, k_s, v_s, a_s, b_s, lw_s, *, tt, chunk, t_valid):
    j = pl.program_id(1)

    @pl.when(j == 0)
    def _():
        carry[...] = shift_ref[...]
        state[...] = s0_ref[...]

    cols = cols_ref[...]
    row = lax.broadcasted_iota(jnp.int32, (tt, 1), 0)
    prev = jnp.where(row == 0, carry[...], pltpu.roll(cols, 1, axis=0))
    carry[...] = cols[t_valid - 1:t_valid, :]
    xs = cols + mu_ref[...] * (prev - cols)
    r = xs[:, :RWKV_W]
    k = xs[:, RWKV_W:2 * RWKV_W]
    v = xs[:, 2 * RWKV_W:3 * RWKV_W]
    wd = xs[:, 3 * RWKV_W:3 * RWKV_W + LORA]
    ad = xs[:, 3 * RWKV_W + LORA:]
    z = w0_ref[...] + _dot(jnp.tanh(wd).astype(BF16), wup_ref[...])
    log_decay = -math.exp(-0.5) * jax.nn.sigmoid(z)
    a = jax.nn.sigmoid(a0_ref[...] + _dot(ad.astype(BF16), aup_ref[...]))

    blockdiag = (lax.broadcasted_iota(jnp.int32, (128, 128), 0) // HEAD_DIM
                 == lax.broadcasted_iota(jnp.int32, (128, 128), 1) // HEAD_DIM)

    def head_sum(x):
        lo = lax.broadcasted_iota(jnp.int32, x.shape, 1) < HEAD_DIM
        s0 = jnp.sum(jnp.where(lo, x, 0.0), axis=-1, keepdims=True)
        s1 = jnp.sum(jnp.where(lo, 0.0, x), axis=-1, keepdims=True)
        return jnp.where(lo, s0, s1)

    kk = k * kk_ref[...]
    kk_sq = kk * kk
    k2 = k * (1.0 + (a - 1.0) * ka_ref[...])
    live = row < t_valid
    for p in range(RWKV_PAIRS):
        ps = slice(p * 128, (p + 1) * 128)
        nrm = jnp.maximum(jnp.sqrt(head_sum(kk_sq[:, ps])), 1e-12)
        kkn = kk[:, ps] / nrm
        if t_valid < tt:
            zero = jnp.zeros((tt, 128), F32)
            r_s[:, ps] = r[:, ps]
            k_s[:, ps] = jnp.where(live, k2[:, ps], zero)
            v_s[:, ps] = jnp.where(live, v[:, ps], zero)
            a_s[:, ps] = jnp.where(live, -kkn, zero)
            b_s[:, ps] = jnp.where(live, kkn * a[:, ps], zero)
            lw_s[:, ps] = jnp.where(live, log_decay[:, ps], zero)
        else:
            r_s[:, ps] = r[:, ps]
            k_s[:, ps] = k2[:, ps]
            v_s[:, ps] = v[:, ps]
            a_s[:, ps] = -kkn
            b_s[:, ps] = kkn * a[:, ps]
            lw_s[:, ps] = log_decay[:, ps]

    ci = lax.broadcasted_iota(jnp.int32, (chunk, chunk), 0)
    cj = lax.broadcasted_iota(jnp.int32, (chunk, chunk), 1)
    tri_incl = (ci >= cj).astype(BF16)
    lo_lane = lax.broadcasted_iota(jnp.int32, (chunk, 128), 1) < HEAD_DIM
    levels = max(1, math.ceil(math.log2(min(chunk, t_valid))))
    n2 = 2 * chunk
    ri = lax.broadcasted_iota(jnp.int32, (n2, n2), 0)
    rj = lax.broadcasted_iota(jnp.int32, (n2, n2), 1)
    same = ri // chunk == rj // chunk
    strict = jnp.logical_and(same, ri > rj)
    incl = jnp.logical_and(same, ri >= rj)
    own = lax.broadcasted_iota(jnp.int32, (n2, 128), 0) // chunk == lax.broadcasted_iota(jnp.int32, (n2, 128), 1) // HEAD_DIM
    dup = lambda z: jnp.concatenate([z, z], axis=0)
    cat = jnp.concatenate

    n_chunks = tt // chunk
    group = SCAN_INTERLEAVE if n_chunks % SCAN_INTERLEAVE == 0 else 1

    def chunk_body(ci, _):
        rows = [pl.ds(pl.multiple_of((ci * group + cc) * chunk, chunk), chunk) for cc in range(group)]
        sl = [slice(p * 128, (p + 1) * 128) for p in range(RWKV_PAIRS)]
        chains = [(rows[cc], sl[p]) for cc in range(group) for p in range(RWKV_PAIRS)]
        ids = range(len(chains))
        cum_all = [_mm(tri_incl, lw_s[rw, :], pb=3) for rw in rows]
        cum = [cum_all[cc][:, s] for cc in range(group) for s in sl]
        p_incl = [jnp.exp(z) for z in cum]
        p_inv = [jnp.exp(-z) for z in cum]
        rr = [r_s[rw, s] for rw, s in chains]
        kc = [k_s[rw, s] for rw, s in chains]
        vc = [v_s[rw, s] for rw, s in chains]
        at2 = [jnp.where(own, dup(a_s[rw, s] * jnp.exp(cum[c] - lw_s[rw, s])), 0.0) for c, (rw, s) in enumerate(chains)]
        rt2 = [jnp.where(own, dup(rr[c] * p_incl[c]), 0.0) for c in ids]
        bt = [(b_s[rw, s] * p_inv[c]).astype(BF16) for c, (rw, s) in enumerate(chains)]
        kt = [(kc[c] * p_inv[c]).astype(BF16) for c in ids]
        vb = [z.astype(BF16) for z in vc]
        v2 = [dup(z) for z in vb]
        g = [_mm(cat([at2[c], rt2[c]], 0).astype(BF16), cat([dup(bt[c]), dup(kt[c])], 0), NT) for c in ids]
        a_ak = [jnp.where(strict, z[:n2, n2:], 0.0).astype(BF16) for z in g]
        apow = [jnp.where(strict, z[:n2, :n2], 0.0).astype(BF16) for z in g]
        a_r = [cat([jnp.where(incl, z[n2:, :n2], 0.0), jnp.where(incl, z[n2:, n2:], 0.0)], 1).astype(BF16)
               for z in g]
        akv = [_mm(a_ak[c], v2[c]) for c in ids]
        sol = [cat([at2[c], akv[c]], axis=1) for c in ids]
        for lvl in range(levels):
            sol = [sol[c] + _mm(apow[c], sol[c].astype(BF16)) for c in ids]
            if lvl + 1 < levels:
                apow = [_mm(z, z).astype(BF16) for z in apow]
        ws = [z[:, :128] for z in sol]
        u0s = [jnp.where(own, z[:, 128:], 0.0) for z in sol]
        zeros2 = jnp.zeros((n2, 128), BF16)
        qy = [_mm(a_r[c], cat([cat([ws[c], u0s[c]], 1).astype(BF16), cat([zeros2, v2[c]], 1)], 0)) for c in ids]
        qs = [(rt2[c] + qy[c][:, :128]).astype(BF16) for c in ids]
        zeros1 = jnp.zeros((chunk, 128), BF16)
        lhs = [cat([cat([ws[c][:chunk] + ws[c][chunk:], u0s[c][:chunk] + u0s[c][chunk:]], 1).astype(BF16),
                    cat([zeros1, vb[c]], 1)], 0) for c in ids]
        wbn = [_mm(lhs[c], cat([bt[c], kt[c]], 0), TN) for c in ids]
        wb = [z[:128].astype(BF16) for z in wbn]
        s_cur = [state[p] for p in range(RWKV_PAIRS)]
        ys = []
        for cc in range(group):
            base = cc * RWKV_PAIRS
            s_b = [z.astype(BF16) for z in s_cur]
            ys += [_mm(qs[base + p], s_b[p], NT) + qy[base + p][:, 128:] for p in range(RWKV_PAIRS)]
            sw = [_mm(s_b[p], wb[base + p]) for p in range(RWKV_PAIRS)]
            s_cur = [jnp.where(blockdiag, (s_cur[p] + sw[p] + wbn[base + p][128:])
                               * p_incl[base + p][chunk - 1:chunk, :], 0.0) for p in range(RWKV_PAIRS)]
        for p in range(RWKV_PAIRS):
            state[p] = s_cur[p]
        y = [jnp.where(lo_lane, z[:chunk], z[chunk:]) for z in ys]

        bonus = [head_sum(rr[c] * kc[c] * rk_ref[:, s]) for c, (_, s) in enumerate(chains)]
        dlt = [y[c] - head_sum(y[c]) * (1.0 / HEAD_DIM) for c in ids]
        var = [head_sum(z * z) * (1.0 / HEAD_DIM) for z in dlt]
        for c, (rw, s) in enumerate(chains):
            yn = dlt[c] * lax.rsqrt(var[c] + GN_EPS) * lnw_ref[:, s] + lnb_ref[:, s]
            y_ref[rw, s] = yn + bonus[c] * vc[c]
        return 0

    lax.fori_loop(0, n_chunks // group, chunk_body, 0)

    @pl.when(j == pl.num_programs(1) - 1)
    def _():
        sout_ref[...] = state[...]
        shout_ref[...] = carry[...]


def _rwkv_scan(cols, shift_prev, s0, prm, tt, t_valid):
    b, t, _ = cols.shape
    s = s0.reshape(b, RWKV_PAIRS, 2, HEAD_DIM, HEAD_DIM)
    z = jnp.zeros_like(s[:, :, 0])
    s_bd = jnp.concatenate([jnp.concatenate([s[:, :, 0], z], -1), jnp.concatenate([z, s[:, :, 1]], -1)], -2)
    row = lambda n: pl.BlockSpec((1, n), lambda bi, j: (0, 0))
    vec = lambda a: a.reshape(1, -1)
    y, s_out, sh_out = pl.pallas_call(
        functools.partial(_rwkv_kernel, tt=tt, chunk=SCAN_CHUNK, t_valid=min(t_valid, tt)),
        grid=(b, t // tt),
        in_specs=[pl.BlockSpec((None, tt, C_SHIFT), lambda bi, j: (bi, j, 0)),
                  pl.BlockSpec((None, 1, C_SHIFT), lambda bi, j: (bi, 0, 0)),
                  pl.BlockSpec((None, RWKV_PAIRS, 128, 128), lambda bi, j: (bi, 0, 0, 0)),
                  row(C_SHIFT), row(RWKV_W),
                  pl.BlockSpec((LORA, RWKV_W), lambda bi, j: (0, 0)),
                  row(RWKV_W),
                  pl.BlockSpec((LORA, RWKV_W), lambda bi, j: (0, 0)),
                  row(RWKV_W), row(RWKV_W), row(RWKV_W), row(RWKV_W), row(RWKV_W)],
        out_specs=[pl.BlockSpec((None, tt, RWKV_W), lambda bi, j: (bi, j, 0)),
                   pl.BlockSpec((None, RWKV_PAIRS, 128, 128), lambda bi, j: (bi, 0, 0, 0)),
                   pl.BlockSpec((None, 1, C_SHIFT), lambda bi, j: (bi, 0, 0))],
        out_shape=[jax.ShapeDtypeStruct((b, t, RWKV_W), F32),
                   jax.ShapeDtypeStruct((b, RWKV_PAIRS, 128, 128), F32),
                   jax.ShapeDtypeStruct((b, 1, C_SHIFT), F32)],
        scratch_shapes=[pltpu.VMEM((1, C_SHIFT), F32), pltpu.VMEM((RWKV_PAIRS, 128, 128), F32)]
                       + [pltpu.VMEM((tt, RWKV_W), F32)] * 6,
        compiler_params=_params("parallel", "arbitrary"),
        name="rwkv_scan",
    )(cols, shift_prev.reshape(b, 1, C_SHIFT), s_bd, vec(prm["mu"]), vec(prm["w0"]), prm["w_up"].astype(BF16),
      vec(prm["a0"]), prm["a_up"].astype(BF16), vec(prm["k_k"]), vec(prm["k_a"]), vec(prm["r_k"]),
      vec(prm["ln_w"]), vec(prm["ln_b"]))
    s_fin = jnp.stack([s_out[:, :, :HEAD_DIM, :HEAD_DIM], s_out[:, :, HEAD_DIM:, HEAD_DIM:]], axis=2)
    return y, s_fin.reshape(b, 2 * RWKV_PAIRS, HEAD_DIM, HEAD_DIM), sh_out.reshape(b, C_SHIFT)


def _post_kernel(*refs, merge_dils, mix_w):
    x_ref, omem_ref, gate_ref, w_ref, g_ref = refs[:5]
    mix_refs = refs[5:-2]
    y_ref, stage = refs[-2:]
    if merge_dils:
        er = lax.broadcasted_iota(jnp.int32, (128, 2 * GROUP_W), 0)
        ec = lax.broadcasted_iota(jnp.int32, (128, 2 * GROUP_W), 1)
        expand = (er == ec // HEAD_DIM).astype(BF16)
        ms, ss, os_ = [], [], []
        for g, dil in enumerate(merge_dils):
            acc_ref, st_ref = mix_refs[2 * g], mix_refs[2 * g + 1]
            acc = acc_ref[...] if dil == 1 else _from_classes(acc_ref, stage, dil, GROUP_W)
            stats = st_ref[...] if dil == 1 else _from_classes(st_ref, stage, dil, 128)
            st = _mm(stats, expand, pa=3)
            ms.append(st[:, :GROUP_W])
            ss.append(st[:, GROUP_W:])
            os_.append(acc / st[:, GROUP_W:])
        mx = functools.reduce(jnp.maximum, ms)
        wts = [jnp.exp(m - mx) * s for m, s in zip(ms, ss)]
        mix = sum(w * o for w, o in zip(wts, os_)) / sum(wts)
    else:
        mix = mix_refs[0][...]
    gate = gate_ref[...]
    act = gate * jax.nn.sigmoid(gate)
    h1 = (mix * act[:, :mix_w]).astype(BF16)
    h2 = (omem_ref[...] * act[:, mix_w:]).astype(BF16)
    out = _dot(h1, w_ref[:mix_w, :]) + _dot(h2, w_ref[mix_w:, :])
    y_ref[...] = x_ref[...] + _rms(out, g_ref[...])


def _post(x, mix, o_mem, gate, w_out, g_post, tm):
    m, d = x.shape
    merge = isinstance(mix, (list, tuple))
    gw = gate.shape[1]
    rows = lambda wd: pl.BlockSpec((tm, wd), lambda i: (i, 0))
    if merge:
        mix_w = GROUP_W
        merge_dils = tuple(acc.shape[2] // GROUP_W for acc, _ in mix)
        tiles = mix[0][0].shape[1] * merge_dils[0] // tm
        mix_arrays = [a for pair in mix for a in pair]
        mix_specs = [pl.BlockSpec((None, tm // dil, a.shape[2]), lambda i: (i // tiles, i % tiles, 0))
                     for dil, pair in zip(merge_dils, mix) for a in pair]
    else:
        mix_w, merge_dils, mix_arrays, mix_specs = mix.shape[1], (), [mix], [rows(mix.shape[1])]
    return pl.pallas_call(
        functools.partial(_post_kernel, merge_dils=merge_dils, mix_w=mix_w),
        grid=(m // tm,),
        in_specs=[rows(d), rows(MEM_W), rows(gw),
                  pl.BlockSpec((gw, d), lambda i: (0, 0)),
                  pl.BlockSpec((1, d), lambda i: (0, 0))] + mix_specs,
        out_specs=rows(d),
        out_shape=jax.ShapeDtypeStruct((m, d), F32),
        scratch_shapes=[pltpu.VMEM((tm, 128), F32)],
        compiler_params=_params("parallel"),
        name="post",
    )(x, o_mem, gate, w_out.astype(BF16), g_post.reshape(1, d), *mix_arrays)


def _t5_bucket(dist):
    max_exact = N_BUCKETS // 2
    d = jnp.maximum(dist, 1).astype(F32)
    large = max_exact + (jnp.log(d / max_exact) / math.log(MAX_DISTANCE / max_exact)
                         * (N_BUCKETS - max_exact)).astype(jnp.int32)
    large = jnp.minimum(large, N_BUCKETS - 1)
    return jnp.where(dist < max_exact, dist, large)


def _group_bias(rel_bias, g):
    dist = DILATIONS[g] * jnp.arange(A_KEYS, dtype=jnp.int32)
    bias = rel_bias[_t5_bucket(dist)]
    return bias[:, g * GROUP_HEADS:(g + 1) * GROUP_HEADS].T.astype(F32)


def _bias_tiles_kernel(x_ref, o_ref):
    row = lax.broadcasted_iota(jnp.int32, (Q_TILE, 2 * Q_TILE), 0)
    for h in range(o_ref.shape[0]):
        t = jnp.broadcast_to(x_ref[h:h + 1, :], (Q_TILE, 2 * Q_TILE))
        for bit in range(int(math.log2(Q_TILE))):
            t = jnp.where((row >> bit) & 1 == 1, pltpu.roll(t, 1 << bit, axis=1), t)
        o_ref[h] = t


def _prompt_bias(biases):
    bias = jnp.concatenate(biases, axis=0)
    neg = jnp.full((bias.shape[0], Q_TILE - 1), NEG_INF, F32)
    table = jnp.concatenate([bias[:, :1], neg, bias[:, :0:-1]], axis=1)
    n = table.shape[0]
    return pl.pallas_call(
        _bias_tiles_kernel,
        out_shape=jax.ShapeDtypeStruct((n, Q_TILE, 2 * Q_TILE), F32),
        name="bias_tiles",
    )(table)


def _sample_bias(biases, n_new, tp):
    t = np.arange(tp)[:, None]
    real = t < n_new
    past = []
    for g, bias in enumerate(biases):
        w, dil = WINDOWS[g], DILATIONS[g]
        spread = jnp.concatenate([bias[:, :0:-1, None], jnp.full((GROUP_HEADS, w // dil, dil - 1), NEG_INF, F32)],
                                 axis=-1).reshape(GROUP_HEADS, w)
        rows = [jnp.concatenate([jnp.full((GROUP_HEADS, tt), NEG_INF, F32), spread[:, :w - tt]], axis=-1)
                for tt in range(n_new)] + [jnp.zeros((GROUP_HEADS, w), F32)] * (tp - n_new)
        past.append(jnp.stack(rows, axis=1))
    s = np.arange(n_new)[None, :, None]
    g_idx = np.arange(N_GROUPS)[:, None, None]
    ok = (s <= t[None, :, 0]) & real[None, :, 0] & ((g_idx == 0) | (s == 0))
    vals = jnp.stack([bias[:, :n_new].T for bias in biases])
    new = jnp.where(jnp.asarray(ok)[..., None], vals[:, :, None, :], NEG_INF)
    return past, jnp.pad(new, ((0, 0), (0, 0), (0, 0), (0, 128 - GROUP_HEADS)))


def _pieces_a():
    q = [((g * GROUP_W, (g + 1) * GROUP_W),) for g in range(N_GROUPS)]
    kv = [((A_Q + g * GROUP_W, A_Q + (g + 1) * GROUP_W), (2 * A_Q + g * GROUP_W, 2 * A_Q + (g + 1) * GROUP_W))
          for g in range(N_GROUPS)]
    qmem = ((3 * A_Q, 3 * A_Q + MEM_W),)
    gate = ((3 * A_Q + MEM_W, 3 * A_Q + MEM_W + GROUP_W + MEM_W),)
    return tuple(q + kv + [qmem, gate])


def _pieces_b():
    return (((0, C_SHIFT),), ((C_SHIFT, C_SHIFT + MEM_W),), ((C_SHIFT + MEM_W, C_SHIFT + MEM_W + RWKV_W + MEM_W),))


def kernel(x_prompt, x_sample, mem_prompt, cache_mem_kv, cache_win0, cache_win1, cache_win2, state_wkv, state_shift, norm_pre, norm_post, norm_mem, w_mem_kv, rel_bias, w_in_a, w_out_a, w_in_b, w_out_b, rwkv_mu, rwkv_w0, rwkv_w_up, rwkv_a0, rwkv_a_up, rwkv_k_k, rwkv_k_a, rwkv_r_k, rwkv_ln_w, rwkv_ln_b):
    bp, tp, d = x_prompt.shape
    bs, ts, _ = x_sample.shape
    tsp = SAMPLE_PAD_T
    xp = x_prompt.reshape(bp * tp, d)
    xs = jnp.pad(x_sample, ((0, 0), (0, tsp - ts), (0, 0))).reshape(bs * tsp, d)
    tm_p, tm_s = 512, bs * tsp
    time_minor = lambda c: jnp.moveaxis(c, -4, -1)
    caches = [time_minor(c[0]) for c in (cache_win0, cache_win1, cache_win2)]
    mkv_s = time_minor(cache_mem_kv).reshape(cache_mem_kv.shape[0], bs, 2 * MEM_W, N_MEM)
    mkv_p = _mem_kv(mem_prompt, norm_mem, w_mem_kv)

    biases = [_group_bias(rel_bias, g) for g in range(N_GROUPS)]
    outs_p = _norm_proj(xp, norm_pre[0], w_in_a[0], _pieces_a(), tm_p, transposed=(3, 4, 5),
                        dils=DILATIONS + DILATIONS + (1, 1), rows_per_batch=tp)
    outs_s = _norm_proj(xs, norm_pre[0], w_in_a[0], _pieces_a(), tm_s)
    q_p, kv_p, qmem_p, gate_p, kvt_p = outs_p[0:3], outs_p[3:6], outs_p[6], outs_p[7], outs_p[8:11]
    q_s, kv_s, qmem_s, gate_s = outs_s[0:3], outs_s[3:6], outs_s[6], outs_s[7]

    bias_tiles = _prompt_bias(biases)
    merged = [_dil_attn(q_p[g].reshape(bp, tp // DILATIONS[g], -1), kv_p[g].reshape(bp, tp // DILATIONS[g], -1),
                        bias_tiles, g) for g in range(N_GROUPS)]
    omem_p = _cross_attn(qmem_p.reshape(bp, tp, MEM_W), mkv_p, 0, 1, 512)
    xp1 = _post(xp, merged, omem_p.reshape(bp * tp, MEM_W), gate_p, w_out_a[0], norm_post[0], tm_p)

    bias_past, bias_new = _sample_bias(biases, ts, tsp)
    q_s_all = jnp.concatenate(q_s, axis=-1).reshape(bs, tsp, A_Q)
    kvn = [a.reshape(bs, tsp, 2 * GROUP_W) for a in kv_s]
    o_s = _sample_attn(q_s_all, kvn, caches, bias_past, bias_new, ts)
    omem_s = _cross_attn(qmem_s.reshape(bs, tsp, MEM_W), mkv_s, 0, 4, tsp)
    xs1 = _post(xs, o_s.reshape(bs * tsp, GROUP_W), omem_s.reshape(bs * tsp, MEM_W), gate_s, w_out_a[0],
                norm_post[0], tm_s)

    cols_p, qmem_p, gate_p = _norm_proj(xp1, norm_pre[1], w_in_b[0], _pieces_b(), tm_p)
    cols_s, qmem_s, gate_s = _norm_proj(xs1, norm_pre[1], w_in_b[0], _pieces_b(), tm_s)
    prm = dict(mu=rwkv_mu[0], w0=rwkv_w0[0], w_up=rwkv_w_up[0], a0=rwkv_a0[0], a_up=rwkv_a_up[0], k_k=rwkv_k_k[0],
               k_a=rwkv_k_a[0], r_k=rwkv_r_k[0], ln_w=rwkv_ln_w[0], ln_b=rwkv_ln_b[0])
    y_p, wkv_p, sh_p = _rwkv_scan(cols_p.reshape(bp, tp, C_SHIFT), jnp.zeros((bp, C_SHIFT), F32),
                                  jnp.zeros((bp, 2 * RWKV_PAIRS, HEAD_DIM, HEAD_DIM), F32), prm, 256, tp)
    cols_s_pad = jnp.pad(cols_s.reshape(bs, tsp, C_SHIFT), ((0, 0), (0, SCAN_CHUNK - tsp), (0, 0)))
    y_s, wkv_s, sh_s = _rwkv_scan(cols_s_pad, state_shift[0], state_wkv[0], prm, SCAN_CHUNK, ts)
    y_s = y_s[:, :tsp].reshape(bs * tsp, RWKV_W)
    omem_p = _cross_attn(qmem_p.reshape(bp, tp, MEM_W), mkv_p, 1, 1, 512)
    omem_s = _cross_attn(qmem_s.reshape(bs, tsp, MEM_W), mkv_s, 1, 4, tsp)
    xp2 = _post(xp1, y_p.reshape(bp * tp, RWKV_W), omem_p.reshape(bp * tp, MEM_W), gate_p, w_out_b[0],
                norm_post[1], tm_p)
    xs2 = _post(xs1, y_s, omem_s.reshape(bs * tsp, MEM_W), gate_s, w_out_b[0], norm_post[1], tm_s)

    kv_shape = (2, GROUP_HEADS, HEAD_DIM)
    time_major = lambda c: jnp.moveaxis(c, -1, -4)
    new_mem_kv = time_major(mkv_p.reshape(mkv_p.shape[0], bp, 2, MEM_W // HEAD_DIM, HEAD_DIM, N_MEM))
    win_p = [time_major(kvt_p[g].reshape(bp, *kv_shape, tp)[..., tp - min(WINDOWS[g], tp):])[None]
             for g in range(N_GROUPS)]
    win_s = [kv_s[g].reshape(bs, tsp, *kv_shape)[None, :, :ts] for g in range(N_GROUPS)]
    return (xp2.reshape(bp, tp, d), xs2.reshape(bs, tsp, d)[:, :ts], new_mem_kv,
            win_p[0], win_p[1], win_p[2], win_s[0], win_s[1], win_s[2],
            wkv_p[None], wkv_s[None], sh_p[None], sh_s[None])
```

```python
import functools
import math

import numpy as np
import jax
import jax.numpy as jnp
from jax import lax
from jax.experimental import pallas as pl
from jax.experimental.pallas import tpu as pltpu

F32 = jnp.float32
BF16 = jnp.bfloat16

D_MODEL = 1024
HEAD_DIM = 64
N_GROUPS = 3
GROUP_HEADS = 4
WINDOWS = (128, 512, 2048)
DILATIONS = (1, 4, 16)
A_HEADS = N_GROUPS * GROUP_HEADS
A_KEYS = 129
GROUP_W = GROUP_HEADS * HEAD_DIM
A_Q = A_HEADS * HEAD_DIM
N_MEM = 256
MEM_W = 256
RWKV_W = 768
RWKV_PAIRS = RWKV_W // 128
LORA = 64
C_SHIFT = 3 * RWKV_W + 2 * LORA
N_BUCKETS = 32
MAX_DISTANCE = WINDOWS[-1]
RMS_EPS = 1e-6
GN_EPS = 64e-5
NEG_INF = -1e30
SCALE = HEAD_DIM ** -0.5

Q_TILE = 128
DIL_SEGMENTS = 4
SCAN_CHUNK = 64
SCAN_INTERLEAVE = 2
SAMPLE_PAD_T = 8
VMEM_LIMIT = 56 * 1024 * 1024
NN =(((1,), (0,)), ((), ()))
NT = (((1,), (1,)), ((), ()))
TN = (((0,), (0,)), ((), ()))


def _params(*sem):
    return pltpu.CompilerParams(dimension_semantics=sem, vmem_limit_bytes=VMEM_LIMIT)


def _dot(a, b):
    return jnp.dot(a, b, preferred_element_type=F32)


def _dot_nt(a, b):
    return lax.dot_general(a, b, NT, preferred_element_type=F32)


def _split(x, n):
    if x.dtype == BF16:
        return [x]
    parts, rem = [], x
    for i in range(n):
        parts.append(rem.astype(BF16))
        if i + 1 < n:
            rem = rem - parts[-1].astype(F32)
    return parts


def _mm(a, b, dims=NN, pa=1, pb=1):
    pas, pbs = _split(a, pa), _split(b, pb)
    order = max(len(pas), len(pbs))
    out = None
    for i, ai in enumerate(pas):
        for j, bj in enumerate(pbs):
            if i + j < order:
                term = lax.dot_general(ai, bj, dims, preferred_element_type=F32)
                out = term if out is None else out + term
    return out


def _rms(x, g):
    return x * lax.rsqrt(jnp.mean(x * x, axis=-1, keepdims=True) + RMS_EPS) * g


def _to_classes(dst_ref, stage, val, dil, width, off):
    tm = val.shape[0]
    for j in range(val.shape[1] // 128):
        stage[...] = val[:, 128 * j:128 * (j + 1)]
        for r in range(dil):
            lane0 = r * width + off + 128 * j
            dst_ref[:, lane0:lane0 + 128] = stage[pl.ds(r, tm // dil, stride=dil), :]


def _from_classes(src_ref, stage, dil, width):
    tm = stage.shape[0]
    chunks = []
    for j in range(width // 128):
        for r in range(dil):
            lane0 = r * width + 128 * j
            stage[pl.ds(r, tm // dil, stride=dil), :] = src_ref[:, lane0:lane0 + 128]
        chunks.append(stage[...])
    return jnp.concatenate(chunks, axis=1)


def _norm_proj_kernel(x_ref, g_ref, w_ref, *refs, pieces, transposed, dils):
    stage = refs[-1]
    out_refs = refs[:-1]
    xn = _rms(x_ref[...], g_ref[...]).astype(BF16)
    t_refs = iter(out_refs[len(pieces):])
    for idx, (out_ref, cols) in enumerate(zip(out_refs, pieces)):
        t_ref = next(t_refs) if idx in transposed else None
        width = sum(c1 - c0 for c0, c1 in cols)
        off = 0
        for c0, c1 in cols:
            res = _dot(xn, w_ref[:, c0:c1])
            if dils[idx] > 1:
                _to_classes(out_ref, stage, res, dils[idx], width, off)
            else:
                out_ref[:, off:off + c1 - c0] = res
            if t_ref is not None:
                t_ref[off:off + c1 - c0, :] = res.T
            off += c1 - c0


def _norm_proj(x, g, w, pieces, tm, transposed=(), dils=None, rows_per_batch=None):
    m, d = x.shape
    n = w.shape[1]
    dils = tuple(dils) if dils else (1,) * len(pieces)
    widths = [sum(c1 - c0 for c0, c1 in cols) for cols in pieces]
    tiles = rows_per_batch // tm if rows_per_batch else None
    out_specs, out_shape = [], []
    for wd, dil in zip(widths, dils):
        if dil > 1:
            out_specs.append(pl.BlockSpec((None, tm // dil, dil * wd), lambda i: (i // tiles, i % tiles, 0)))
            out_shape.append(jax.ShapeDtypeStruct((m // rows_per_batch, rows_per_batch // dil, dil * wd), F32))
        else:
            out_specs.append(pl.BlockSpec((tm, wd), lambda i: (i, 0)))
            out_shape.append(jax.ShapeDtypeStruct((m, wd), F32))
    out_specs += [pl.BlockSpec((None, widths[idx], tm), lambda i: (i // tiles, 0, i % tiles)) for idx in transposed]
    out_shape += [jax.ShapeDtypeStruct((m // rows_per_batch, widths[idx], rows_per_batch), F32) for idx in transposed]
    return pl.pallas_call(
        functools.partial(_norm_proj_kernel, pieces=pieces, transposed=tuple(transposed), dils=dils),
        grid=(m // tm,),
        in_specs=[pl.BlockSpec((tm, d), lambda i: (i, 0)),
                  pl.BlockSpec((1, d), lambda i: (0, 0)),
                  pl.BlockSpec((d, n), lambda i: (0, 0))],
        out_specs=out_specs,
        out_shape=out_shape,
        scratch_shapes=[pltpu.VMEM((tm, 128), F32)],
        compiler_params=_params("parallel"),
        name="norm_proj",
    )(x, g.reshape(1, d), w.astype(BF16))


def _mem_kv_kernel(x_ref, g_ref, wt_ref, o_ref):
    xn = _rms(x_ref[...], g_ref[...]).astype(BF16)
    o_ref[...] = _dot_nt(wt_ref[...], xn)


def _mem_kv(mem, g, w):
    b, n, d = mem.shape
    nl, _, wd = w.shape
    return pl.pallas_call(
        _mem_kv_kernel,
        grid=(nl, b),
        in_specs=[pl.BlockSpec((None, n, d), lambda l, bi: (bi, 0, 0)),
                  pl.BlockSpec((None, 1, d), lambda l, bi: (l, 0, 0)),
                  pl.BlockSpec((None, wd, d), lambda l, bi: (l, 0, 0))],
        out_specs=pl.BlockSpec((None, None, wd, n), lambda l, bi: (l, bi, 0, 0)),
        out_shape=jax.ShapeDtypeStruct((nl, b, wd, n), F32),
        compiler_params=_params("parallel", "parallel"),
        name="mem_kv",
    )(mem, g.reshape(nl, 1, d), jnp.swapaxes(w, 1, 2).astype(BF16))


def _dil_attn_kernel(q_ref, kvc_ref, kvp_ref, bias_ref, acc_ref, st_ref, *, nr, tiles, prev_block):
    i = pl.program_id(2)
    qf = q_ref[...] * SCALE
    kvc = kvc_ref[...].astype(BF16)
    kvp = kvp_ref[...].astype(BF16) if prev_block else None
    chains = [(rr, tt, h) for rr in range(nr) for tt in range(tiles) for h in range(GROUP_HEADS)]
    n = len(chains)
    rows = lambda tt: slice(tt * Q_TILE, (tt + 1) * Q_TILE)
    lo_lane = lax.broadcasted_iota(jnp.int32, (Q_TILE, 128), 1) < HEAD_DIM

    def keys(rr, tt, h, off):
        lanes = slice(rr * 2 * GROUP_W + off + (h // 2) * 128, rr * 2 * GROUP_W + off + (h // 2 + 1) * 128)
        return kvp[:, lanes] if tt < 0 else kvc[rows(tt), lanes]

    has_prev = [tt > 0 or prev_block for _, tt, _ in chains]
    qh = [jnp.where(lo_lane if h % 2 == 0 else jnp.logical_not(lo_lane),
                    qf[rows(tt), rr * GROUP_W + (h // 2) * 128:rr * GROUP_W + (h // 2 + 1) * 128], 0.0).astype(BF16)
          for rr, tt, h in chains]
    lc = [_dot_nt(qh[c], keys(rr, tt, h, 0)) + bias_ref[h, :, :Q_TILE] for c, (rr, tt, h) in enumerate(chains)]
    lp = [None] * n
    for c, (rr, tt, h) in enumerate(chains):
        if has_prev[c]:
            z = _dot_nt(qh[c], keys(rr, tt - 1, h, 0)) + bias_ref[h, :, Q_TILE:]
            lp[c] = jnp.where(i > 0, z, NEG_INF) if tt == 0 else z
    m = [jnp.max(z, axis=-1, keepdims=True) for z in lc]
    m = [jnp.maximum(m[c], jnp.max(lp[c], axis=-1, keepdims=True)) if has_prev[c] else m[c] for c in range(n)]
    pc = [jnp.exp(lc[c] - m[c]) for c in range(n)]
    pp = [jnp.exp(lp[c] - m[c]) if has_prev[c] else None for c in range(n)]
    den = [jnp.sum(z, axis=-1, keepdims=True) for z in pc]
    den = [den[c] + jnp.sum(pp[c], axis=-1, keepdims=True) if has_prev[c] else den[c] for c in range(n)]
    o = [_dot(pc[c].astype(BF16), keys(rr, tt, h, GROUP_W)) for c, (rr, tt, h) in enumerate(chains)]
    o = [o[c] + _dot(pp[c].astype(BF16), keys(rr, tt - 1, h, GROUP_W)) if has_prev[c] else o[c]
         for c, (rr, tt, h) in enumerate(chains)]
    lane = lax.broadcasted_iota(jnp.int32, (Q_TILE, 128), 1)
    st = None
    for c, (rr, tt, h) in enumerate(chains):
        if h % 2 == 1:
            pair = slice(rr * GROUP_W + (h // 2) * 128, rr * GROUP_W + (h // 2 + 1) * 128)
            acc_ref[rows(tt), pair] = jnp.where(lo_lane, o[c - 1], o[c])
        st = jnp.zeros((Q_TILE, 128), F32) if h == 0 else st
        st = jnp.where(lane == h, m[c], st)
        st = jnp.where(lane == GROUP_HEADS + h, den[c], st)
        if h == GROUP_HEADS - 1:
            st_ref[rows(tt), rr * 128:(rr + 1) * 128] = st


def _dil_attn(qv, kvv, bias_tiles, g):
    dil = DILATIONS[g]
    b, ln, _ = qv.shape
    tiles = min(ln // Q_TILE, DIL_SEGMENTS)
    nr = min(dil, DIL_SEGMENTS // tiles)
    nblk = ln // (tiles * Q_TILE)
    return pl.pallas_call(
        functools.partial(_dil_attn_kernel, nr=nr, tiles=tiles, prev_block=nblk > 1),
        grid=(b, dil // nr, nblk),
        in_specs=[pl.BlockSpec((None, tiles * Q_TILE, nr * GROUP_W), lambda bi, r, i: (bi, i, r)),
                  pl.BlockSpec((None, tiles * Q_TILE, nr * 2 * GROUP_W), lambda bi, r, i: (bi, i, r)),
                  pl.BlockSpec((None, Q_TILE, nr * 2 * GROUP_W),
                               lambda bi, r, i: (bi, jnp.maximum(tiles * i - 1, 0), r)),
                  pl.BlockSpec((GROUP_HEADS, Q_TILE, 2 * Q_TILE), lambda bi, r, i: (g, 0, 0))],
        out_specs=[pl.BlockSpec((None, tiles * Q_TILE, nr * GROUP_W), lambda bi, r, i: (bi, i, r)),
                   pl.BlockSpec((None, tiles * Q_TILE, nr * 128), lambda bi, r, i: (bi, i, r))],
        out_shape=[jax.ShapeDtypeStruct((b, ln, dil * GROUP_W), F32),
                   jax.ShapeDtypeStruct((b, ln, dil * 128), F32)],
        compiler_params=_params("parallel", "parallel", "arbitrary"),
        name=f"dil_attn_d{dil}",
    )(qv, kvv, kvv, bias_tiles)


def _sample_attn_kernel(q_ref, kn0_ref, kn1_ref, kn2_ref, c0_ref, c1_ref, c2_ref, b0_ref, b1_ref, b2_ref,
                        bnew_ref, o_ref, *, n_new):
    q = (q_ref[...] * SCALE).astype(BF16)
    qf = q.astype(F32)
    groups = ((kn0_ref, c0_ref, b0_ref), (kn1_ref, c1_ref, b1_ref), (kn2_ref, c2_ref, b2_ref))
    new_rows = []
    for g, (kn_ref, _, _) in enumerate(groups):
        kn = kn_ref[...]
        shifted = [kn] + [pltpu.roll(kn, s, axis=0) for s in range(1, n_new if g == 0 else 1)]
        new_rows.append([z.astype(BF16).astype(F32) for z in shifted])
    chains = [(h, g) for h in range(GROUP_HEADS) for g in range(N_GROUPS)]
    ids = range(len(chains))
    ks = lambda h: slice(h * HEAD_DIM, (h + 1) * HEAD_DIM)
    vs = lambda h: slice(GROUP_W + h * HEAD_DIM, GROUP_W + (h + 1) * HEAD_DIM)
    qs = lambda h, g: slice(g * GROUP_W + h * HEAD_DIM, g * GROUP_W + (h + 1) * HEAD_DIM)
    lp = [_dot(q[:, qs(h, g)], groups[g][1][0, h].astype(BF16)) + groups[g][2][h] for h, g in chains]
    lns = [[jnp.sum(qf[:, qs(h, g)] * rows[:, ks(h)], axis=-1, keepdims=True) + bnew_ref[g, s][:, h:h + 1]
            for s, rows in enumerate(new_rows[g])] for h, g in chains]
    m = [functools.reduce(jnp.maximum, [jnp.max(lp[c], axis=-1, keepdims=True)] + lns[c]) for c in ids]
    pp = [jnp.exp(lp[c] - m[c]) for c in ids]
    pn = [[jnp.exp(ln - m[c]) for ln in lns[c]] for c in ids]
    den = [jnp.sum(pp[c], axis=-1, keepdims=True) + sum(pn[c]) for c in ids]
    o = [_dot_nt(pp[c].astype(BF16), groups[g][1][1, h].astype(BF16)) for c, (h, g) in enumerate(chains)]
    o = [o[c] + sum(z.astype(BF16).astype(F32) * rows[:, vs(h)] for z, rows in zip(pn[c], new_rows[g]))
         for c, (h, g) in enumerate(chains)]
    for h in range(GROUP_HEADS):
        cs = [h * N_GROUPS + g for g in range(N_GROUPS)]
        mx = functools.reduce(jnp.maximum, [m[c] for c in cs])
        wts = [jnp.exp(m[c] - mx) * den[c] for c in cs]
        num = sum(w * (o[c] / den[c]) for w, c in zip(wts, cs))
        o_ref[:, ks(h)] = num / sum(wts)


def _sample_attn(q, kvn, caches, bias_past, bias_new, n_new):
    b, tp, _ = q.shape
    new_spec = pl.BlockSpec((None, tp, 2 * GROUP_W), lambda bi: (bi, 0, 0))
    cache_spec = lambda w: pl.BlockSpec((None, 2, GROUP_HEADS, HEAD_DIM, w), lambda bi: (bi, 0, 0, 0, 0))
    bias_spec = lambda w: pl.BlockSpec((GROUP_HEADS, tp, w), lambda bi: (0, 0, 0))
    return pl.pallas_call(
        functools.partial(_sample_attn_kernel, n_new=n_new),
        grid=(b,),
        in_specs=[pl.BlockSpec((None, tp, A_Q), lambda bi: (bi, 0, 0)), new_spec, new_spec, new_spec]
                 + [cache_spec(w) for w in WINDOWS] + [bias_spec(w) for w in WINDOWS]
                 + [pl.BlockSpec((N_GROUPS, n_new, tp, 128), lambda bi: (0, 0, 0, 0))],
        out_specs=pl.BlockSpec((None, tp, GROUP_W), lambda bi: (bi, 0, 0)),
        out_shape=jax.ShapeDtypeStruct((b, tp, GROUP_W), F32),
        compiler_params=_params("parallel"),
        name="sample_attn",
    )(q, kvn[0], kvn[1], kvn[2], *caches, *bias_past, bias_new)


def _cross_attn_kernel(q_ref, kvt_ref, o_ref, *, sub):
    nb, tq, _ = q_ref.shape
    chains = [(bi, slice(r0, r0 + sub), h) for bi in range(nb) for r0 in range(0, tq, sub)
              for h in range(MEM_W // HEAD_DIM)]
    qf = [q_ref[bi] * SCALE for bi in range(nb)]
    kvt = [kvt_ref[bi].astype(BF16) for bi in range(nb)]
    pair = lambda h, off=0: slice(off + (h // 2) * 128, off + (h // 2 + 1) * 128)
    lo_lane = lax.broadcasted_iota(jnp.int32, (sub, 128), 1) < HEAD_DIM
    qm = [jnp.where(lo_lane if h % 2 == 0 else jnp.logical_not(lo_lane), qf[bi][rw, pair(h)], 0.0).astype(BF16)
          for bi, rw, h in chains]
    lg = [_dot(qm[c], kvt[bi][pair(h), :]) for c, (bi, rw, h) in enumerate(chains)]
    p = [jnp.exp(z - jnp.max(z, axis=-1, keepdims=True)) for z in lg]
    den = [jnp.sum(z, axis=-1, keepdims=True) for z in p]
    o = [_dot_nt(p[c].astype(BF16), kvt[bi][pair(h, MEM_W), :]) / den[c] for c, (bi, rw, h) in enumerate(chains)]
    for c, (bi, rw, h) in enumerate(chains):
        if h % 2 == 1:
            o_ref[bi, rw, pair(h)] = jnp.where(lo_lane, o[c - 1], o[c])


def _cross_attn(q, kvt, layer, nb, tq):
    b, t, _ = q.shape
    return pl.pallas_call(
        functools.partial(_cross_attn_kernel, sub=min(tq, Q_TILE)),
        grid=(b // nb, t // tq),
        in_specs=[pl.BlockSpec((nb, tq, MEM_W), lambda bi, i: (bi, i, 0)),
                  pl.BlockSpec((None, nb, 2 * MEM_W, N_MEM), lambda bi, i: (layer, bi, 0, 0))],
        out_specs=pl.BlockSpec((nb, tq, MEM_W), lambda bi, i: (bi, i, 0)),
        out_shape=jax.ShapeDtypeStruct((b, t, MEM_W), F32),
        compiler_params=_params("parallel", "parallel"),
        name="cross_attn",
    )(q, kvt)


def _rwkv_kernel(cols_ref, shift_ref, s0_ref, mu_ref, w0_ref, wup_ref, a0_ref, aup_ref, kk_ref, ka_ref, rk_ref,
                 lnw_ref, lnb_ref, y_ref, sout_ref, shout_ref,
                 carry, state, r_s, k_s, v_s, a_s, b_s, lw_s, *, tt, chunk, t_valid):
    j = pl.program_id(1)

    @pl.when(j == 0)
    def _():
        carry[...] = shift_ref[...]
        state[...] = s0_ref[...]

    cols = cols_ref[...]
    row = lax.broadcasted_iota(jnp.int32, (tt, 1), 0)
    prev = jnp.where(row == 0, carry[...], pltpu.roll(cols, 1, axis=0))
    carry[...] = cols[t_valid - 1:t_valid, :]
    xs = cols + mu_ref[...] * (prev - cols)
    r = xs[:, :RWKV_W]
    k = xs[:, RWKV_W:2 * RWKV_W]
    v = xs[:, 2 * RWKV_W:3 * RWKV_W]
    wd = xs[:, 3 * RWKV_W:3 * RWKV_W + LORA]
    ad = xs[:, 3 * RWKV_W + LORA:]
    z = w0_ref[...] + _dot(jnp.tanh(wd).astype(BF16), wup_ref[...])
    log_decay = -math.exp(-0.5) * jax.nn.sigmoid(z)
    a = jax.nn.sigmoid(a0_ref[...] + _dot(ad.astype(BF16), aup_ref[...]))

    blockdiag = (lax.broadcasted_iota(jnp.int32, (128, 128), 0) // HEAD_DIM
                 == lax.broadcasted_iota(jnp.int32, (128, 128), 1) // HEAD_DIM)

    def head_sum(x):
        lo = lax.broadcasted_iota(jnp.int32, x.shape, 1) < HEAD_DIM
        s0 = jnp.sum(jnp.where(lo, x, 0.0), axis=-1, keepdims=True)
        s1 = jnp.sum(jnp.where(lo, 0.0, x), axis=-1, keepdims=True)
        return jnp.where(lo, s0, s1)

    kk = k * kk_ref[...]
    kk_sq = kk * kk
    k2 = k * (1.0 + (a - 1.0) * ka_ref[...])
    live = row < t_valid
    for p in range(RWKV_PAIRS):
        ps = slice(p * 128, (p + 1) * 128)
        nrm = jnp.maximum(jnp.sqrt(head_sum(kk_sq[:, ps])), 1e-12)
        kkn = kk[:, ps] / nrm
        if t_valid < tt:
            zero = jnp.zeros((tt, 128), F32)
            r_s[:, ps] = r[:, ps]
            k_s[:, ps] = jnp.where(live, k2[:, ps], zero)
            v_s[:, ps] = jnp.where(live, v[:, ps], zero)
            a_s[:, ps] = jnp.where(live, -kkn, zero)
            b_s[:, ps] = jnp.where(live, kkn * a[:, ps], zero)
            lw_s[:, ps] = jnp.where(live, log_decay[:, ps], zero)
        else:
            r_s[:, ps] = r[:, ps]
            k_s[:, ps] = k2[:, ps]
            v_s[:, ps] = v[:, ps]
            a_s[:, ps] = -kkn
            b_s[:, ps] = kkn * a[:, ps]
            lw_s[:, ps] = log_decay[:, ps]

    ci = lax.broadcasted_iota(jnp.int32, (chunk, chunk), 0)
    cj = lax.broadcasted_iota(jnp.int32, (chunk, chunk), 1)
    tri_incl = (ci >= cj).astype(BF16)
    lo_lane = lax.broadcasted_iota(jnp.int32, (chunk, 128), 1) < HEAD_DIM
    levels = max(1, math.ceil(math.log2(min(chunk, t_valid))))
    n2 = 2 * chunk
    ri = lax.broadcasted_iota(jnp.int32, (n2, n2), 0)
    rj = lax.broadcasted_iota(jnp.int32, (n2, n2), 1)
    same = ri // chunk == rj // chunk
    strict = jnp.logical_and(same, ri > rj)
    eye = (ri == rj).astype(F32)
    incl = jnp.logical_and(same, ri >= rj)
    own = lax.broadcasted_iota(jnp.int32, (n2, 128), 0) // chunk == lax.broadcasted_iota(jnp.int32, (n2, 128), 1) // HEAD_DIM
    dup = lambda z: jnp.concatenate([z, z], axis=0)
    cat = jnp.concatenate

    n_chunks = tt // chunk
    group = SCAN_INTERLEAVE if n_chunks % SCAN_INTERLEAVE == 0 else 1

    def chunk_body(ci, _):
        rows = [pl.ds((ci * group + cc) * chunk, chunk) for cc in range(group)]
        sl = [slice(p * 128, (p + 1) * 128) for p in range(RWKV_PAIRS)]
        chains = [(rows[cc], sl[p]) for cc in range(group) for p in range(RWKV_PAIRS)]
        ids = range(len(chains))
        cum_all = [_mm(tri_incl, lw_s[rw, :], pb=3) for rw in rows]
        cum = [cum_all[cc][:, s] for cc in range(group) for s in sl]
        p_incl = [jnp.exp(z) for z in cum]
        p_inv = [jnp.exp(-z) for z in cum]
        rr = [r_s[rw, s] for rw, s in chains]
        kc = [k_s[rw, s] for rw, s in chains]
        vc = [v_s[rw, s] for rw, s in chains]
        at2 = [jnp.where(own, dup(a_s[rw, s] * jnp.exp(cum[c] - lw_s[rw, s])), 0.0) for c, (rw, s) in enumerate(chains)]
        rt2 = [jnp.where(own, dup(rr[c] * p_incl[c]), 0.0) for c in ids]
        bt = [(b_s[rw, s] * p_inv[c]).astype(BF16) for c, (rw, s) in enumerate(chains)]
        kt = [(kc[c] * p_inv[c]).astype(BF16) for c in ids]
        vb = [z.astype(BF16) for z in vc]
        v2 = [dup(z) for z in vb]
        g = [_mm(cat([at2[c], rt2[c]], 0).astype(BF16), cat([dup(bt[c]), dup(kt[c])], 0), NT) for c in ids]
        a_ak = [jnp.where(strict, z[:n2, n2:], 0.0).astype(BF16) for z in g]
        apow = [jnp.where(strict, z[:n2, :n2], 0.0).astype(BF16) for z in g]
        a_r = [cat([jnp.where(incl, z[n2:, :n2], 0.0), jnp.where(incl, z[n2:, n2:], 0.0)], 1).astype(BF16)
               for z in g]
        akv = [_mm(a_ak[c], v2[c]) for c in ids]
        tinv = [eye + jnp.where(strict, z[:n2, :n2], 0.0) for z in g]
        for _ in range(levels - 1):
            apow = [_mm(z, z).astype(BF16) for z in apow]
            tinv = [tinv[c] + _mm(tinv[c].astype(BF16), apow[c]) for c in ids]
        sol = [_mm(tinv[c].astype(BF16), cat([at2[c], akv[c]], axis=1).astype(BF16)) for c in ids]
        ws = [z[:, :128] for z in sol]
        u0s = [jnp.where(own, z[:, 128:], 0.0) for z in sol]
        zeros2 = jnp.zeros((n2, 128), BF16)
        qy = [_mm(a_r[c], cat([cat([ws[c], u0s[c]], 1).astype(BF16), cat([zeros2, v2[c]], 1)], 0)) for c in ids]
        qs = [(rt2[c] + qy[c][:, :128]).astype(BF16) for c in ids]
        zeros1 = jnp.zeros((chunk, 128), BF16)
        lhs = [cat([cat([ws[c][:chunk] + ws[c][chunk:], u0s[c][:chunk] + u0s[c][chunk:]], 1).astype(BF16),
                    cat([zeros1, vb[c]], 1)], 0) for c in ids]
        wbn = [_mm(lhs[c], cat([bt[c], kt[c]], 0), TN) for c in ids]
        wb = [z[:128].astype(BF16) for z in wbn]
        s_cur = [state[p] for p in range(RWKV_PAIRS)]
        ys = []
        for cc in range(group):
            base = cc * RWKV_PAIRS
            s_b = [z.astype(BF16) for z in s_cur]
            ys += [_mm(qs[base + p], s_b[p], NT) + qy[base + p][:, 128:] for p in range(RWKV_PAIRS)]
            sw = [_mm(s_b[p], wb[base + p]) for p in range(RWKV_PAIRS)]
            s_cur = [jnp.where(blockdiag, (s_cur[p] + sw[p] + wbn[base + p][128:])
                               * p_incl[base + p][chunk - 1:chunk, :], 0.0) for p in range(RWKV_PAIRS)]
        for p in range(RWKV_PAIRS):
            state[p] = s_cur[p]
        y = [jnp.where(lo_lane, z[:chunk], z[chunk:]) for z in ys]

        bonus = [head_sum(rr[c] * kc[c] * rk_ref[:, s]) for c, (_, s) in enumerate(chains)]
        dlt = [y[c] - head_sum(y[c]) * (1.0 / HEAD_DIM) for c in ids]
        var = [head_sum(z * z) * (1.0 / HEAD_DIM) for z in dlt]
        for c, (rw, s) in enumerate(chains):
            yn = dlt[c] * lax.rsqrt(var[c] + GN_EPS) * lnw_ref[:, s] + lnb_ref[:, s]
            y_ref[rw, s] = yn + bonus[c] * vc[c]
        return 0

    for ci in range(n_chunks // group):
        chunk_body(ci, 0)

    @pl.when(j == pl.num_programs(1) - 1)
    def _():
        sout_ref[...] = state[...]
        shout_ref[...] = carry[...]


def _rwkv_scan(cols, shift_prev, s0, prm, tt, t_valid):
    b, t, _ = cols.shape
    s = s0.reshape(b, RWKV_PAIRS, 2, HEAD_DIM, HEAD_DIM)
    z = jnp.zeros_like(s[:, :, 0])
    s_bd = jnp.concatenate([jnp.concatenate([s[:, :, 0], z], -1), jnp.concatenate([z, s[:, :, 1]], -1)], -2)
    row = lambda n: pl.BlockSpec((1, n), lambda bi, j: (0, 0))
    vec = lambda a: a.reshape(1, -1)
    y, s_out, sh_out = pl.pallas_call(
        functools.partial(_rwkv_kernel, tt=tt, chunk=SCAN_CHUNK, t_valid=min(t_valid, tt)),
        grid=(b, t // tt),
        in_specs=[pl.BlockSpec((None, tt, C_SHIFT), lambda bi, j: (bi, j, 0)),
                  pl.BlockSpec((None, 1, C_SHIFT), lambda bi, j: (bi, 0, 0)),
                  pl.BlockSpec((None, RWKV_PAIRS, 128, 128), lambda bi, j: (bi, 0, 0, 0)),
                  row(C_SHIFT), row(RWKV_W),
                  pl.BlockSpec((LORA, RWKV_W), lambda bi, j: (0, 0)),
                  row(RWKV_W),
                  pl.BlockSpec((LORA, RWKV_W), lambda bi, j: (0, 0)),
                  row(RWKV_W), row(RWKV_W), row(RWKV_W), row(RWKV_W), row(RWKV_W)],
        out_specs=[pl.BlockSpec((None, tt, RWKV_W), lambda bi, j: (bi, j, 0)),
                   pl.BlockSpec((None, RWKV_PAIRS, 128, 128), lambda bi, j: (bi, 0, 0, 0)),
                   pl.BlockSpec((None, 1, C_SHIFT), lambda bi, j: (bi, 0, 0))],
        out_shape=[jax.ShapeDtypeStruct((b, t, RWKV_W), F32),
                   jax.ShapeDtypeStruct((b, RWKV_PAIRS, 128, 128), F32),
                   jax.ShapeDtypeStruct((b, 1, C_SHIFT), F32)],
        scratch_shapes=[pltpu.VMEM((1, C_SHIFT), F32), pltpu.VMEM((RWKV_PAIRS, 128, 128), F32)]
                       + [pltpu.VMEM((tt, RWKV_W), F32)] * 6,
        compiler_params=_params("parallel", "arbitrary"),
        name="rwkv_scan",
    )(cols, shift_prev.reshape(b, 1, C_SHIFT), s_bd, vec(prm["mu"]), vec(prm["w0"]), prm["w_up"].astype(BF16),
      vec(prm["a0"]), prm["a_up"].astype(BF16), vec(prm["k_k"]), vec(prm["k_a"]), vec(prm["r_k"]),
      vec(prm["ln_w"]), vec(prm["ln_b"]))
    s_fin = jnp.stack([s_out[:, :, :HEAD_DIM, :HEAD_DIM], s_out[:, :, HEAD_DIM:, HEAD_DIM:]], axis=2)
    return y, s_fin.reshape(b, 2 * RWKV_PAIRS, HEAD_DIM, HEAD_DIM), sh_out.reshape(b, C_SHIFT)


def _post_kernel(*refs, merge_dils, mix_w):
    x_ref, omem_ref, gate_ref, w_ref, g_ref = refs[:5]
    mix_refs = refs[5:-2]
    y_ref, stage = refs[-2:]
    if merge_dils:
        er = lax.broadcasted_iota(jnp.int32, (128, 2 * GROUP_W), 0)
        ec = lax.broadcasted_iota(jnp.int32, (128, 2 * GROUP_W), 1)
        expand = (er == ec // HEAD_DIM).astype(BF16)
        ms, ss, os_ = [], [], []
        for g, dil in enumerate(merge_dils):
            acc_ref, st_ref = mix_refs[2 * g], mix_refs[2 * g + 1]
            acc = acc_ref[...] if dil == 1 else _from_classes(acc_ref, stage, dil, GROUP_W)
            stats = st_ref[...] if dil == 1 else _from_classes(st_ref, stage, dil, 128)
            st = _mm(stats, expand, pa=3)
            ms.append(st[:, :GROUP_W])
            ss.append(st[:, GROUP_W:])
            os_.append(acc / st[:, GROUP_W:])
        mx = functools.reduce(jnp.maximum, ms)
        wts = [jnp.exp(m - mx) * s for m, s in zip(ms, ss)]
        mix = sum(w * o for w, o in zip(wts, os_)) / sum(wts)
    else:
        mix = mix_refs[0][...]
    gate = gate_ref[...]
    act = gate * jax.nn.sigmoid(gate)
    h1 = (mix * act[:, :mix_w]).astype(BF16)
    h2 = (omem_ref[...] * act[:, mix_w:]).astype(BF16)
    out = _dot(h1, w_ref[:mix_w, :]) + _dot(h2, w_ref[mix_w:, :])
    y_ref[...] = x_ref[...] + _rms(out, g_ref[...])


def _post(x, mix, o_mem, gate, w_out, g_post, tm):
    m, d = x.shape
    merge = isinstance(mix, (list, tuple))
    gw = gate.shape[1]
    rows = lambda wd: pl.BlockSpec((tm, wd), lambda i: (i, 0))
    if merge:
        mix_w = GROUP_W
        merge_dils = tuple(acc.shape[2] // GROUP_W for acc, _ in mix)
        tiles = mix[0][0].shape[1] * merge_dils[0] // tm
        mix_arrays = [a for pair in mix for a in pair]
        mix_specs = [pl.BlockSpec((None, tm // dil, a.shape[2]), lambda i: (i // tiles, i % tiles, 0))
                     for dil, pair in zip(merge_dils, mix) for a in pair]
    else:
        mix_w, merge_dils, mix_arrays, mix_specs = mix.shape[1], (), [mix], [rows(mix.shape[1])]
    return pl.pallas_call(
        functools.partial(_post_kernel, merge_dils=merge_dils, mix_w=mix_w),
        grid=(m // tm,),
        in_specs=[rows(d), rows(MEM_W), rows(gw),
                  pl.BlockSpec((gw, d), lambda i: (0, 0)),
                  pl.BlockSpec((1, d), lambda i: (0, 0))] + mix_specs,
        out_specs=rows(d),
        out_shape=jax.ShapeDtypeStruct((m, d), F32),
        scratch_shapes=[pltpu.VMEM((tm, 128), F32)],
        compiler_params=_params("parallel"),
        name="post",
    )(x, o_mem, gate, w_out.astype(BF16), g_post.reshape(1, d), *mix_arrays)


def _t5_bucket(dist):
    max_exact = N_BUCKETS // 2
    d = jnp.maximum(dist, 1).astype(F32)
    large = max_exact + (jnp.log(d / max_exact) / math.log(MAX_DISTANCE / max_exact)
                         * (N_BUCKETS - max_exact)).astype(jnp.int32)
    large = jnp.minimum(large, N_BUCKETS - 1)
    return jnp.where(dist < max_exact, dist, large)


def _group_bias(rel_bias, g):
    dist = DILATIONS[g] * jnp.arange(A_KEYS, dtype=jnp.int32)
    bias = rel_bias[_t5_bucket(dist)]
    return bias[:, g * GROUP_HEADS:(g + 1) * GROUP_HEADS].T.astype(F32)


def _bias_tiles_kernel(x_ref, o_ref):
    row = lax.broadcasted_iota(jnp.int32, (Q_TILE, 2 * Q_TILE), 0)
    for h in range(o_ref.shape[0]):
        t = jnp.broadcast_to(x_ref[h:h + 1, :], (Q_TILE, 2 * Q_TILE))
        for bit in range(int(math.log2(Q_TILE))):
            t = jnp.where((row >> bit) & 1 == 1, pltpu.roll(t, 1 << bit, axis=1), t)
        o_ref[h] = t


def _prompt_bias(biases):
    bias = jnp.concatenate(biases, axis=0)
    neg = jnp.full((bias.shape[0], Q_TILE - 1), NEG_INF, F32)
    table = jnp.concatenate([bias[:, :1], neg, bias[:, :0:-1]], axis=1)
    n = table.shape[0]
    return pl.pallas_call(
        _bias_tiles_kernel,
        out_shape=jax.ShapeDtypeStruct((n, Q_TILE, 2 * Q_TILE), F32),
        name="bias_tiles",
    )(table)


def _sample_bias(biases, n_new, tp):
    t = np.arange(tp)[:, None]
    real = t < n_new
    past = []
    for g, bias in enumerate(biases):
        w, dil = WINDOWS[g], DILATIONS[g]
        spread = jnp.concatenate([bias[:, :0:-1, None], jnp.full((GROUP_HEADS, w // dil, dil - 1), NEG_INF, F32)],
                                 axis=-1).reshape(GROUP_HEADS, w)
        rows = [jnp.concatenate([jnp.full((GROUP_HEADS, tt), NEG_INF, F32), spread[:, :w - tt]], axis=-1)
                for tt in range(n_new)] + [jnp.zeros((GROUP_HEADS, w), F32)] * (tp - n_new)
        past.append(jnp.stack(rows, axis=1))
    s = np.arange(n_new)[None, :, None]
    g_idx = np.arange(N_GROUPS)[:, None, None]
    ok = (s <= t[None, :, 0]) & real[None, :, 0] & ((g_idx == 0) | (s == 0))
    vals = jnp.stack([bias[:, :n_new].T for bias in biases])
    new = jnp.where(jnp.asarray(ok)[..., None], vals[:, :, None, :], NEG_INF)
    return past, jnp.pad(new, ((0, 0), (0, 0), (0, 0), (0, 128 - GROUP_HEADS)))


def _pieces_a():
    q = [((g * GROUP_W, (g + 1) * GROUP_W),) for g in range(N_GROUPS)]
    kv = [((A_Q + g * GROUP_W, A_Q + (g + 1) * GROUP_W), (2 * A_Q + g * GROUP_W, 2 * A_Q + (g + 1) * GROUP_W))
          for g in range(N_GROUPS)]
    qmem = ((3 * A_Q, 3 * A_Q + MEM_W),)
    gate = ((3 * A_Q + MEM_W, 3 * A_Q + MEM_W + GROUP_W + MEM_W),)
    return tuple(q + kv + [qmem, gate])


def _pieces_b():
    return (((0, C_SHIFT),), ((C_SHIFT, C_SHIFT + MEM_W),), ((C_SHIFT + MEM_W, C_SHIFT + MEM_W + RWKV_W + MEM_W),))


def kernel(x_prompt, x_sample, mem_prompt, cache_mem_kv, cache_win0, cache_win1, cache_win2, state_wkv, state_shift, norm_pre, norm_post, norm_mem, w_mem_kv, rel_bias, w_in_a, w_out_a, w_in_b, w_out_b, rwkv_mu, rwkv_w0, rwkv_w_up, rwkv_a0, rwkv_a_up, rwkv_k_k, rwkv_k_a, rwkv_r_k, rwkv_ln_w, rwkv_ln_b):
    bp, tp, d = x_prompt.shape
    bs, ts, _ = x_sample.shape
    tsp = SAMPLE_PAD_T
    xp = x_prompt.reshape(bp * tp, d)
    xs = jnp.pad(x_sample, ((0, 0), (0, tsp - ts), (0, 0))).reshape(bs * tsp, d)
    tm_p, tm_s = 512, bs * tsp
    time_minor = lambda c: jnp.moveaxis(c, -4, -1)
    caches = [time_minor(c[0]) for c in (cache_win0, cache_win1, cache_win2)]
    mkv_s = time_minor(cache_mem_kv).reshape(cache_mem_kv.shape[0], bs, 2 * MEM_W, N_MEM)
    mkv_p = _mem_kv(mem_prompt, norm_mem, w_mem_kv)

    biases = [_group_bias(rel_bias, g) for g in range(N_GROUPS)]
    outs_p = _norm_proj(xp, norm_pre[0], w_in_a[0], _pieces_a(), tm_p, transposed=(3, 4, 5),
                        dils=DILATIONS + DILATIONS + (1, 1), rows_per_batch=tp)
    outs_s = _norm_proj(xs, norm_pre[0], w_in_a[0], _pieces_a(), tm_s)
    q_p, kv_p, qmem_p, gate_p, kvt_p = outs_p[0:3], outs_p[3:6], outs_p[6], outs_p[7], outs_p[8:11]
    q_s, kv_s, qmem_s, gate_s = outs_s[0:3], outs_s[3:6], outs_s[6], outs_s[7]

    bias_tiles = _prompt_bias(biases)
    merged = [_dil_attn(q_p[g].reshape(bp, tp // DILATIONS[g], -1), kv_p[g].reshape(bp, tp // DILATIONS[g], -1),
                        bias_tiles, g) for g in range(N_GROUPS)]
    omem_p = _cross_attn(qmem_p.reshape(bp, tp, MEM_W), mkv_p, 0, 1, 512)
    xp1 = _post(xp, merged, omem_p.reshape(bp * tp, MEM_W), gate_p, w_out_a[0], norm_post[0], tm_p)

    bias_past, bias_new = _sample_bias(biases, ts, tsp)
    q_s_all = jnp.concatenate(q_s, axis=-1).reshape(bs, tsp, A_Q)
    kvn = [a.reshape(bs, tsp, 2 * GROUP_W) for a in kv_s]
    o_s = _sample_attn(q_s_all, kvn, caches, bias_past, bias_new, ts)
    omem_s = _cross_attn(qmem_s.reshape(bs, tsp, MEM_W), mkv_s, 0, 4, tsp)
    xs1 = _post(xs, o_s.reshape(bs * tsp, GROUP_W), omem_s.reshape(bs * tsp, MEM_W), gate_s, w_out_a[0],
                norm_post[0], tm_s)

    cols_p, qmem_p, gate_p = _norm_proj(xp1, norm_pre[1], w_in_b[0], _pieces_b(), tm_p)
    cols_s, qmem_s, gate_s = _norm_proj(xs1, norm_pre[1], w_in_b[0], _pieces_b(), tm_s)
    prm = dict(mu=rwkv_mu[0], w0=rwkv_w0[0], w_up=rwkv_w_up[0], a0=rwkv_a0[0], a_up=rwkv_a_up[0], k_k=rwkv_k_k[0],
               k_a=rwkv_k_a[0], r_k=rwkv_r_k[0], ln_w=rwkv_ln_w[0], ln_b=rwkv_ln_b[0])
    y_p, wkv_p, sh_p = _rwkv_scan(cols_p.reshape(bp, tp, C_SHIFT), jnp.zeros((bp, C_SHIFT), F32),
                                  jnp.zeros((bp, 2 * RWKV_PAIRS, HEAD_DIM, HEAD_DIM), F32), prm, 256, tp)
    cols_s_pad = jnp.pad(cols_s.reshape(bs, tsp, C_SHIFT), ((0, 0), (0, SCAN_CHUNK - tsp), (0, 0)))
    y_s, wkv_s, sh_s = _rwkv_scan(cols_s_pad, state_shift[0], state_wkv[0], prm, SCAN_CHUNK, ts)
    y_s = y_s[:, :tsp].reshape(bs * tsp, RWKV_W)
    omem_p = _cross_attn(qmem_p.reshape(bp, tp, MEM_W), mkv_p, 1, 1, 512)
    omem_s = _cross_attn(qmem_s.reshape(bs, tsp, MEM_W), mkv_s, 1, 4, tsp)
    xp2 = _post(xp1, y_p.reshape(bp * tp, RWKV_W), omem_p.reshape(bp * tp, MEM_W), gate_p, w_out_b[0],
                norm_post[1], tm_p)
    xs2 = _post(xs1, y_s, omem_s.reshape(bs * tsp, MEM_W), gate_s, w_out_b[0], norm_post[1], tm_s)

    kv_shape = (2, GROUP_HEADS, HEAD_DIM)
    time_major = lambda c: jnp.moveaxis(c, -1, -4)
    new_mem_kv = time_major(mkv_p.reshape(mkv_p.shape[0], bp, 2, MEM_W // HEAD_DIM, HEAD_DIM, N_MEM))
    win_p = [time_major(kvt_p[g].reshape(bp, *kv_shape, tp)[..., tp - min(WINDOWS[g], tp):])[None]
             for g in range(N_GROUPS)]
    win_s = [kv_s[g].reshape(bs, tsp, *kv_shape)[None, :, :ts] for g in range(N_GROUPS)]
    return (xp2.reshape(bp, tp, d), xs2.reshape(bs, tsp, d)[:, :ts], new_mem_kv,
            win_p[0], win_p[1], win_p[2], win_s[0], win_s[1], win_s[2],
            wkv_p[None], wkv_s[None], sh_p[None], sh_s[None])
```

```python
import functools
import math

import numpy as np
import jax
import jax.numpy as jnp
from jax import lax
from jax.experimental import pallas as pl
from jax.experimental.pallas import tpu as pltpu

F32 = jnp.float32
BF16 = jnp.bfloat16

D_MODEL = 1024
HEAD_DIM = 64
N_GROUPS = 3
GROUP_HEADS = 4
WINDOWS = (128, 512, 2048)
DILATIONS = (1, 4, 16)
A_HEADS = N_GROUPS * GROUP_HEADS
A_KEYS = 129
GROUP_W = GROUP_HEADS * HEAD_DIM
A_Q = A_HEADS * HEAD_DIM
N_MEM = 256
MEM_W = 256
RWKV_W = 768
RWKV_PAIRS = RWKV_W // 128
LORA = 64
C_SHIFT = 3 * RWKV_W + 2 * LORA
N_BUCKETS = 32
MAX_DISTANCE = WINDOWS[-1]
RMS_EPS = 1e-6
GN_EPS = 64e-5
NEG_INF = -1e30
SCALE = HEAD_DIM ** -0.5

Q_TILE = 128
DIL_SEGMENTS = 4
SCAN_CHUNK = 64
SCAN_INTERLEAVE = 2
SAMPLE_PAD_T = 8
VMEM_LIMIT = 56 * 1024 * 1024
NN =(((1,), (0,)), ((), ()))
NT = (((1,), (1,)), ((), ()))
TN = (((0,), (0,)), ((), ()))


def _params(*sem):
    return pltpu.CompilerParams(dimension_semantics=sem, vmem_limit_bytes=VMEM_LIMIT)


def _dot(a, b):
    return jnp.dot(a, b, preferred_element_type=F32)


def _dot_nt(a, b):
    return lax.dot_general(a, b, NT, preferred_element_type=F32)


def _split(x, n):
    if x.dtype == BF16:
        return [x]
    parts, rem = [], x
    for i in range(n):
        parts.append(rem.astype(BF16))
        if i + 1 < n:
            rem = rem - parts[-1].astype(F32)
    return parts


def _mm(a, b, dims=NN, pa=1, pb=1):
    pas, pbs = _split(a, pa), _split(b, pb)
    order = max(len(pas), len(pbs))
    out = None
    for i, ai in enumerate(pas):
        for j, bj in enumerate(pbs):
            if i + j < order:
                term = lax.dot_general(ai, bj, dims, preferred_element_type=F32)
                out = term if out is None else out + term
    return out


def _rms(x, g):
    return x * lax.rsqrt(jnp.mean(x * x, axis=-1, keepdims=True) + RMS_EPS) * g


def _to_classes(dst_ref, stage, val, dil, width, off):
    tm = val.shape[0]
    for j in range(val.shape[1] // 128):
        stage[...] = val[:, 128 * j:128 * (j + 1)]
        for r in range(dil):
            lane0 = r * width + off + 128 * j
            dst_ref[:, lane0:lane0 + 128] = stage[pl.ds(r, tm // dil, stride=dil), :]


def _from_classes(src_ref, stage, dil, width):
    tm = stage.shape[0]
    chunks = []
    for j in range(width // 128):
        for r in range(dil):
            lane0 = r * width + 128 * j
            stage[pl.ds(r, tm // dil, stride=dil), :] = src_ref[:, lane0:lane0 + 128]
        chunks.append(stage[...])
    return jnp.concatenate(chunks, axis=1)


def _norm_proj_kernel(x_ref, g_ref, w_ref, *refs, pieces, transposed, dils):
    stage = refs[-1]
    out_refs = refs[:-1]
    xn = _rms(x_ref[...], g_ref[...]).astype(BF16)
    t_refs = iter(out_refs[len(pieces):])
    for idx, (out_ref, cols) in enumerate(zip(out_refs, pieces)):
        t_ref = next(t_refs) if idx in transposed else None
        width = sum(c1 - c0 for c0, c1 in cols)
        off = 0
        for c0, c1 in cols:
            res = _dot(xn, w_ref[:, c0:c1])
            if dils[idx] > 1:
                _to_classes(out_ref, stage, res, dils[idx], width, off)
            else:
                out_ref[:, off:off + c1 - c0] = res
            if t_ref is not None:
                t_ref[off:off + c1 - c0, :] = res.T
            off += c1 - c0


def _norm_proj(x, g, w, pieces, tm, transposed=(), dils=None, rows_per_batch=None):
    m, d = x.shape
    n = w.shape[1]
    dils = tuple(dils) if dils else (1,) * len(pieces)
    widths = [sum(c1 - c0 for c0, c1 in cols) for cols in pieces]
    tiles = rows_per_batch // tm if rows_per_batch else None
    out_specs, out_shape = [], []
    for wd, dil in zip(widths, dils):
        if dil > 1:
            out_specs.append(pl.BlockSpec((None, tm // dil, dil * wd), lambda i: (i // tiles, i % tiles, 0)))
            out_shape.append(jax.ShapeDtypeStruct((m // rows_per_batch, rows_per_batch // dil, dil * wd), F32))
        else:
            out_specs.append(pl.BlockSpec((tm, wd), lambda i: (i, 0)))
            out_shape.append(jax.ShapeDtypeStruct((m, wd), F32))
    out_specs += [pl.BlockSpec((None, widths[idx], tm), lambda i: (i // tiles, 0, i % tiles)) for idx in transposed]
    out_shape += [jax.ShapeDtypeStruct((m // rows_per_batch, widths[idx], rows_per_batch), F32) for idx in transposed]
    return pl.pallas_call(
        functools.partial(_norm_proj_kernel, pieces=pieces, transposed=tuple(transposed), dils=dils),
        grid=(m // tm,),
        in_specs=[pl.BlockSpec((tm, d), lambda i: (i, 0)),
                  pl.BlockSpec((1, d), lambda i: (0, 0)),
                  pl.BlockSpec((d, n), lambda i: (0, 0))],
        out_specs=out_specs,
        out_shape=out_shape,
        scratch_shapes=[pltpu.VMEM((tm, 128), F32)],
        compiler_params=_params("parallel"),
        name="norm_proj",
    )(x, g.reshape(1, d), w.astype(BF16))


def _mem_kv_kernel(x_ref, g_ref, wt_ref, o_ref):
    xn = _rms(x_ref[...], g_ref[...]).astype(BF16)
    o_ref[...] = _dot_nt(wt_ref[...], xn)


def _mem_kv(mem, g, w):
    b, n, d = mem.shape
    nl, _, wd = w.shape
    return pl.pallas_call(
        _mem_kv_kernel,
        grid=(nl, b),
        in_specs=[pl.BlockSpec((None, n, d), lambda l, bi: (bi, 0, 0)),
                  pl.BlockSpec((None, 1, d), lambda l, bi: (l, 0, 0)),
                  pl.BlockSpec((None, wd, d), lambda l, bi: (l, 0, 0))],
        out_specs=pl.BlockSpec((None, None, wd, n), lambda l, bi: (l, bi, 0, 0)),
        out_shape=jax.ShapeDtypeStruct((nl, b, wd, n), F32),
        compiler_params=_params("parallel", "parallel"),
        name="mem_kv",
    )(mem, g.reshape(nl, 1, d), jnp.swapaxes(w, 1, 2).astype(BF16))


def _dil_attn_kernel(q_ref, kvc_ref, kvp_ref, bias_ref, acc_ref, st_ref, *, nr, tiles, prev_block):
    i = pl.program_id(2)
    qf = q_ref[...] * SCALE
    kvc = kvc_ref[...].astype(BF16)
    kvp = kvp_ref[...].astype(BF16) if prev_block else None
    chains = [(rr, tt, h) for rr in range(nr) for tt in range(tiles) for h in range(GROUP_HEADS)]
    n = len(chains)
    rows = lambda tt: slice(tt * Q_TILE, (tt + 1) * Q_TILE)
    lo_lane = lax.broadcasted_iota(jnp.int32, (Q_TILE, 128), 1) < HEAD_DIM

    def keys(rr, tt, h, off):
        lanes = slice(rr * 2 * GROUP_W + off + (h // 2) * 128, rr * 2 * GROUP_W + off + (h // 2 + 1) * 128)
        return kvp[:, lanes] if tt < 0 else kvc[rows(tt), lanes]

    has_prev = [tt > 0 or prev_block for _, tt, _ in chains]
    qh = [jnp.where(lo_lane if h % 2 == 0 else jnp.logical_not(lo_lane),
                    qf[rows(tt), rr * GROUP_W + (h // 2) * 128:rr * GROUP_W + (h // 2 + 1) * 128], 0.0).astype(BF16)
          for rr, tt, h in chains]
    lc = [_dot_nt(qh[c], keys(rr, tt, h, 0)) + bias_ref[h, :, :Q_TILE] for c, (rr, tt, h) in enumerate(chains)]
    lp = [None] * n
    for c, (rr, tt, h) in enumerate(chains):
        if has_prev[c]:
            z = _dot_nt(qh[c], keys(rr, tt - 1, h, 0)) + bias_ref[h, :, Q_TILE:]
            lp[c] = jnp.where(i > 0, z, NEG_INF) if tt == 0 else z
    m = [jnp.max(z, axis=-1, keepdims=True) for z in lc]
    m = [jnp.maximum(m[c], jnp.max(lp[c], axis=-1, keepdims=True)) if has_prev[c] else m[c] for c in range(n)]
    pc = [jnp.exp(lc[c] - m[c]) for c in range(n)]
    pp = [jnp.exp(lp[c] - m[c]) if has_prev[c] else None for c in range(n)]
    den = [jnp.sum(z, axis=-1, keepdims=True) for z in pc]
    den = [den[c] + jnp.sum(pp[c], axis=-1, keepdims=True) if has_prev[c] else den[c] for c in range(n)]
    o = [_dot(pc[c].astype(BF16), keys(rr, tt, h, GROUP_W)) for c, (rr, tt, h) in enumerate(chains)]
    o = [o[c] + _dot(pp[c].astype(BF16), keys(rr, tt - 1, h, GROUP_W)) if has_prev[c] else o[c]
         for c, (rr, tt, h) in enumerate(chains)]
    lane = lax.broadcasted_iota(jnp.int32, (Q_TILE, 128), 1)
    st = None
    for c, (rr, tt, h) in enumerate(chains):
        if h % 2 == 1:
            pair = slice(rr * GROUP_W + (h // 2) * 128, rr * GROUP_W + (h // 2 + 1) * 128)
            acc_ref[rows(tt), pair] = jnp.where(lo_lane, o[c - 1], o[c])
        st = jnp.zeros((Q_TILE, 128), F32) if h == 0 else st
        st = jnp.where(lane == h, m[c], st)
        st = jnp.where(lane == GROUP_HEADS + h, den[c], st)
        if h == GROUP_HEADS - 1:
            st_ref[rows(tt), rr * 128:(rr + 1) * 128] = st


def _dil_attn(qv, kvv, bias_tiles, g):
    dil = DILATIONS[g]
    b, ln, _ = qv.shape
    tiles = min(ln // Q_TILE, DIL_SEGMENTS)
    nr = min(dil, DIL_SEGMENTS // tiles)
    nblk = ln // (tiles * Q_TILE)
    return pl.pallas_call(
        functools.partial(_dil_attn_kernel, nr=nr, tiles=tiles, prev_block=nblk > 1),
        grid=(b, dil // nr, nblk),
        in_specs=[pl.BlockSpec((None, tiles * Q_TILE, nr * GROUP_W), lambda bi, r, i: (bi, i, r)),
                  pl.BlockSpec((None, tiles * Q_TILE, nr * 2 * GROUP_W), lambda bi, r, i: (bi, i, r)),
                  pl.BlockSpec((None, Q_TILE if nblk > 1 else 8, nr * 2 * GROUP_W),
                               lambda bi, r, i: (bi, jnp.maximum(tiles * i - 1, 0), r)),
                  pl.BlockSpec((GROUP_HEADS, Q_TILE, 2 * Q_TILE), lambda bi, r, i: (g, 0, 0))],
        out_specs=[pl.BlockSpec((None, tiles * Q_TILE, nr * GROUP_W), lambda bi, r, i: (bi, i, r)),
                   pl.BlockSpec((None, tiles * Q_TILE, nr * 128), lambda bi, r, i: (bi, i, r))],
        out_shape=[jax.ShapeDtypeStruct((b, ln, dil * GROUP_W), F32),
                   jax.ShapeDtypeStruct((b, ln, dil * 128), F32)],
        compiler_params=_params("parallel", "parallel", "arbitrary"),
        name=f"dil_attn_d{dil}",
    )(qv, kvv, kvv, bias_tiles)


def _sample_attn_kernel(q_ref, kn0_ref, kn1_ref, kn2_ref, c0_ref, c1_ref, c2_ref, b0_ref, b1_ref, b2_ref,
                        bnew_ref, o_ref, *, n_new):
    q = (q_ref[...] * SCALE).astype(BF16)
    qf = q.astype(F32)
    groups = ((kn0_ref, c0_ref, b0_ref), (kn1_ref, c1_ref, b1_ref), (kn2_ref, c2_ref, b2_ref))
    new_rows = []
    for g, (kn_ref, _, _) in enumerate(groups):
        kn = kn_ref[...]
        shifted = [kn] + [pltpu.roll(kn, s, axis=0) for s in range(1, n_new if g == 0 else 1)]
        new_rows.append([z.astype(BF16).astype(F32) for z in shifted])
    chains = [(h, g) for h in range(GROUP_HEADS) for g in range(N_GROUPS)]
    ids = range(len(chains))
    ks = lambda h: slice(h * HEAD_DIM, (h + 1) * HEAD_DIM)
    vs = lambda h: slice(GROUP_W + h * HEAD_DIM, GROUP_W + (h + 1) * HEAD_DIM)
    qs = lambda h, g: slice(g * GROUP_W + h * HEAD_DIM, g * GROUP_W + (h + 1) * HEAD_DIM)
    lp = [_dot(q[:, qs(h, g)], groups[g][1][0, h].astype(BF16)) + groups[g][2][h] for h, g in chains]
    lns = [[jnp.sum(qf[:, qs(h, g)] * rows[:, ks(h)], axis=-1, keepdims=True) + bnew_ref[g, s][:, h:h + 1]
            for s, rows in enumerate(new_rows[g])] for h, g in chains]
    m = [functools.reduce(jnp.maximum, [jnp.max(lp[c], axis=-1, keepdims=True)] + lns[c]) for c in ids]
    pp = [jnp.exp(lp[c] - m[c]) for c in ids]
    pn = [[jnp.exp(ln - m[c]) for ln in lns[c]] for c in ids]
    den = [jnp.sum(pp[c], axis=-1, keepdims=True) + sum(pn[c]) for c in ids]
    o = [_dot_nt(pp[c].astype(BF16), groups[g][1][1, h].astype(BF16)) for c, (h, g) in enumerate(chains)]
    o = [o[c] + sum(z.astype(BF16).astype(F32) * rows[:, vs(h)] for z, rows in zip(pn[c], new_rows[g]))
         for c, (h, g) in enumerate(chains)]
    for h in range(GROUP_HEADS):
        cs = [h * N_GROUPS + g for g in range(N_GROUPS)]
        mx = functools.reduce(jnp.maximum, [m[c] for c in cs])
        wts = [jnp.exp(m[c] - mx) * den[c] for c in cs]
        num = sum(w * (o[c] / den[c]) for w, c in zip(wts, cs))
        o_ref[:, ks(h)] = num / sum(wts)


def _sample_attn(q, kvn, caches, bias_past, bias_new, n_new):
    b, tp, _ = q.shape
    new_spec = pl.BlockSpec((None, tp, 2 * GROUP_W), lambda bi: (bi, 0, 0))
    cache_spec = lambda w: pl.BlockSpec((None, 2, GROUP_HEADS, HEAD_DIM, w), lambda bi: (bi, 0, 0, 0, 0))
    bias_spec = lambda w: pl.BlockSpec((GROUP_HEADS, tp, w), lambda bi: (0, 0, 0))
    return pl.pallas_call(
        functools.partial(_sample_attn_kernel, n_new=n_new),
        grid=(b,),
        in_specs=[pl.BlockSpec((None, tp, A_Q), lambda bi: (bi, 0, 0)), new_spec, new_spec, new_spec]
                 + [cache_spec(w) for w in WINDOWS] + [bias_spec(w) for w in WINDOWS]
                 + [pl.BlockSpec((N_GROUPS, n_new, tp, 128), lambda bi: (0, 0, 0, 0))],
        out_specs=pl.BlockSpec((None, tp, GROUP_W), lambda bi: (bi, 0, 0)),
        out_shape=jax.ShapeDtypeStruct((b, tp, GROUP_W), F32),
        compiler_params=_params("parallel"),
        name="sample_attn",
    )(q, kvn[0], kvn[1], kvn[2], *caches, *bias_past, bias_new)


def _cross_attn(q_ref, kvt_ref, dst_ref, nb):
    tq = q_ref.shape[0] // nb
    sub = min(tq, Q_TILE)
    chains = [(bi, slice(r0, r0 + sub), h) for bi in range(nb) for r0 in range(bi * tq, (bi + 1) * tq, sub)
              for h in range(MEM_W // HEAD_DIM)]
    kvt = [kvt_ref[bi].astype(BF16) for bi in range(nb)]
    pair = lambda h, off=0: slice(off + (h // 2) * 128, off + (h // 2 + 1) * 128)
    lo_lane = lax.broadcasted_iota(jnp.int32, (sub, 128), 1) < HEAD_DIM
    qm = [jnp.where(lo_lane if h % 2 == 0 else jnp.logical_not(lo_lane), q_ref[rw, pair(h)] * SCALE, 0.0).astype(BF16)
          for bi, rw, h in chains]
    lg = [_dot(qm[c], kvt[bi][pair(h), :]) for c, (bi, rw, h) in enumerate(chains)]
    p = [jnp.exp(z - jnp.max(z, axis=-1, keepdims=True)) for z in lg]
    den = [jnp.sum(z, axis=-1, keepdims=True) for z in p]
    o = [_dot_nt(p[c].astype(BF16), kvt[bi][pair(h, MEM_W), :]) / den[c] for c, (bi, rw, h) in enumerate(chains)]
    for c, (bi, rw, h) in enumerate(chains):
        if h % 2 == 1:
            dst_ref[rw, pair(h)] = jnp.where(lo_lane, o[c - 1], o[c])


def _rwkv_kernel(cols_ref, shift_ref, s0_ref, mu_ref, w0_ref, wup_ref, a0_ref, aup_ref, kk_ref, ka_ref, rk_ref,
                 lnw_ref, lnb_ref, y_ref, sout_ref, shout_ref,
                 carry, state, r_s, k_s, v_s, a_s, b_s, lw_s, y_s, *, tt, chunk, t_valid):
    j = pl.program_id(1)
    t_in = cols_ref.shape[0]
    zero_head = jnp.zeros((HEAD_DIM, HEAD_DIM), F32)

    @pl.when(j == 0)
    def _():
        carry[...] = shift_ref[...]
        for p in range(RWKV_PAIRS):
            state[p] = jnp.concatenate([jnp.concatenate([s0_ref[2 * p], zero_head], axis=1),
                                        jnp.concatenate([zero_head, s0_ref[2 * p + 1]], axis=1)], axis=0)

    cols = cols_ref[...]
    if t_in < tt:
        cols = jnp.concatenate([cols, jnp.zeros((tt - t_in, cols.shape[1]), F32)], axis=0)
    y_dst = y_ref if t_in == tt else y_s
    row = lax.broadcasted_iota(jnp.int32, (tt, 1), 0)
    prev = jnp.where(row == 0, carry[...], pltpu.roll(cols, 1, axis=0))
    carry[...] = cols[t_valid - 1:t_valid, :]
    xs = cols + mu_ref[...] * (prev - cols)
    r = xs[:, :RWKV_W]
    k = xs[:, RWKV_W:2 * RWKV_W]
    v = xs[:, 2 * RWKV_W:3 * RWKV_W]
    wd = xs[:, 3 * RWKV_W:3 * RWKV_W + LORA]
    ad = xs[:, 3 * RWKV_W + LORA:]
    z = w0_ref[...] + _dot(jnp.tanh(wd).astype(BF16), wup_ref[...])
    log_decay = -math.exp(-0.5) * jax.nn.sigmoid(z)
    a = jax.nn.sigmoid(a0_ref[...] + _dot(ad.astype(BF16), aup_ref[...]))

    blockdiag = (lax.broadcasted_iota(jnp.int32, (128, 128), 0) // HEAD_DIM
                 == lax.broadcasted_iota(jnp.int32, (128, 128), 1) // HEAD_DIM)

    def head_sum(x):
        lo = lax.broadcasted_iota(jnp.int32, x.shape, 1) < HEAD_DIM
        s0 = jnp.sum(jnp.where(lo, x, 0.0), axis=-1, keepdims=True)
        s1 = jnp.sum(jnp.where(lo, 0.0, x), axis=-1, keepdims=True)
        return jnp.where(lo, s0, s1)

    kk = k * kk_ref[...]
    kk_sq = kk * kk
    k2 = k * (1.0 + (a - 1.0) * ka_ref[...])
    live = row < t_valid
    for p in range(RWKV_PAIRS):
        ps = slice(p * 128, (p + 1) * 128)
        nrm = jnp.maximum(jnp.sqrt(head_sum(kk_sq[:, ps])), 1e-12)
        kkn = kk[:, ps] / nrm
        if t_valid < tt:
            zero = jnp.zeros((tt, 128), F32)
            r_s[:, ps] = r[:, ps]
            k_s[:, ps] = jnp.where(live, k2[:, ps], zero)
            v_s[:, ps] = jnp.where(live, v[:, ps], zero)
            a_s[:, ps] = jnp.where(live, -kkn, zero)
            b_s[:, ps] = jnp.where(live, kkn * a[:, ps], zero)
            lw_s[:, ps] = jnp.where(live, log_decay[:, ps], zero)
        else:
            r_s[:, ps] = r[:, ps]
            k_s[:, ps] = k2[:, ps]
            v_s[:, ps] = v[:, ps]
            a_s[:, ps] = -kkn
            b_s[:, ps] = kkn * a[:, ps]
            lw_s[:, ps] = log_decay[:, ps]

    ci = lax.broadcasted_iota(jnp.int32, (chunk, chunk), 0)
    cj = lax.broadcasted_iota(jnp.int32, (chunk, chunk), 1)
    tri_incl = (ci >= cj).astype(BF16)
    lo_lane = lax.broadcasted_iota(jnp.int32, (chunk, 128), 1) < HEAD_DIM
    levels = max(1, math.ceil(math.log2(min(chunk, t_valid))))
    n2 = 2 * chunk
    ri = lax.broadcasted_iota(jnp.int32, (n2, n2), 0)
    rj = lax.broadcasted_iota(jnp.int32, (n2, n2), 1)
    same = ri // chunk == rj // chunk
    strict = jnp.logical_and(same, ri > rj)
    eye = (ri == rj).astype(F32)
    incl = jnp.logical_and(same, ri >= rj)
    own = lax.broadcasted_iota(jnp.int32, (n2, 128), 0) // chunk == lax.broadcasted_iota(jnp.int32, (n2, 128), 1) // HEAD_DIM
    dup = lambda z: jnp.concatenate([z, z], axis=0)
    cat = jnp.concatenate

    n_chunks = tt // chunk
    group = SCAN_INTERLEAVE if n_chunks % SCAN_INTERLEAVE == 0 else 1

    def chunk_body(ci, _):
        rows = [pl.ds((ci * group + cc) * chunk, chunk) for cc in range(group)]
        sl = [slice(p * 128, (p + 1) * 128) for p in range(RWKV_PAIRS)]
        chains = [(rows[cc], sl[p]) for cc in range(group) for p in range(RWKV_PAIRS)]
        ids = range(len(chains))
        cum_all = [_mm(tri_incl, lw_s[rw, :], pb=3) for rw in rows]
        cum = [cum_all[cc][:, s] for cc in range(group) for s in sl]
        p_incl = [jnp.exp(z) for z in cum]
        p_inv = [jnp.exp(-z) for z in cum]
        rr = [r_s[rw, s] for rw, s in chains]
        kc = [k_s[rw, s] for rw, s in chains]
        vc = [v_s[rw, s] for rw, s in chains]
        at2 = [jnp.where(own, dup(a_s[rw, s] * jnp.exp(cum[c] - lw_s[rw, s])), 0.0) for c, (rw, s) in enumerate(chains)]
        rt2 = [jnp.where(own, dup(rr[c] * p_incl[c]), 0.0) for c in ids]
        bt = [(b_s[rw, s] * p_inv[c]).astype(BF16) for c, (rw, s) in enumerate(chains)]
        kt = [(kc[c] * p_inv[c]).astype(BF16) for c in ids]
        vb = [z.astype(BF16) for z in vc]
        v2 = [dup(z) for z in vb]
        g = [_mm(cat([at2[c], rt2[c]], 0).astype(BF16), cat([dup(bt[c]), dup(kt[c])], 0), NT) for c in ids]
        a_ak = [jnp.where(strict, z[:n2, n2:], 0.0).astype(BF16) for z in g]
        apow = [jnp.where(strict, z[:n2, :n2], 0.0).astype(BF16) for z in g]
        a_r = [cat([jnp.where(incl, z[n2:, :n2], 0.0), jnp.where(incl, z[n2:, n2:], 0.0)], 1).astype(BF16)
               for z in g]
        akv = [_mm(a_ak[c], v2[c]) for c in ids]
        tinv = [eye + jnp.where(strict, z[:n2, :n2], 0.0) for z in g]
        for _ in range(levels - 1):
            apow = [_mm(z, z).astype(BF16) for z in apow]
            tinv = [tinv[c] + _mm(tinv[c].astype(BF16), apow[c]) for c in ids]
        sol = [_mm(tinv[c].astype(BF16), cat([at2[c], akv[c]], axis=1).astype(BF16)) for c in ids]
        ws = [z[:, :128] for z in sol]
        u0s = [jnp.where(own, z[:, 128:], 0.0) for z in sol]
        zeros2 = jnp.zeros((n2, 128), BF16)
        qy = [_mm(a_r[c], cat([cat([ws[c], u0s[c]], 1).astype(BF16), cat([zeros2, v2[c]], 1)], 0)) for c in ids]
        qs = [(rt2[c] + qy[c][:, :128]).astype(BF16) for c in ids]
        zeros1 = jnp.zeros((chunk, 128), BF16)
        lhs = [cat([cat([ws[c][:chunk] + ws[c][chunk:], u0s[c][:chunk] + u0s[c][chunk:]], 1).astype(BF16),
                    cat([zeros1, vb[c]], 1)], 0) for c in ids]
        wbn = [_mm(lhs[c], cat([bt[c], kt[c]], 0), TN) for c in ids]
        wb = [z[:128].astype(BF16) for z in wbn]
        s_cur = [state[p] for p in range(RWKV_PAIRS)]
        ys = []
        for cc in range(group):
            base = cc * RWKV_PAIRS
            s_b = [z.astype(BF16) for z in s_cur]
            ys += [_mm(qs[base + p], s_b[p], NT) + qy[base + p][:, 128:] for p in range(RWKV_PAIRS)]
            sw = [_mm(s_b[p], wb[base + p]) for p in range(RWKV_PAIRS)]
            s_cur = [jnp.where(blockdiag, (s_cur[p] + sw[p] + wbn[base + p][128:])
                               * p_incl[base + p][chunk - 1:chunk, :], 0.0) for p in range(RWKV_PAIRS)]
        for p in range(RWKV_PAIRS):
            state[p] = s_cur[p]
        y = [jnp.where(lo_lane, z[:chunk], z[chunk:]) for z in ys]

        bonus = [head_sum(rr[c] * kc[c] * rk_ref[:, s]) for c, (_, s) in enumerate(chains)]
        dlt = [y[c] - head_sum(y[c]) * (1.0 / HEAD_DIM) for c in ids]
        var = [head_sum(z * z) * (1.0 / HEAD_DIM) for z in dlt]
        for c, (rw, s) in enumerate(chains):
            yn = dlt[c] * lax.rsqrt(var[c] + GN_EPS) * lnw_ref[:, s] + lnb_ref[:, s]
            y_dst[rw, s] = yn + bonus[c] * vc[c]
        return 0

    for ci in range(n_chunks // group):
        chunk_body(ci, 0)
    if t_in < tt:
        y_ref[...] = y_s[0:t_in, :]

    @pl.when(j == pl.num_programs(1) - 1)
    def _():
        for p in range(RWKV_PAIRS):
            sout_ref[2 * p] = state[p, 0:HEAD_DIM, 0:HEAD_DIM]
            sout_ref[2 * p + 1] = state[p, HEAD_DIM:, HEAD_DIM:]
        shout_ref[...] = carry[...]


def _rwkv_scan(cols, shift_prev, s0, prm, tt, t_valid):
    b, t, _ = cols.shape
    t_blk = min(t, tt)
    heads = pl.BlockSpec((None, 2 * RWKV_PAIRS, HEAD_DIM, HEAD_DIM), lambda bi, j: (bi, 0, 0, 0))
    row = lambda n: pl.BlockSpec((1, n), lambda bi, j: (0, 0))
    vec = lambda a: a.reshape(1, -1)
    y, s_out, sh_out = pl.pallas_call(
        functools.partial(_rwkv_kernel, tt=tt, chunk=SCAN_CHUNK, t_valid=min(t_valid, tt)),
        grid=(b, t // t_blk),
        in_specs=[pl.BlockSpec((None, t_blk, C_SHIFT), lambda bi, j: (bi, j, 0)),
                  pl.BlockSpec((None, 1, C_SHIFT), lambda bi, j: (bi, 0, 0)),
                  heads,
                  row(C_SHIFT), row(RWKV_W),
                  pl.BlockSpec((LORA, RWKV_W), lambda bi, j: (0, 0)),
                  row(RWKV_W),
                  pl.BlockSpec((LORA, RWKV_W), lambda bi, j: (0, 0)),
                  row(RWKV_W), row(RWKV_W), row(RWKV_W), row(RWKV_W), row(RWKV_W)],
        out_specs=[pl.BlockSpec((None, t_blk, RWKV_W), lambda bi, j: (bi, j, 0)),
                   heads,
                   pl.BlockSpec((None, 1, C_SHIFT), lambda bi, j: (bi, 0, 0))],
        out_shape=[jax.ShapeDtypeStruct((b, t, RWKV_W), F32),
                   jax.ShapeDtypeStruct((b, 2 * RWKV_PAIRS, HEAD_DIM, HEAD_DIM), F32),
                   jax.ShapeDtypeStruct((b, 1, C_SHIFT), F32)],
        scratch_shapes=[pltpu.VMEM((1, C_SHIFT), F32), pltpu.VMEM((RWKV_PAIRS, 128, 128), F32)]
                       + [pltpu.VMEM((tt, RWKV_W), F32)] * 7,
        compiler_params=_params("parallel", "arbitrary"),
        name="rwkv_scan",
    )(cols, shift_prev.reshape(b, 1, C_SHIFT), s0, vec(prm["mu"]), vec(prm["w0"]), prm["w_up"].astype(BF16),
      vec(prm["a0"]), prm["a_up"].astype(BF16), vec(prm["k_k"]), vec(prm["k_a"]), vec(prm["r_k"]),
      vec(prm["ln_w"]), vec(prm["ln_b"]))
    return y, s_out, sh_out.reshape(b, C_SHIFT)


def _post_kernel(*refs, merge_dils, mix_w, nb):
    x_ref, qmem_ref, kvt_ref, gate_ref, w_ref, g_ref = refs[:6]
    mix_refs = refs[6:-3]
    y_ref, stage, omem = refs[-3:]
    _cross_attn(qmem_ref, kvt_ref, omem, nb)
    if merge_dils:
        er = lax.broadcasted_iota(jnp.int32, (128, 2 * GROUP_W), 0)
        ec = lax.broadcasted_iota(jnp.int32, (128, 2 * GROUP_W), 1)
        expand = (er == ec // HEAD_DIM).astype(BF16)
        ms, ss, os_ = [], [], []
        for g, dil in enumerate(merge_dils):
            acc_ref, st_ref = mix_refs[2 * g], mix_refs[2 * g + 1]
            acc = acc_ref[...] if dil == 1 else _from_classes(acc_ref, stage, dil, GROUP_W)
            stats = st_ref[...] if dil == 1 else _from_classes(st_ref, stage, dil, 128)
            st = _mm(stats, expand, pa=3)
            ms.append(st[:, :GROUP_W])
            ss.append(st[:, GROUP_W:])
            os_.append(acc / st[:, GROUP_W:])
        mx = functools.reduce(jnp.maximum, ms)
        wts = [jnp.exp(m - mx) * s for m, s in zip(ms, ss)]
        mix = sum(w * o for w, o in zip(wts, os_)) / sum(wts)
    else:
        mix = mix_refs[0][...]
    gate = gate_ref[...]
    act = gate * jax.nn.sigmoid(gate)
    h1 = (mix * act[:, :mix_w]).astype(BF16)
    h2 = (omem[...] * act[:, mix_w:]).astype(BF16)
    out = _dot(h1, w_ref[:mix_w, :]) + _dot(h2, w_ref[mix_w:, :])
    y_ref[...] = x_ref[...] + _rms(out, g_ref[...])


def _post(x, mix, q_mem, kvt, layer, gate, w_out, g_post, tm, rows_per_batch):
    m, d = x.shape
    merge = isinstance(mix, (list, tuple))
    gw = gate.shape[1]
    rows = lambda wd: pl.BlockSpec((tm, wd), lambda i: (i, 0))
    nb = max(1, tm // rows_per_batch)
    per_batch = max(1, rows_per_batch // tm)
    kvt_spec = pl.BlockSpec((None, nb, 2 * MEM_W, N_MEM), lambda i: (layer, i // per_batch, 0, 0))
    if merge:
        mix_w = GROUP_W
        merge_dils = tuple(acc.shape[2] // GROUP_W for acc, _ in mix)
        tiles = mix[0][0].shape[1] * merge_dils[0] // tm
        mix_arrays = [a for pair in mix for a in pair]
        mix_specs = [pl.BlockSpec((None, tm // dil, a.shape[2]), lambda i: (i // tiles, i % tiles, 0))
                     for dil, pair in zip(merge_dils, mix) for a in pair]
    else:
        mix_w, merge_dils, mix_arrays, mix_specs = mix.shape[1], (), [mix], [rows(mix.shape[1])]
    return pl.pallas_call(
        functools.partial(_post_kernel, merge_dils=merge_dils, mix_w=mix_w, nb=nb),
        grid=(m // tm,),
        in_specs=[rows(d), rows(MEM_W), kvt_spec, rows(gw),
                  pl.BlockSpec((gw, d), lambda i: (0, 0)),
                  pl.BlockSpec((1, d), lambda i: (0, 0))] + mix_specs,
        out_specs=rows(d),
        out_shape=jax.ShapeDtypeStruct((m, d), F32),
        scratch_shapes=[pltpu.VMEM((tm, 128), F32), pltpu.VMEM((tm, MEM_W), F32)],
        compiler_params=_params("parallel"),
        name="post",
    )(x, q_mem, kvt, gate, w_out.astype(BF16), g_post.reshape(1, d), *mix_arrays)


def _t5_bucket(dist):
    max_exact = N_BUCKETS // 2
    d = jnp.maximum(dist, 1).astype(F32)
    large = max_exact + (jnp.log(d / max_exact) / math.log(MAX_DISTANCE / max_exact)
                         * (N_BUCKETS - max_exact)).astype(jnp.int32)
    large = jnp.minimum(large, N_BUCKETS - 1)
    return jnp.where(dist < max_exact, dist, large)


def _group_bias(rel_bias, g):
    dist = DILATIONS[g] * jnp.arange(A_KEYS, dtype=jnp.int32)
    bias = rel_bias[_t5_bucket(dist)]
    return bias[:, g * GROUP_HEADS:(g + 1) * GROUP_HEADS].T.astype(F32)


def _bias_tiles_kernel(x_ref, o_ref):
    row = lax.broadcasted_iota(jnp.int32, (Q_TILE, 2 * Q_TILE), 0)
    for h in range(o_ref.shape[0]):
        t = jnp.broadcast_to(x_ref[h:h + 1, :], (Q_TILE, 2 * Q_TILE))
        for bit in range(int(math.log2(Q_TILE))):
            t = jnp.where((row >> bit) & 1 == 1, pltpu.roll(t, 1 << bit, axis=1), t)
        o_ref[h] = t


def _prompt_bias(biases):
    bias = jnp.concatenate(biases, axis=0)
    neg = jnp.full((bias.shape[0], Q_TILE - 1), NEG_INF, F32)
    table = jnp.concatenate([bias[:, :1], neg, bias[:, :0:-1]], axis=1)
    n = table.shape[0]
    return pl.pallas_call(
        _bias_tiles_kernel,
        out_shape=jax.ShapeDtypeStruct((n, Q_TILE, 2 * Q_TILE), F32),
        name="bias_tiles",
    )(table)


def _sample_bias(biases, n_new, tp):
    t = np.arange(tp)[:, None]
    real = t < n_new
    past = []
    for g, bias in enumerate(biases):
        w, dil = WINDOWS[g], DILATIONS[g]
        spread = jnp.concatenate([bias[:, :0:-1, None], jnp.full((GROUP_HEADS, w // dil, dil - 1), NEG_INF, F32)],
                                 axis=-1).reshape(GROUP_HEADS, w)
        rows = [jnp.concatenate([jnp.full((GROUP_HEADS, tt), NEG_INF, F32), spread[:, :w - tt]], axis=-1)
                for tt in range(n_new)] + [jnp.zeros((GROUP_HEADS, w), F32)] * (tp - n_new)
        past.append(jnp.stack(rows, axis=1))
    s = np.arange(n_new)[None, :, None]
    g_idx = np.arange(N_GROUPS)[:, None, None]
    ok = (s <= t[None, :, 0]) & real[None, :, 0] & ((g_idx == 0) | (s == 0))
    vals = jnp.stack([bias[:, :n_new].T for bias in biases])
    new = jnp.where(jnp.asarray(ok)[..., None], vals[:, :, None, :], NEG_INF)
    return past, jnp.pad(new, ((0, 0), (0, 0), (0, 0), (0, 128 - GROUP_HEADS)))


def _pieces_a():
    q = [((g * GROUP_W, (g + 1) * GROUP_W),) for g in range(N_GROUPS)]
    kv = [((A_Q + g * GROUP_W, A_Q + (g + 1) * GROUP_W), (2 * A_Q + g * GROUP_W, 2 * A_Q + (g + 1) * GROUP_W))
          for g in range(N_GROUPS)]
    qmem = ((3 * A_Q, 3 * A_Q + MEM_W),)
    gate = ((3 * A_Q + MEM_W, 3 * A_Q + MEM_W + GROUP_W + MEM_W),)
    return tuple(q + kv + [qmem, gate])


def _pieces_b():
    return (((0, C_SHIFT),), ((C_SHIFT, C_SHIFT + MEM_W),), ((C_SHIFT + MEM_W, C_SHIFT + MEM_W + RWKV_W + MEM_W),))


def kernel(x_prompt, x_sample, mem_prompt, cache_mem_kv, cache_win0, cache_win1, cache_win2, state_wkv, state_shift, norm_pre, norm_post, norm_mem, w_mem_kv, rel_bias, w_in_a, w_out_a, w_in_b, w_out_b, rwkv_mu, rwkv_w0, rwkv_w_up, rwkv_a0, rwkv_a_up, rwkv_k_k, rwkv_k_a, rwkv_r_k, rwkv_ln_w, rwkv_ln_b):
    bp, tp, d = x_prompt.shape
    bs, ts, _ = x_sample.shape
    tsp = SAMPLE_PAD_T
    xp = x_prompt.reshape(bp * tp, d)
    xs = jnp.pad(x_sample, ((0, 0), (0, tsp - ts), (0, 0))).reshape(bs * tsp, d)
    tm_p, tm_s = 512, bs * tsp
    tm_post_s = 8 * tsp
    time_minor = lambda c: jnp.moveaxis(c, -4, -1)
    caches = [time_minor(c[0]) for c in (cache_win0, cache_win1, cache_win2)]
    mkv_s = time_minor(cache_mem_kv).reshape(cache_mem_kv.shape[0], bs, 2 * MEM_W, N_MEM)
    mkv_p = _mem_kv(mem_prompt, norm_mem, w_mem_kv)

    biases = [_group_bias(rel_bias, g) for g in range(N_GROUPS)]
    outs_p = _norm_proj(xp, norm_pre[0], w_in_a[0], _pieces_a(), tm_p, transposed=(3, 4, 5),
                        dils=DILATIONS + DILATIONS + (1, 1), rows_per_batch=tp)
    outs_s = _norm_proj(xs, norm_pre[0], w_in_a[0], _pieces_a(), tm_s)
    q_p, kv_p, qmem_p, gate_p, kvt_p = outs_p[0:3], outs_p[3:6], outs_p[6], outs_p[7], outs_p[8:11]
    q_s, kv_s, qmem_s, gate_s = outs_s[0:3], outs_s[3:6], outs_s[6], outs_s[7]

    bias_tiles = _prompt_bias(biases)
    merged = [_dil_attn(q_p[g].reshape(bp, tp // DILATIONS[g], -1), kv_p[g].reshape(bp, tp // DILATIONS[g], -1),
                        bias_tiles, g) for g in range(N_GROUPS)]
    xp1 = _post(xp, merged, qmem_p, mkv_p, 0, gate_p, w_out_a[0], norm_post[0], tm_p, tp)

    bias_past, bias_new = _sample_bias(biases, ts, tsp)
    q_s_all = jnp.concatenate(q_s, axis=-1).reshape(bs, tsp, A_Q)
    kvn = [a.reshape(bs, tsp, 2 * GROUP_W) for a in kv_s]
    o_s = _sample_attn(q_s_all, kvn, caches, bias_past, bias_new, ts)
    xs1 = _post(xs, o_s.reshape(bs * tsp, GROUP_W), qmem_s, mkv_s, 0, gate_s, w_out_a[0], norm_post[0],
                tm_post_s, tsp)

    cols_p, qmem_p, gate_p = _norm_proj(xp1, norm_pre[1], w_in_b[0], _pieces_b(), tm_p)
    cols_s, qmem_s, gate_s = _norm_proj(xs1, norm_pre[1], w_in_b[0], _pieces_b(), tm_s)
    prm = dict(mu=rwkv_mu[0], w0=rwkv_w0[0], w_up=rwkv_w_up[0], a0=rwkv_a0[0], a_up=rwkv_a_up[0], k_k=rwkv_k_k[0],
               k_a=rwkv_k_a[0], r_k=rwkv_r_k[0], ln_w=rwkv_ln_w[0], ln_b=rwkv_ln_b[0])
    y_p, wkv_p, sh_p = _rwkv_scan(cols_p.reshape(bp, tp, C_SHIFT), jnp.zeros((bp, C_SHIFT), F32),
                                  jnp.zeros((bp, 2 * RWKV_PAIRS, HEAD_DIM, HEAD_DIM), F32), prm, 256, tp)
    y_s, wkv_s, sh_s = _rwkv_scan(cols_s.reshape(bs, tsp, C_SHIFT), state_shift[0], state_wkv[0], prm, SCAN_CHUNK, ts)
    y_s = y_s.reshape(bs * tsp, RWKV_W)
    xp2 = _post(xp1, y_p.reshape(bp * tp, RWKV_W), qmem_p, mkv_p, 1, gate_p, w_out_b[0], norm_post[1], tm_p, tp)
    xs2 = _post(xs1, y_s, qmem_s, mkv_s, 1, gate_s, w_out_b[0], norm_post[1], tm_post_s, tsp)

    kv_shape = (2, GROUP_HEADS, HEAD_DIM)
    time_major = lambda c: jnp.moveaxis(c, -1, -4)
    new_mem_kv = time_major(mkv_p.reshape(mkv_p.shape[0], bp, 2, MEM_W // HEAD_DIM, HEAD_DIM, N_MEM))
    win_p = [time_major(kvt_p[g].reshape(bp, *kv_shape, tp)[..., tp - min(WINDOWS[g], tp):])[None]
             for g in range(N_GROUPS)]
    win_s = [kv_s[g].reshape(bs, tsp, *kv_shape)[None, :, :ts] for g in range(N_GROUPS)]
    return (xp2.reshape(bp, tp, d), xs2.reshape(bs, tsp, d)[:, :ts], new_mem_kv,
            win_p[0], win_p[1], win_p[2], win_s[0], win_s[1], win_s[2],
            wkv_p[None], wkv_s[None], sh_p[None], sh_s[None])
```

```python
import functools
import math

import numpy as np
import jax
import jax.numpy as jnp
from jax import lax
from jax.experimental import pallas as pl
from jax.experimental.pallas import tpu as pltpu

F32 = jnp.float32
BF16 = jnp.bfloat16

D_MODEL = 1024
HEAD_DIM = 64
N_GROUPS = 3
GROUP_HEADS = 4
WINDOWS = (128, 512, 2048)
DILATIONS = (1, 4, 16)
A_HEADS = N_GROUPS * GROUP_HEADS
A_KEYS = 129
GROUP_W = GROUP_HEADS * HEAD_DIM
A_Q = A_HEADS * HEAD_DIM
N_MEM = 256
MEM_W = 256
RWKV_W = 768
RWKV_PAIRS = RWKV_W // 128
LORA = 64
C_SHIFT = 3 * RWKV_W + 2 * LORA
N_BUCKETS = 32
MAX_DISTANCE = WINDOWS[-1]
RMS_EPS = 1e-6
GN_EPS = 64e-5
NEG_INF = -1e30
SCALE = HEAD_DIM ** -0.5

Q_TILE = 128
DIL_SEGMENTS = 4
SCAN_CHUNK = 64
SCAN_INTERLEAVE = 2
SAMPLE_PAD_T = 8
VMEM_LIMIT = 56 * 1024 * 1024
NN =(((1,), (0,)), ((), ()))
NT = (((1,), (1,)), ((), ()))
TN = (((0,), (0,)), ((), ()))


def _params(*sem):
    return pltpu.CompilerParams(dimension_semantics=sem, vmem_limit_bytes=VMEM_LIMIT)


def _dot(a, b):
    return jnp.dot(a, b, preferred_element_type=F32)


def _dot_nt(a, b):
    return lax.dot_general(a, b, NT, preferred_element_type=F32)


def _split(x, n):
    if x.dtype == BF16:
        return [x]
    parts, rem = [], x
    for i in range(n):
        parts.append(rem.astype(BF16))
        if i + 1 < n:
            rem = rem - parts[-1].astype(F32)
    return parts


def _mm(a, b, dims=NN, pa=1, pb=1):
    pas, pbs = _split(a, pa), _split(b, pb)
    order = max(len(pas), len(pbs))
    out = None
    for i, ai in enumerate(pas):
        for j, bj in enumerate(pbs):
            if i + j < order:
                term = lax.dot_general(ai, bj, dims, preferred_element_type=F32)
                out = term if out is None else out + term
    return out


def _rms(x, g):
    return x * lax.rsqrt(jnp.mean(x * x, axis=-1, keepdims=True) + RMS_EPS) * g


def _to_classes(dst_ref, stage, val, dil, width, off):
    tm = val.shape[0]
    for j in range(val.shape[1] // 128):
        stage[...] = val[:, 128 * j:128 * (j + 1)]
        for r in range(dil):
            lane0 = r * width + off + 128 * j
            dst_ref[:, lane0:lane0 + 128] = stage[pl.ds(r, tm // dil, stride=dil), :]


def _from_classes(src_ref, stage, dil, width):
    tm = stage.shape[0]
    chunks = []
    for j in range(width // 128):
        for r in range(dil):
            lane0 = r * width + 128 * j
            stage[pl.ds(r, tm // dil, stride=dil), :] = src_ref[:, lane0:lane0 + 128]
        chunks.append(stage[...])
    return jnp.concatenate(chunks, axis=1)


def _norm_proj_kernel(x_ref, g_ref, w_ref, *refs, pieces, transposed, dils):
    stage = refs[-1]
    out_refs = refs[:-1]
    xn = _rms(x_ref[...], g_ref[...]).astype(BF16)
    t_refs = iter(out_refs[len(pieces):])
    for idx, (out_ref, cols) in enumerate(zip(out_refs, pieces)):
        t_ref = next(t_refs) if idx in transposed else None
        width = sum(c1 - c0 for c0, c1 in cols)
        off = 0
        for c0, c1 in cols:
            res = _dot(xn, w_ref[:, c0:c1])
            if dils[idx] > 1:
                _to_classes(out_ref, stage, res, dils[idx], width, off)
            else:
                out_ref[:, off:off + c1 - c0] = res
            if t_ref is not None:
                t_ref[off:off + c1 - c0, :] = res.T
            off += c1 - c0


def _norm_proj(x, g, w, pieces, tm, transposed=(), dils=None, rows_per_batch=None):
    m, d = x.shape
    n = w.shape[1]
    dils = tuple(dils) if dils else (1,) * len(pieces)
    widths = [sum(c1 - c0 for c0, c1 in cols) for cols in pieces]
    tiles = rows_per_batch // tm if rows_per_batch else None
    out_specs, out_shape = [], []
    for wd, dil in zip(widths, dils):
        if dil > 1:
            out_specs.append(pl.BlockSpec((None, tm // dil, dil * wd), lambda i: (i // tiles, i % tiles, 0)))
            out_shape.append(jax.ShapeDtypeStruct((m // rows_per_batch, rows_per_batch // dil, dil * wd), F32))
        else:
            out_specs.append(pl.BlockSpec((tm, wd), lambda i: (i, 0)))
            out_shape.append(jax.ShapeDtypeStruct((m, wd), F32))
    out_specs += [pl.BlockSpec((None, widths[idx], tm), lambda i: (i // tiles, 0, i % tiles)) for idx in transposed]
    out_shape += [jax.ShapeDtypeStruct((m // rows_per_batch, widths[idx], rows_per_batch), F32) for idx in transposed]
    return pl.pallas_call(
        functools.partial(_norm_proj_kernel, pieces=pieces, transposed=tuple(transposed), dils=dils),
        grid=(m // tm,),
        in_specs=[pl.BlockSpec((tm, d), lambda i: (i, 0)),
                  pl.BlockSpec((1, d), lambda i: (0, 0)),
                  pl.BlockSpec((d, n), lambda i: (0, 0))],
        out_specs=out_specs,
        out_shape=out_shape,
        scratch_shapes=[pltpu.VMEM((tm, 128), F32)],
        compiler_params=_params("parallel"),
        name="norm_proj",
    )(x, g.reshape(1, d), w.astype(BF16))


def _mem_kv_kernel(x_ref, g_ref, wt_ref, o_ref):
    xn = _rms(x_ref[...], g_ref[...]).astype(BF16)
    o_ref[...] = _dot_nt(wt_ref[...], xn)


def _mem_kv(mem, g, w):
    b, n, d = mem.shape
    nl, _, wd = w.shape
    return pl.pallas_call(
        _mem_kv_kernel,
        grid=(nl, b),
        in_specs=[pl.BlockSpec((None, n, d), lambda l, bi: (bi, 0, 0)),
                  pl.BlockSpec((None, 1, d), lambda l, bi: (l, 0, 0)),
                  pl.BlockSpec((None, wd, d), lambda l, bi: (l, 0, 0))],
        out_specs=pl.BlockSpec((None, None, wd, n), lambda l, bi: (l, bi, 0, 0)),
        out_shape=jax.ShapeDtypeStruct((nl, b, wd, n), F32),
        compiler_params=_params("parallel", "parallel"),
        name="mem_kv",
    )(mem, g.reshape(nl, 1, d), jnp.swapaxes(w, 1, 2).astype(BF16))


def _dil_attn_kernel(q_ref, kvc_ref, kvp_ref, bias_ref, acc_ref, st_ref, *, nr, tiles, prev_block):
    i = pl.program_id(2)
    qf = q_ref[...] * SCALE
    kvc = kvc_ref[...].astype(BF16)
    kvp = kvp_ref[...].astype(BF16) if prev_block else None
    chains = [(rr, tt, h) for rr in range(nr) for tt in range(tiles) for h in range(GROUP_HEADS)]
    n = len(chains)
    rows = lambda tt: slice(tt * Q_TILE, (tt + 1) * Q_TILE)
    lo_lane = lax.broadcasted_iota(jnp.int32, (Q_TILE, 128), 1) < HEAD_DIM

    def keys(rr, tt, h, off):
        lanes = slice(rr * 2 * GROUP_W + off + (h // 2) * 128, rr * 2 * GROUP_W + off + (h // 2 + 1) * 128)
        return kvp[:, lanes] if tt < 0 else kvc[rows(tt), lanes]

    has_prev = [tt > 0 or prev_block for _, tt, _ in chains]
    qh = [jnp.where(lo_lane if h % 2 == 0 else jnp.logical_not(lo_lane),
                    qf[rows(tt), rr * GROUP_W + (h // 2) * 128:rr * GROUP_W + (h // 2 + 1) * 128], 0.0).astype(BF16)
          for rr, tt, h in chains]
    lc = [_dot_nt(qh[c], keys(rr, tt, h, 0)) + bias_ref[h, :, :Q_TILE] for c, (rr, tt, h) in enumerate(chains)]
    lp = [None] * n
    for c, (rr, tt, h) in enumerate(chains):
        if has_prev[c]:
            z = _dot_nt(qh[c], keys(rr, tt - 1, h, 0)) + bias_ref[h, :, Q_TILE:]
            lp[c] = jnp.where(i > 0, z, NEG_INF) if tt == 0 else z
    m = [jnp.max(z, axis=-1, keepdims=True) for z in lc]
    m = [jnp.maximum(m[c], jnp.max(lp[c], axis=-1, keepdims=True)) if has_prev[c] else m[c] for c in range(n)]
    pc = [jnp.exp(lc[c] - m[c]) for c in range(n)]
    pp = [jnp.exp(lp[c] - m[c]) if has_prev[c] else None for c in range(n)]
    den = [jnp.sum(z, axis=-1, keepdims=True) for z in pc]
    den = [den[c] + jnp.sum(pp[c], axis=-1, keepdims=True) if has_prev[c] else den[c] for c in range(n)]
    o = [_dot(pc[c].astype(BF16), keys(rr, tt, h, GROUP_W)) for c, (rr, tt, h) in enumerate(chains)]
    o = [o[c] + _dot(pp[c].astype(BF16), keys(rr, tt - 1, h, GROUP_W)) if has_prev[c] else o[c]
         for c, (rr, tt, h) in enumerate(chains)]
    lane = lax.broadcasted_iota(jnp.int32, (Q_TILE, 128), 1)
    st = None
    for c, (rr, tt, h) in enumerate(chains):
        if h % 2 == 1:
            pair = slice(rr * GROUP_W + (h // 2) * 128, rr * GROUP_W + (h // 2 + 1) * 128)
            acc_ref[rows(tt), pair] = jnp.where(lo_lane, o[c - 1], o[c])
        st = jnp.zeros((Q_TILE, 128), F32) if h == 0 else st
        st = jnp.where(lane == h, m[c], st)
        st = jnp.where(lane == GROUP_HEADS + h, den[c], st)
        if h == GROUP_HEADS - 1:
            st_ref[rows(tt), rr * 128:(rr + 1) * 128] = st


def _dil_attn(qv, kvv, bias_tiles, g):
    dil = DILATIONS[g]
    b, ln, _ = qv.shape
    tiles = min(ln // Q_TILE, DIL_SEGMENTS)
    nr = min(dil, DIL_SEGMENTS // tiles)
    nblk = ln // (tiles * Q_TILE)
    return pl.pallas_call(
        functools.partial(_dil_attn_kernel, nr=nr, tiles=tiles, prev_block=nblk > 1),
        grid=(b, dil // nr, nblk),
        in_specs=[pl.BlockSpec((None, tiles * Q_TILE, nr * GROUP_W), lambda bi, r, i: (bi, i, r)),
                  pl.BlockSpec((None, tiles * Q_TILE, nr * 2 * GROUP_W), lambda bi, r, i: (bi, i, r)),
                  pl.BlockSpec((None, Q_TILE if nblk > 1 else 8, nr * 2 * GROUP_W),
                               lambda bi, r, i: (bi, jnp.maximum(tiles * i - 1, 0), r)),
                  pl.BlockSpec((GROUP_HEADS, Q_TILE, 2 * Q_TILE), lambda bi, r, i: (g, 0, 0))],
        out_specs=[pl.BlockSpec((None, tiles * Q_TILE, nr * GROUP_W), lambda bi, r, i: (bi, i, r)),
                   pl.BlockSpec((None, tiles * Q_TILE, nr * 128), lambda bi, r, i: (bi, i, r))],
        out_shape=[jax.ShapeDtypeStruct((b, ln, dil * GROUP_W), F32),
                   jax.ShapeDtypeStruct((b, ln, dil * 128), F32)],
        compiler_params=_params("parallel", "parallel", "arbitrary"),
        name=f"dil_attn_d{dil}",
    )(qv, kvv, kvv, bias_tiles)


def _sample_attn_kernel(q_ref, kn0_ref, kn1_ref, kn2_ref, c0_ref, c1_ref, c2_ref, b0_ref, b1_ref, b2_ref,
                        bnew_ref, o_ref, *, n_new):
    q = (q_ref[...] * SCALE).astype(BF16)
    qf = q.astype(F32)
    groups = ((kn0_ref, c0_ref, b0_ref), (kn1_ref, c1_ref, b1_ref), (kn2_ref, c2_ref, b2_ref))
    new_rows = []
    for g, (kn_ref, _, _) in enumerate(groups):
        kn = kn_ref[...]
        shifted = [kn] + [pltpu.roll(kn, s, axis=0) for s in range(1, n_new if g == 0 else 1)]
        new_rows.append([z.astype(BF16).astype(F32) for z in shifted])
    chains = [(h, g) for h in range(GROUP_HEADS) for g in range(N_GROUPS)]
    ids = range(len(chains))
    ks = lambda h: slice(h * HEAD_DIM, (h + 1) * HEAD_DIM)
    vs = lambda h: slice(GROUP_W + h * HEAD_DIM, GROUP_W + (h + 1) * HEAD_DIM)
    qs = lambda h, g: slice(g * GROUP_W + h * HEAD_DIM, g * GROUP_W + (h + 1) * HEAD_DIM)
    lp = [_dot(q[:, qs(h, g)], groups[g][1][0, h].astype(BF16)) + groups[g][2][h] for h, g in chains]
    lns = [[jnp.sum(qf[:, qs(h, g)] * rows[:, ks(h)], axis=-1, keepdims=True) + bnew_ref[g, s][:, h:h + 1]
            for s, rows in enumerate(new_rows[g])] for h, g in chains]
    m = [functools.reduce(jnp.maximum, [jnp.max(lp[c], axis=-1, keepdims=True)] + lns[c]) for c in ids]
    pp = [jnp.exp(lp[c] - m[c]) for c in ids]
    pn = [[jnp.exp(ln - m[c]) for ln in lns[c]] for c in ids]
    den = [jnp.sum(pp[c], axis=-1, keepdims=True) + sum(pn[c]) for c in ids]
    o = [_dot_nt(pp[c].astype(BF16), groups[g][1][1, h].astype(BF16)) for c, (h, g) in enumerate(chains)]
    o = [o[c] + sum(z.astype(BF16).astype(F32) * rows[:, vs(h)] for z, rows in zip(pn[c], new_rows[g]))
         for c, (h, g) in enumerate(chains)]
    for h in range(GROUP_HEADS):
        cs = [h * N_GROUPS + g for g in range(N_GROUPS)]
        mx = functools.reduce(jnp.maximum, [m[c] for c in cs])
        wts = [jnp.exp(m[c] - mx) * den[c] for c in cs]
        num = sum(w * (o[c] / den[c]) for w, c in zip(wts, cs))
        o_ref[:, ks(h)] = num / sum(wts)


def _sample_attn(q, kvn, caches, bias_past, bias_new, n_new):
    b, tp, _ = q.shape
    new_spec = pl.BlockSpec((None, tp, 2 * GROUP_W), lambda bi: (bi, 0, 0))
    cache_spec = lambda w: pl.BlockSpec((None, 2, GROUP_HEADS, HEAD_DIM, w), lambda bi: (bi, 0, 0, 0, 0))
    bias_spec = lambda w: pl.BlockSpec((GROUP_HEADS, tp, w), lambda bi: (0, 0, 0))
    return pl.pallas_call(
        functools.partial(_sample_attn_kernel, n_new=n_new),
        grid=(b,),
        in_specs=[pl.BlockSpec((None, tp, A_Q), lambda bi: (bi, 0, 0)), new_spec, new_spec, new_spec]
                 + [cache_spec(w) for w in WINDOWS] + [bias_spec(w) for w in WINDOWS]
                 + [pl.BlockSpec((N_GROUPS, n_new, tp, 128), lambda bi: (0, 0, 0, 0))],
        out_specs=pl.BlockSpec((None, tp, GROUP_W), lambda bi: (bi, 0, 0)),
        out_shape=jax.ShapeDtypeStruct((b, tp, GROUP_W), F32),
        compiler_params=_params("parallel"),
        name="sample_attn",
    )(q, kvn[0], kvn[1], kvn[2], *caches, *bias_past, bias_new)


def _cross_attn(q_ref, kvt_ref, dst_ref, nb):
    tq = q_ref.shape[0] // nb
    sub = min(tq, Q_TILE)
    chains = [(bi, slice(r0, r0 + sub), h) for bi in range(nb) for r0 in range(bi * tq, (bi + 1) * tq, sub)
              for h in range(MEM_W // HEAD_DIM)]
    kvt = [kvt_ref[bi].astype(BF16) for bi in range(nb)]
    pair = lambda h, off=0: slice(off + (h // 2) * 128, off + (h // 2 + 1) * 128)
    lo_lane = lax.broadcasted_iota(jnp.int32, (sub, 128), 1) < HEAD_DIM
    qm = [jnp.where(lo_lane if h % 2 == 0 else jnp.logical_not(lo_lane), q_ref[rw, pair(h)] * SCALE, 0.0).astype(BF16)
          for bi, rw, h in chains]
    lg = [_dot(qm[c], kvt[bi][pair(h), :]) for c, (bi, rw, h) in enumerate(chains)]
    p = [jnp.exp(z - jnp.max(z, axis=-1, keepdims=True)) for z in lg]
    den = [jnp.sum(z, axis=-1, keepdims=True) for z in p]
    o = [_dot_nt(p[c].astype(BF16), kvt[bi][pair(h, MEM_W), :]) / den[c] for c, (bi, rw, h) in enumerate(chains)]
    for c, (bi, rw, h) in enumerate(chains):
        if h % 2 == 1:
            dst_ref[rw, pair(h)] = jnp.where(lo_lane, o[c - 1], o[c])


def _rwkv_kernel(cols_ref, shift_ref, s0_ref, mu_ref, w0_ref, wup_ref, a0_ref, aup_ref, kk_ref, ka_ref, rk_ref,
                 lnw_ref, lnb_ref, y_ref, sout_ref, shout_ref,
                 carry, state, r_s, k_s, v_s, a_s, b_s, lw_s, y_s, *, tt, chunk, t_valid):
    j = pl.program_id(1)
    nb, t_in = cols_ref.shape[0], cols_ref.shape[1]
    zero_head = jnp.zeros((HEAD_DIM, HEAD_DIM), F32)

    @pl.when(j == 0)
    def _():
        carry[...] = shift_ref[...]
        for bi in range(nb):
            for p in range(RWKV_PAIRS):
                state[bi * RWKV_PAIRS + p] = jnp.concatenate(
                    [jnp.concatenate([s0_ref[bi, 2 * p], zero_head], axis=1),
                     jnp.concatenate([zero_head, s0_ref[bi, 2 * p + 1]], axis=1)], axis=0)

    direct = nb == 1 and t_in == tt
    y_dst = y_ref.at[0] if direct else y_s
    row = lax.broadcasted_iota(jnp.int32, (tt, 1), 0)
    live = row < t_valid
    blockdiag = (lax.broadcasted_iota(jnp.int32, (128, 128), 0) // HEAD_DIM
                 == lax.broadcasted_iota(jnp.int32, (128, 128), 1) // HEAD_DIM)

    def head_sum(x):
        lo = lax.broadcasted_iota(jnp.int32, x.shape, 1) < HEAD_DIM
        s0 = jnp.sum(jnp.where(lo, x, 0.0), axis=-1, keepdims=True)
        s1 = jnp.sum(jnp.where(lo, 0.0, x), axis=-1, keepdims=True)
        return jnp.where(lo, s0, s1)

    for bi in range(nb):
        base = slice(bi * tt, (bi + 1) * tt)
        cols = cols_ref[bi]
        if t_in < tt:
            cols = jnp.concatenate([cols, jnp.zeros((tt - t_in, cols.shape[1]), F32)], axis=0)
        prev = jnp.where(row == 0, carry[bi], pltpu.roll(cols, 1, axis=0))
        carry[bi] = cols[t_valid - 1:t_valid, :]
        xs = cols + mu_ref[...] * (prev - cols)
        r = xs[:, :RWKV_W]
        k = xs[:, RWKV_W:2 * RWKV_W]
        v = xs[:, 2 * RWKV_W:3 * RWKV_W]
        wd = xs[:, 3 * RWKV_W:3 * RWKV_W + LORA]
        ad = xs[:, 3 * RWKV_W + LORA:]
        z = w0_ref[...] + _dot(jnp.tanh(wd).astype(BF16), wup_ref[...])
        log_decay = -math.exp(-0.5) * jax.nn.sigmoid(z)
        a = jax.nn.sigmoid(a0_ref[...] + _dot(ad.astype(BF16), aup_ref[...]))
        kk = k * kk_ref[...]
        kk_sq = kk * kk
        k2 = k * (1.0 + (a - 1.0) * ka_ref[...])
        for p in range(RWKV_PAIRS):
            ps = slice(p * 128, (p + 1) * 128)
            nrm = jnp.maximum(jnp.sqrt(head_sum(kk_sq[:, ps])), 1e-12)
            kkn = kk[:, ps] / nrm
            if t_valid < tt:
                zero = jnp.zeros((tt, 128), F32)
                r_s[base, ps] = r[:, ps]
                k_s[base, ps] = jnp.where(live, k2[:, ps], zero)
                v_s[base, ps] = jnp.where(live, v[:, ps], zero)
                a_s[base, ps] = jnp.where(live, -kkn, zero)
                b_s[base, ps] = jnp.where(live, kkn * a[:, ps], zero)
                lw_s[base, ps] = jnp.where(live, log_decay[:, ps], zero)
            else:
                r_s[base, ps] = r[:, ps]
                k_s[base, ps] = k2[:, ps]
                v_s[base, ps] = v[:, ps]
                a_s[base, ps] = -kkn
                b_s[base, ps] = kkn * a[:, ps]
                lw_s[base, ps] = log_decay[:, ps]

    ci = lax.broadcasted_iota(jnp.int32, (chunk, chunk), 0)
    cj = lax.broadcasted_iota(jnp.int32, (chunk, chunk), 1)
    tri_incl = (ci >= cj).astype(BF16)
    lo_lane = lax.broadcasted_iota(jnp.int32, (chunk, 128), 1) < HEAD_DIM
    levels = max(1, math.ceil(math.log2(min(chunk, t_valid))))
    n2 = 2 * chunk
    ri = lax.broadcasted_iota(jnp.int32, (n2, n2), 0)
    rj = lax.broadcasted_iota(jnp.int32, (n2, n2), 1)
    same = ri // chunk == rj // chunk
    strict = jnp.logical_and(same, ri > rj)
    eye = (ri == rj).astype(F32)
    incl = jnp.logical_and(same, ri >= rj)
    own = lax.broadcasted_iota(jnp.int32, (n2, 128), 0) // chunk == lax.broadcasted_iota(jnp.int32, (n2, 128), 1) // HEAD_DIM
    dup = lambda z: jnp.concatenate([z, z], axis=0)
    cat = jnp.concatenate

    n_chunks = tt // chunk
    group = SCAN_INTERLEAVE if n_chunks % SCAN_INTERLEAVE == 0 else 1

    def chunk_body(ci, _):
        rows = [pl.ds(bi * tt + (ci * group + cc) * chunk, chunk) for cc in range(group) for bi in range(nb)]
        sl = [slice(p * 128, (p + 1) * 128) for p in range(RWKV_PAIRS)]
        chains = [(rw, s) for rw in rows for s in sl]
        ids = range(len(chains))
        per_chunk = nb * RWKV_PAIRS
        cum_all = [_mm(tri_incl, lw_s[rw, :], pb=3) for rw in rows]
        cum = [z[:, s] for z in cum_all for s in sl]
        p_incl = [jnp.exp(z) for z in cum]
        p_inv = [jnp.exp(-z) for z in cum]
        rr = [r_s[rw, s] for rw, s in chains]
        kc = [k_s[rw, s] for rw, s in chains]
        vc = [v_s[rw, s] for rw, s in chains]
        at2 = [jnp.where(own, dup(a_s[rw, s] * jnp.exp(cum[c] - lw_s[rw, s])), 0.0) for c, (rw, s) in enumerate(chains)]
        rt2 = [jnp.where(own, dup(rr[c] * p_incl[c]), 0.0) for c in ids]
        bt = [(b_s[rw, s] * p_inv[c]).astype(BF16) for c, (rw, s) in enumerate(chains)]
        kt = [(kc[c] * p_inv[c]).astype(BF16) for c in ids]
        vb = [z.astype(BF16) for z in vc]
        v2 = [dup(z) for z in vb]
        g = [_mm(cat([at2[c], rt2[c]], 0).astype(BF16), cat([dup(bt[c]), dup(kt[c])], 0), NT) for c in ids]
        a_ak = [jnp.where(strict, z[:n2, n2:], 0.0).astype(BF16) for z in g]
        apow = [jnp.where(strict, z[:n2, :n2], 0.0).astype(BF16) for z in g]
        a_r = [cat([jnp.where(incl, z[n2:, :n2], 0.0), jnp.where(incl, z[n2:, n2:], 0.0)], 1).astype(BF16)
               for z in g]
        akv = [_mm(a_ak[c], v2[c]) for c in ids]
        tinv = [eye + jnp.where(strict, z[:n2, :n2], 0.0) for z in g]
        for _ in range(levels - 1):
            apow = [_mm(z, z).astype(BF16) for z in apow]
            tinv = [tinv[c] + _mm(tinv[c].astype(BF16), apow[c]) for c in ids]
        sol = [_mm(tinv[c].astype(BF16), cat([at2[c], akv[c]], axis=1).astype(BF16)) for c in ids]
        ws = [z[:, :128] for z in sol]
        u0s = [jnp.where(own, z[:, 128:], 0.0) for z in sol]
        zeros2 = jnp.zeros((n2, 128), BF16)
        qy = [_mm(a_r[c], cat([cat([ws[c], u0s[c]], 1).astype(BF16), cat([zeros2, v2[c]], 1)], 0)) for c in ids]
        qs = [(rt2[c] + qy[c][:, :128]).astype(BF16) for c in ids]
        zeros1 = jnp.zeros((chunk, 128), BF16)
        lhs = [cat([cat([ws[c][:chunk] + ws[c][chunk:], u0s[c][:chunk] + u0s[c][chunk:]], 1).astype(BF16),
                    cat([zeros1, vb[c]], 1)], 0) for c in ids]
        wbn = [_mm(lhs[c], cat([bt[c], kt[c]], 0), TN) for c in ids]
        wb = [z[:128].astype(BF16) for z in wbn]
        s_cur = [state[q] for q in range(per_chunk)]
        ys = []
        for cc in range(group):
            base = cc * per_chunk
            s_b = [z.astype(BF16) for z in s_cur]
            ys += [_mm(qs[base + q], s_b[q], NT) + qy[base + q][:, 128:] for q in range(per_chunk)]
            sw = [_mm(s_b[q], wb[base + q]) for q in range(per_chunk)]
            s_cur = [jnp.where(blockdiag, (s_cur[q] + sw[q] + wbn[base + q][128:])
                               * p_incl[base + q][chunk - 1:chunk, :], 0.0) for q in range(per_chunk)]
        for q in range(per_chunk):
            state[q] = s_cur[q]
        y = [jnp.where(lo_lane, z[:chunk], z[chunk:]) for z in ys]

        bonus = [head_sum(rr[c] * kc[c] * rk_ref[:, s]) for c, (_, s) in enumerate(chains)]
        dlt = [y[c] - head_sum(y[c]) * (1.0 / HEAD_DIM) for c in ids]
        var = [head_sum(z * z) * (1.0 / HEAD_DIM) for z in dlt]
        for c, (rw, s) in enumerate(chains):
            yn = dlt[c] * lax.rsqrt(var[c] + GN_EPS) * lnw_ref[:, s] + lnb_ref[:, s]
            y_dst[rw, s] = yn + bonus[c] * vc[c]
        return 0

    for ci in range(n_chunks // group):
        chunk_body(ci, 0)
    if not direct:
        for bi in range(nb):
            y_ref[bi] = y_s[bi * tt:bi * tt + t_in, :]

    @pl.when(j == pl.num_programs(1) - 1)
    def _():
        for bi in range(nb):
            for p in range(RWKV_PAIRS):
                sout_ref[bi, 2 * p] = state[bi * RWKV_PAIRS + p, 0:HEAD_DIM, 0:HEAD_DIM]
                sout_ref[bi, 2 * p + 1] = state[bi * RWKV_PAIRS + p, HEAD_DIM:, HEAD_DIM:]
        shout_ref[...] = carry[...]


def _rwkv_scan(cols, shift_prev, s0, prm, tt, t_valid, nb):
    b, t, _ = cols.shape
    t_blk = min(t, tt)
    heads = pl.BlockSpec((nb, 2 * RWKV_PAIRS, HEAD_DIM, HEAD_DIM), lambda bi, j: (bi, 0, 0, 0))
    row = lambda n: pl.BlockSpec((1, n), lambda bi, j: (0, 0))
    vec = lambda a: a.reshape(1, -1)
    y, s_out, sh_out = pl.pallas_call(
        functools.partial(_rwkv_kernel, tt=tt, chunk=SCAN_CHUNK, t_valid=min(t_valid, tt)),
        grid=(b // nb, t // t_blk),
        in_specs=[pl.BlockSpec((nb, t_blk, C_SHIFT), lambda bi, j: (bi, j, 0)),
                  pl.BlockSpec((nb, 1, C_SHIFT), lambda bi, j: (bi, 0, 0)),
                  heads,
                  row(C_SHIFT), row(RWKV_W),
                  pl.BlockSpec((LORA, RWKV_W), lambda bi, j: (0, 0)),
                  row(RWKV_W),
                  pl.BlockSpec((LORA, RWKV_W), lambda bi, j: (0, 0)),
                  row(RWKV_W), row(RWKV_W), row(RWKV_W), row(RWKV_W), row(RWKV_W)],
        out_specs=[pl.BlockSpec((nb, t_blk, RWKV_W), lambda bi, j: (bi, j, 0)),
                   heads,
                   pl.BlockSpec((nb, 1, C_SHIFT), lambda bi, j: (bi, 0, 0))],
        out_shape=[jax.ShapeDtypeStruct((b, t, RWKV_W), F32),
                   jax.ShapeDtypeStruct((b, 2 * RWKV_PAIRS, HEAD_DIM, HEAD_DIM), F32),
                   jax.ShapeDtypeStruct((b, 1, C_SHIFT), F32)],
        scratch_shapes=[pltpu.VMEM((nb, 1, C_SHIFT), F32), pltpu.VMEM((nb * RWKV_PAIRS, 128, 128), F32)]
                       + [pltpu.VMEM((nb * tt, RWKV_W), F32)] * 7,
        compiler_params=_params("parallel", "arbitrary"),
        name="rwkv_scan",
    )(cols, shift_prev.reshape(b, 1, C_SHIFT), s0, vec(prm["mu"]), vec(prm["w0"]), prm["w_up"].astype(BF16),
      vec(prm["a0"]), prm["a_up"].astype(BF16), vec(prm["k_k"]), vec(prm["k_a"]), vec(prm["r_k"]),
      vec(prm["ln_w"]), vec(prm["ln_b"]))
    return y, s_out, sh_out.reshape(b, C_SHIFT)


def _post_kernel(*refs, merge_dils, mix_w, nb):
    x_ref, qmem_ref, kvt_ref, gate_ref, w_ref, g_ref = refs[:6]
    mix_refs = refs[6:-3]
    y_ref, stage, omem = refs[-3:]
    _cross_attn(qmem_ref, kvt_ref, omem, nb)
    if merge_dils:
        er = lax.broadcasted_iota(jnp.int32, (128, 2 * GROUP_W), 0)
        ec = lax.broadcasted_iota(jnp.int32, (128, 2 * GROUP_W), 1)
        expand = (er == ec // HEAD_DIM).astype(BF16)
        ms, ss, os_ = [], [], []
        for g, dil in enumerate(merge_dils):
            acc_ref, st_ref = mix_refs[2 * g], mix_refs[2 * g + 1]
            acc = acc_ref[...] if dil == 1 else _from_classes(acc_ref, stage, dil, GROUP_W)
            stats = st_ref[...] if dil == 1 else _from_classes(st_ref, stage, dil, 128)
            st = _mm(stats, expand, pa=3)
            ms.append(st[:, :GROUP_W])
            ss.append(st[:, GROUP_W:])
            os_.append(acc / st[:, GROUP_W:])
        mx = functools.reduce(jnp.maximum, ms)
        wts = [jnp.exp(m - mx) * s for m, s in zip(ms, ss)]
        mix = sum(w * o for w, o in zip(wts, os_)) / sum(wts)
    else:
        mix = mix_refs[0][...]
    gate = gate_ref[...]
    act = gate * jax.nn.sigmoid(gate)
    h1 = (mix * act[:, :mix_w]).astype(BF16)
    h2 = (omem[...] * act[:, mix_w:]).astype(BF16)
    out = _dot(h1, w_ref[:mix_w, :]) + _dot(h2, w_ref[mix_w:, :])
    y_ref[...] = x_ref[...] + _rms(out, g_ref[...])


def _post(x, mix, q_mem, kvt, layer, gate, w_out, g_post, tm, rows_per_batch):
    m, d = x.shape
    merge = isinstance(mix, (list, tuple))
    gw = gate.shape[1]
    rows = lambda wd: pl.BlockSpec((tm, wd), lambda i: (i, 0))
    nb = max(1, tm // rows_per_batch)
    per_batch = max(1, rows_per_batch // tm)
    kvt_spec = pl.BlockSpec((None, nb, 2 * MEM_W, N_MEM), lambda i: (layer, i // per_batch, 0, 0))
    if merge:
        mix_w = GROUP_W
        merge_dils = tuple(acc.shape[2] // GROUP_W for acc, _ in mix)
        tiles = mix[0][0].shape[1] * merge_dils[0] // tm
        mix_arrays = [a for pair in mix for a in pair]
        mix_specs = [pl.BlockSpec((None, tm // dil, a.shape[2]), lambda i: (i // tiles, i % tiles, 0))
                     for dil, pair in zip(merge_dils, mix) for a in pair]
    else:
        mix_w, merge_dils, mix_arrays, mix_specs = mix.shape[1], (), [mix], [rows(mix.shape[1])]
    return pl.pallas_call(
        functools.partial(_post_kernel, merge_dils=merge_dils, mix_w=mix_w, nb=nb),
        grid=(m // tm,),
        in_specs=[rows(d), rows(MEM_W), kvt_spec, rows(gw),
                  pl.BlockSpec((gw, d), lambda i: (0, 0)),
                  pl.BlockSpec((1, d), lambda i: (0, 0))] + mix_specs,
        out_specs=rows(d),
        out_shape=jax.ShapeDtypeStruct((m, d), F32),
        scratch_shapes=[pltpu.VMEM((tm, 128), F32), pltpu.VMEM((tm, MEM_W), F32)],
        compiler_params=_params("parallel"),
        name="post",
    )(x, q_mem, kvt, gate, w_out.astype(BF16), g_post.reshape(1, d), *mix_arrays)


def _t5_bucket(dist):
    max_exact = N_BUCKETS // 2
    d = jnp.maximum(dist, 1).astype(F32)
    large = max_exact + (jnp.log(d / max_exact) / math.log(MAX_DISTANCE / max_exact)
                         * (N_BUCKETS - max_exact)).astype(jnp.int32)
    large = jnp.minimum(large, N_BUCKETS - 1)
    return jnp.where(dist < max_exact, dist, large)


def _group_bias(rel_bias, g):
    dist = DILATIONS[g] * jnp.arange(A_KEYS, dtype=jnp.int32)
    bias = rel_bias[_t5_bucket(dist)]
    return bias[:, g * GROUP_HEADS:(g + 1) * GROUP_HEADS].T.astype(F32)


def _bias_tiles_kernel(x_ref, o_ref):
    row = lax.broadcasted_iota(jnp.int32, (Q_TILE, 2 * Q_TILE), 0)
    for h in range(o_ref.shape[0]):
        t = jnp.broadcast_to(x_ref[h:h + 1, :], (Q_TILE, 2 * Q_TILE))
        for bit in range(int(math.log2(Q_TILE))):
            t = jnp.where((row >> bit) & 1 == 1, pltpu.roll(t, 1 << bit, axis=1), t)
        o_ref[h] = t


def _prompt_bias(biases):
    bias = jnp.concatenate(biases, axis=0)
    neg = jnp.full((bias.shape[0], Q_TILE - 1), NEG_INF, F32)
    table = jnp.concatenate([bias[:, :1], neg, bias[:, :0:-1]], axis=1)
    n = table.shape[0]
    return pl.pallas_call(
        _bias_tiles_kernel,
        out_shape=jax.ShapeDtypeStruct((n, Q_TILE, 2 * Q_TILE), F32),
        name="bias_tiles",
    )(table)


def _sample_bias(biases, n_new, tp):
    t = np.arange(tp)[:, None]
    real = t < n_new
    past = []
    for g, bias in enumerate(biases):
        w, dil = WINDOWS[g], DILATIONS[g]
        spread = jnp.concatenate([bias[:, :0:-1, None], jnp.full((GROUP_HEADS, w // dil, dil - 1), NEG_INF, F32)],
                                 axis=-1).reshape(GROUP_HEADS, w)
        rows = [jnp.concatenate([jnp.full((GROUP_HEADS, tt), NEG_INF, F32), spread[:, :w - tt]], axis=-1)
                for tt in range(n_new)] + [jnp.zeros((GROUP_HEADS, w), F32)] * (tp - n_new)
        past.append(jnp.stack(rows, axis=1))
    s = np.arange(n_new)[None, :, None]
    g_idx = np.arange(N_GROUPS)[:, None, None]
    ok = (s <= t[None, :, 0]) & real[None, :, 0] & ((g_idx == 0) | (s == 0))
    vals = jnp.stack([bias[:, :n_new].T for bias in biases])
    new = jnp.where(jnp.asarray(ok)[..., None], vals[:, :, None, :], NEG_INF)
    return past, jnp.pad(new, ((0, 0), (0, 0), (0, 0), (0, 128 - GROUP_HEADS)))


def _pieces_a():
    q = [((g * GROUP_W, (g + 1) * GROUP_W),) for g in range(N_GROUPS)]
    kv = [((A_Q + g * GROUP_W, A_Q + (g + 1) * GROUP_W), (2 * A_Q + g * GROUP_W, 2 * A_Q + (g + 1) * GROUP_W))
          for g in range(N_GROUPS)]
    qmem = ((3 * A_Q, 3 * A_Q + MEM_W),)
    gate = ((3 * A_Q + MEM_W, 3 * A_Q + MEM_W + GROUP_W + MEM_W),)
    return tuple(q + kv + [qmem, gate])


def _pieces_b():
    return (((0, C_SHIFT),), ((C_SHIFT, C_SHIFT + MEM_W),), ((C_SHIFT + MEM_W, C_SHIFT + MEM_W + RWKV_W + MEM_W),))


def kernel(x_prompt, x_sample, mem_prompt, cache_mem_kv, cache_win0, cache_win1, cache_win2, state_wkv, state_shift, norm_pre, norm_post, norm_mem, w_mem_kv, rel_bias, w_in_a, w_out_a, w_in_b, w_out_b, rwkv_mu, rwkv_w0, rwkv_w_up, rwkv_a0, rwkv_a_up, rwkv_k_k, rwkv_k_a, rwkv_r_k, rwkv_ln_w, rwkv_ln_b):
    bp, tp, d = x_prompt.shape
    bs, ts, _ = x_sample.shape
    tsp = SAMPLE_PAD_T
    xp = x_prompt.reshape(bp * tp, d)
    xs = jnp.pad(x_sample, ((0, 0), (0, tsp - ts), (0, 0))).reshape(bs * tsp, d)
    tm_p, tm_s = 512, bs * tsp
    tm_post_s = 8 * tsp
    time_minor = lambda c: jnp.moveaxis(c, -4, -1)
    caches = [time_minor(c[0]) for c in (cache_win0, cache_win1, cache_win2)]
    mkv_s = time_minor(cache_mem_kv).reshape(cache_mem_kv.shape[0], bs, 2 * MEM_W, N_MEM)
    mkv_p = _mem_kv(mem_prompt, norm_mem, w_mem_kv)

    biases = [_group_bias(rel_bias, g) for g in range(N_GROUPS)]
    outs_p = _norm_proj(xp, norm_pre[0], w_in_a[0], _pieces_a(), tm_p, transposed=(3, 4, 5),
                        dils=DILATIONS + DILATIONS + (1, 1), rows_per_batch=tp)
    outs_s = _norm_proj(xs, norm_pre[0], w_in_a[0], _pieces_a(), tm_s)
    q_p, kv_p, qmem_p, gate_p, kvt_p = outs_p[0:3], outs_p[3:6], outs_p[6], outs_p[7], outs_p[8:11]
    q_s, kv_s, qmem_s, gate_s = outs_s[0:3], outs_s[3:6], outs_s[6], outs_s[7]

    bias_tiles = _prompt_bias(biases)
    merged = [_dil_attn(q_p[g].reshape(bp, tp // DILATIONS[g], -1), kv_p[g].reshape(bp, tp // DILATIONS[g], -1),
                        bias_tiles, g) for g in range(N_GROUPS)]
    xp1 = _post(xp, merged, qmem_p, mkv_p, 0, gate_p, w_out_a[0], norm_post[0], tm_p, tp)

    bias_past, bias_new = _sample_bias(biases, ts, tsp)
    q_s_all = jnp.concatenate(q_s, axis=-1).reshape(bs, tsp, A_Q)
    kvn = [a.reshape(bs, tsp, 2 * GROUP_W) for a in kv_s]
    o_s = _sample_attn(q_s_all, kvn, caches, bias_past, bias_new, ts)
    xs1 = _post(xs, o_s.reshape(bs * tsp, GROUP_W), qmem_s, mkv_s, 0, gate_s, w_out_a[0], norm_post[0],
                tm_post_s, tsp)

    cols_p, qmem_p, gate_p = _norm_proj(xp1, norm_pre[1], w_in_b[0], _pieces_b(), tm_p)
    cols_s, qmem_s, gate_s = _norm_proj(xs1, norm_pre[1], w_in_b[0], _pieces_b(), tm_s)
    prm = dict(mu=rwkv_mu[0], w0=rwkv_w0[0], w_up=rwkv_w_up[0], a0=rwkv_a0[0], a_up=rwkv_a_up[0], k_k=rwkv_k_k[0],
               k_a=rwkv_k_a[0], r_k=rwkv_r_k[0], ln_w=rwkv_ln_w[0], ln_b=rwkv_ln_b[0])
    y_p, wkv_p, sh_p = _rwkv_scan(cols_p.reshape(bp, tp, C_SHIFT), jnp.zeros((bp, C_SHIFT), F32),
                                  jnp.zeros((bp, 2 * RWKV_PAIRS, HEAD_DIM, HEAD_DIM), F32), prm, 512, tp, 1)
    y_s, wkv_s, sh_s = _rwkv_scan(cols_s.reshape(bs, tsp, C_SHIFT), state_shift[0], state_wkv[0], prm, SCAN_CHUNK,
                                  ts, 4)
    y_s = y_s.reshape(bs * tsp, RWKV_W)
    xp2 = _post(xp1, y_p.reshape(bp * tp, RWKV_W), qmem_p, mkv_p, 1, gate_p, w_out_b[0], norm_post[1], tm_p, tp)
    xs2 = _post(xs1, y_s, qmem_s, mkv_s, 1, gate_s, w_out_b[0], norm_post[1], tm_post_s, tsp)

    kv_shape = (2, GROUP_HEADS, HEAD_DIM)
    time_major = lambda c: jnp.moveaxis(c, -1, -4)
    new_mem_kv = time_major(mkv_p.reshape(mkv_p.shape[0], bp, 2, MEM_W // HEAD_DIM, HEAD_DIM, N_MEM))
    win_p = [time_major(kvt_p[g].reshape(bp, *kv_shape, tp)[..., tp - min(WINDOWS[g], tp):])[None]
             for g in range(N_GROUPS)]
    win_s = [kv_s[g].reshape(bs, tsp, *kv_shape)[None, :, :ts] for g in range(N_GROUPS)]
    return (xp2.reshape(bp, tp, d), xs2.reshape(bs, tsp, d)[:, :ts], new_mem_kv,
            win_p[0], win_p[1], win_p[2], win_s[0], win_s[1], win_s[2],
            wkv_p[None], wkv_s[None], sh_p[None], sh_s[None])
```

```python
import functools
import math

import numpy as np
import jax
import jax.numpy as jnp
from jax import lax
from jax.experimental import pallas as pl
from jax.experimental.pallas import tpu as pltpu

F32 = jnp.float32
BF16 = jnp.bfloat16

D_MODEL = 1024
HEAD_DIM = 64
N_GROUPS = 3
GROUP_HEADS = 4
WINDOWS = (128, 512, 2048)
DILATIONS = (1, 4, 16)
A_HEADS = N_GROUPS * GROUP_HEADS
A_KEYS = 129
GROUP_W = GROUP_HEADS * HEAD_DIM
A_Q = A_HEADS * HEAD_DIM
N_MEM = 256
MEM_W = 256
RWKV_W = 768
RWKV_PAIRS = RWKV_W // 128
LORA = 64
C_SHIFT = 3 * RWKV_W + 2 * LORA
N_BUCKETS = 32
MAX_DISTANCE = WINDOWS[-1]
RMS_EPS = 1e-6
GN_EPS = 64e-5
NEG_INF = -1e30
SCALE = HEAD_DIM ** -0.5

Q_TILE = 128
DIL_SEGMENTS = 4
SCAN_CHUNK = 64
POST_PARTS = 2
SCAN_INTERLEAVE = 2
SAMPLE_PAD_T = 8
VMEM_LIMIT = 56 * 1024 * 1024
NN =(((1,), (0,)), ((), ()))
NT = (((1,), (1,)), ((), ()))
TN = (((0,), (0,)), ((), ()))


def _params(*sem):
    return pltpu.CompilerParams(dimension_semantics=sem, vmem_limit_bytes=VMEM_LIMIT)


def _dot(a, b):
    return jnp.dot(a, b, preferred_element_type=F32)


def _dot_nt(a, b):
    return lax.dot_general(a, b, NT, preferred_element_type=F32)


def _split(x, n):
    if x.dtype == BF16:
        return [x]
    parts, rem = [], x
    for i in range(n):
        parts.append(rem.astype(BF16))
        if i + 1 < n:
            rem = rem - parts[-1].astype(F32)
    return parts


def _mm(a, b, dims=NN, pa=1, pb=1):
    pas, pbs = _split(a, pa), _split(b, pb)
    order = max(len(pas), len(pbs))
    out = None
    for i, ai in enumerate(pas):
        for j, bj in enumerate(pbs):
            if i + j < order:
                term = lax.dot_general(ai, bj, dims, preferred_element_type=F32)
                out = term if out is None else out + term
    return out


def _rms(x, g):
    return x * lax.rsqrt(jnp.mean(x * x, axis=-1, keepdims=True) + RMS_EPS) * g


def _to_classes(dst_ref, stage, val, dil, width, off):
    tm = val.shape[0]
    for j in range(val.shape[1] // 128):
        stage[...] = val[:, 128 * j:128 * (j + 1)]
        for r in range(dil):
            lane0 = r * width + off + 128 * j
            dst_ref[:, lane0:lane0 + 128] = stage[pl.ds(r, tm // dil, stride=dil), :]


def _from_classes(src_ref, stage, dil, width):
    tm = stage.shape[0]
    chunks = []
    for j in range(width // 128):
        for r in range(dil):
            lane0 = r * width + 128 * j
            stage[pl.ds(r, tm // dil, stride=dil), :] = src_ref[:, lane0:lane0 + 128]
        chunks.append(stage[...])
    return jnp.concatenate(chunks, axis=1)


def _norm_proj_kernel(x_ref, g_ref, w_ref, *refs, pieces, transposed, dils):
    stage = refs[-1]
    out_refs = refs[:-1]
    xn = _rms(x_ref[...], g_ref[...]).astype(BF16)
    t_refs = iter(out_refs[len(pieces):])
    for idx, (out_ref, cols) in enumerate(zip(out_refs, pieces)):
        t_ref = next(t_refs) if idx in transposed else None
        width = sum(c1 - c0 for c0, c1 in cols)
        off = 0
        for c0, c1 in cols:
            res = _dot(xn, w_ref[:, c0:c1])
            if dils[idx] > 1:
                _to_classes(out_ref, stage, res, dils[idx], width, off)
            else:
                out_ref[:, off:off + c1 - c0] = res
            if t_ref is not None:
                t_ref[off:off + c1 - c0, :] = res.T
            off += c1 - c0


def _norm_proj(x, g, w, pieces, tm, transposed=(), dils=None, rows_per_batch=None):
    m, d = x.shape
    n = w.shape[1]
    dils = tuple(dils) if dils else (1,) * len(pieces)
    widths = [sum(c1 - c0 for c0, c1 in cols) for cols in pieces]
    tiles = rows_per_batch // tm if rows_per_batch else None
    out_specs, out_shape = [], []
    for wd, dil in zip(widths, dils):
        if dil > 1:
            out_specs.append(pl.BlockSpec((None, tm // dil, dil * wd), lambda i: (i // tiles, i % tiles, 0)))
            out_shape.append(jax.ShapeDtypeStruct((m // rows_per_batch, rows_per_batch // dil, dil * wd), F32))
        else:
            out_specs.append(pl.BlockSpec((tm, wd), lambda i: (i, 0)))
            out_shape.append(jax.ShapeDtypeStruct((m, wd), F32))
    out_specs += [pl.BlockSpec((None, widths[idx], tm), lambda i: (i // tiles, 0, i % tiles)) for idx in transposed]
    out_shape += [jax.ShapeDtypeStruct((m // rows_per_batch, widths[idx], rows_per_batch), F32) for idx in transposed]
    return pl.pallas_call(
        functools.partial(_norm_proj_kernel, pieces=pieces, transposed=tuple(transposed), dils=dils),
        grid=(m // tm,),
        in_specs=[pl.BlockSpec((tm, d), lambda i: (i, 0)),
                  pl.BlockSpec((1, d), lambda i: (0, 0)),
                  pl.BlockSpec((d, n), lambda i: (0, 0))],
        out_specs=out_specs,
        out_shape=out_shape,
        scratch_shapes=[pltpu.VMEM((tm, 128), F32)],
        compiler_params=_params("parallel"),
        name="norm_proj",
    )(x, g.reshape(1, d), w.astype(BF16))


def _mem_kv_kernel(x_ref, g_ref, wt_ref, o_ref):
    xn = _rms(x_ref[...], g_ref[...]).astype(BF16)
    o_ref[...] = _dot_nt(wt_ref[...], xn)


def _mem_kv(mem, g, w):
    b, n, d = mem.shape
    nl, _, wd = w.shape
    return pl.pallas_call(
        _mem_kv_kernel,
        grid=(nl, b),
        in_specs=[pl.BlockSpec((None, n, d), lambda l, bi: (bi, 0, 0)),
                  pl.BlockSpec((None, 1, d), lambda l, bi: (l, 0, 0)),
                  pl.BlockSpec((None, wd, d), lambda l, bi: (l, 0, 0))],
        out_specs=pl.BlockSpec((None, None, wd, n), lambda l, bi: (l, bi, 0, 0)),
        out_shape=jax.ShapeDtypeStruct((nl, b, wd, n), F32),
        compiler_params=_params("parallel", "parallel"),
        name="mem_kv",
    )(mem, g.reshape(nl, 1, d), jnp.swapaxes(w, 1, 2).astype(BF16))


def _dil_attn_kernel(q_ref, kvc_ref, kvp_ref, bias_ref, acc_ref, st_ref, *, nr, tiles, prev_block):
    i = pl.program_id(2)
    qf = q_ref[...] * SCALE
    kvc = kvc_ref[...].astype(BF16)
    kvp = kvp_ref[...].astype(BF16) if prev_block else None
    chains = [(rr, tt, h) for rr in range(nr) for tt in range(tiles) for h in range(GROUP_HEADS)]
    n = len(chains)
    rows = lambda tt: slice(tt * Q_TILE, (tt + 1) * Q_TILE)
    lo_lane = lax.broadcasted_iota(jnp.int32, (Q_TILE, 128), 1) < HEAD_DIM

    def keys(rr, tt, h, off):
        lanes = slice(rr * 2 * GROUP_W + off + (h // 2) * 128, rr * 2 * GROUP_W + off + (h // 2 + 1) * 128)
        return kvp[:, lanes] if tt < 0 else kvc[rows(tt), lanes]

    has_prev = [tt > 0 or prev_block for _, tt, _ in chains]
    qh = [jnp.where(lo_lane if h % 2 == 0 else jnp.logical_not(lo_lane),
                    qf[rows(tt), rr * GROUP_W + (h // 2) * 128:rr * GROUP_W + (h // 2 + 1) * 128], 0.0).astype(BF16)
          for rr, tt, h in chains]
    lc = [_dot_nt(qh[c], keys(rr, tt, h, 0)) + bias_ref[h, :, :Q_TILE] for c, (rr, tt, h) in enumerate(chains)]
    lp = [None] * n
    for c, (rr, tt, h) in enumerate(chains):
        if has_prev[c]:
            z = _dot_nt(qh[c], keys(rr, tt - 1, h, 0)) + bias_ref[h, :, Q_TILE:]
            lp[c] = jnp.where(i > 0, z, NEG_INF) if tt == 0 else z
    m = [jnp.max(z, axis=-1, keepdims=True) for z in lc]
    m = [jnp.maximum(m[c], jnp.max(lp[c], axis=-1, keepdims=True)) if has_prev[c] else m[c] for c in range(n)]
    pc = [jnp.exp(lc[c] - m[c]) for c in range(n)]
    pp = [jnp.exp(lp[c] - m[c]) if has_prev[c] else None for c in range(n)]
    den = [jnp.sum(z, axis=-1, keepdims=True) for z in pc]
    den = [den[c] + jnp.sum(pp[c], axis=-1, keepdims=True) if has_prev[c] else den[c] for c in range(n)]
    o = [_dot(pc[c].astype(BF16), keys(rr, tt, h, GROUP_W)) for c, (rr, tt, h) in enumerate(chains)]
    o = [o[c] + _dot(pp[c].astype(BF16), keys(rr, tt - 1, h, GROUP_W)) if has_prev[c] else o[c]
         for c, (rr, tt, h) in enumerate(chains)]
    lane = lax.broadcasted_iota(jnp.int32, (Q_TILE, 128), 1)
    st = None
    for c, (rr, tt, h) in enumerate(chains):
        if h % 2 == 1:
            pair = slice(rr * GROUP_W + (h // 2) * 128, rr * GROUP_W + (h // 2 + 1) * 128)
            acc_ref[rows(tt), pair] = jnp.where(lo_lane, o[c - 1], o[c])
        st = jnp.zeros((Q_TILE, 128), F32) if h == 0 else st
        st = jnp.where(lane == h, m[c], st)
        st = jnp.where(lane == GROUP_HEADS + h, den[c], st)
        if h == GROUP_HEADS - 1:
            st_ref[rows(tt), rr * 128:(rr + 1) * 128] = st


def _dil_attn(qv, kvv, bias_tiles, g):
    dil = DILATIONS[g]
    b, ln, _ = qv.shape
    tiles = min(ln // Q_TILE, DIL_SEGMENTS)
    nr = min(dil, DIL_SEGMENTS // tiles)
    nblk = ln // (tiles * Q_TILE)
    return pl.pallas_call(
        functools.partial(_dil_attn_kernel, nr=nr, tiles=tiles, prev_block=nblk > 1),
        grid=(b, dil // nr, nblk),
        in_specs=[pl.BlockSpec((None, tiles * Q_TILE, nr * GROUP_W), lambda bi, r, i: (bi, i, r)),
                  pl.BlockSpec((None, tiles * Q_TILE, nr * 2 * GROUP_W), lambda bi, r, i: (bi, i, r)),
                  pl.BlockSpec((None, Q_TILE if nblk > 1 else 8, nr * 2 * GROUP_W),
                               lambda bi, r, i: (bi, jnp.maximum(tiles * i - 1, 0), r)),
                  pl.BlockSpec((GROUP_HEADS, Q_TILE, 2 * Q_TILE), lambda bi, r, i: (g, 0, 0))],
        out_specs=[pl.BlockSpec((None, tiles * Q_TILE, nr * GROUP_W), lambda bi, r, i: (bi, i, r)),
                   pl.BlockSpec((None, tiles * Q_TILE, nr * 128), lambda bi, r, i: (bi, i, r))],
        out_shape=[jax.ShapeDtypeStruct((b, ln, dil * GROUP_W), F32),
                   jax.ShapeDtypeStruct((b, ln, dil * 128), F32)],
        compiler_params=_params("parallel", "parallel", "arbitrary"),
        name=f"dil_attn_d{dil}",
    )(qv, kvv, kvv, bias_tiles)


def _sample_attn_kernel(q_ref, kn0_ref, kn1_ref, kn2_ref, c0_ref, c1_ref, c2_ref, b0_ref, b1_ref, b2_ref,
                        bnew_ref, o_ref, *, n_new):
    q = (q_ref[...] * SCALE).astype(BF16)
    qf = q.astype(F32)
    groups = ((kn0_ref, c0_ref, b0_ref), (kn1_ref, c1_ref, b1_ref), (kn2_ref, c2_ref, b2_ref))
    new_rows = []
    for g, (kn_ref, _, _) in enumerate(groups):
        kn = kn_ref[...]
        shifted = [kn] + [pltpu.roll(kn, s, axis=0) for s in range(1, n_new if g == 0 else 1)]
        new_rows.append([z.astype(BF16).astype(F32) for z in shifted])
    chains = [(h, g) for h in range(GROUP_HEADS) for g in range(N_GROUPS)]
    ids = range(len(chains))
    ks = lambda h: slice(h * HEAD_DIM, (h + 1) * HEAD_DIM)
    vs = lambda h: slice(GROUP_W + h * HEAD_DIM, GROUP_W + (h + 1) * HEAD_DIM)
    qs = lambda h, g: slice(g * GROUP_W + h * HEAD_DIM, g * GROUP_W + (h + 1) * HEAD_DIM)
    lp = [_dot(q[:, qs(h, g)], groups[g][1][0, h].astype(BF16)) + groups[g][2][h] for h, g in chains]
    lns = [[jnp.sum(qf[:, qs(h, g)] * rows[:, ks(h)], axis=-1, keepdims=True) + bnew_ref[g, s][:, h:h + 1]
            for s, rows in enumerate(new_rows[g])] for h, g in chains]
    m = [functools.reduce(jnp.maximum, [jnp.max(lp[c], axis=-1, keepdims=True)] + lns[c]) for c in ids]
    pp = [jnp.exp(lp[c] - m[c]) for c in ids]
    pn = [[jnp.exp(ln - m[c]) for ln in lns[c]] for c in ids]
    den = [jnp.sum(pp[c], axis=-1, keepdims=True) + sum(pn[c]) for c in ids]
    o = [_dot_nt(pp[c].astype(BF16), groups[g][1][1, h].astype(BF16)) for c, (h, g) in enumerate(chains)]
    o = [o[c] + sum(z.astype(BF16).astype(F32) * rows[:, vs(h)] for z, rows in zip(pn[c], new_rows[g]))
         for c, (h, g) in enumerate(chains)]
    for h in range(GROUP_HEADS):
        cs = [h * N_GROUPS + g for g in range(N_GROUPS)]
        mx = functools.reduce(jnp.maximum, [m[c] for c in cs])
        wts = [jnp.exp(m[c] - mx) * den[c] for c in cs]
        num = sum(w * (o[c] / den[c]) for w, c in zip(wts, cs))
        o_ref[:, ks(h)] = num / sum(wts)


def _sample_attn(q, kvn, caches, bias_past, bias_new, n_new):
    b, tp, _ = q.shape
    new_spec = pl.BlockSpec((None, tp, 2 * GROUP_W), lambda bi: (bi, 0, 0))
    cache_spec = lambda w: pl.BlockSpec((None, 2, GROUP_HEADS, HEAD_DIM, w), lambda bi: (bi, 0, 0, 0, 0))
    bias_spec = lambda w: pl.BlockSpec((GROUP_HEADS, tp, w), lambda bi: (0, 0, 0))
    return pl.pallas_call(
        functools.partial(_sample_attn_kernel, n_new=n_new),
        grid=(b,),
        in_specs=[pl.BlockSpec((None, tp, A_Q), lambda bi: (bi, 0, 0)), new_spec, new_spec, new_spec]
                 + [cache_spec(w) for w in WINDOWS] + [bias_spec(w) for w in WINDOWS]
                 + [pl.BlockSpec((N_GROUPS, n_new, tp, 128), lambda bi: (0, 0, 0, 0))],
        out_specs=pl.BlockSpec((None, tp, GROUP_W), lambda bi: (bi, 0, 0)),
        out_shape=jax.ShapeDtypeStruct((b, tp, GROUP_W), F32),
        compiler_params=_params("parallel"),
        name="sample_attn",
    )(q, kvn[0], kvn[1], kvn[2], *caches, *bias_past, bias_new)


def _cross_attn(q_ref, kvt_ref, dst_ref, nb):
    tq = q_ref.shape[0] // nb
    sub = min(tq, Q_TILE)
    chains = [(bi, slice(r0, r0 + sub), h) for bi in range(nb) for r0 in range(bi * tq, (bi + 1) * tq, sub)
              for h in range(MEM_W // HEAD_DIM)]
    kvt = [kvt_ref[bi].astype(BF16) for bi in range(nb)]
    pair = lambda h, off=0: slice(off + (h // 2) * 128, off + (h // 2 + 1) * 128)
    lo_lane = lax.broadcasted_iota(jnp.int32, (sub, 128), 1) < HEAD_DIM
    qm = [jnp.where(lo_lane if h % 2 == 0 else jnp.logical_not(lo_lane), q_ref[rw, pair(h)] * SCALE, 0.0).astype(BF16)
          for bi, rw, h in chains]
    lg = [_dot(qm[c], kvt[bi][pair(h), :]) for c, (bi, rw, h) in enumerate(chains)]
    p = [jnp.exp(z - jnp.max(z, axis=-1, keepdims=True)) for z in lg]
    den = [jnp.sum(z, axis=-1, keepdims=True) for z in p]
    o = [_dot_nt(p[c].astype(BF16), kvt[bi][pair(h, MEM_W), :]) / den[c] for c, (bi, rw, h) in enumerate(chains)]
    for c, (bi, rw, h) in enumerate(chains):
        if h % 2 == 1:
            dst_ref[rw, pair(h)] = jnp.where(lo_lane, o[c - 1], o[c])


def _rwkv_kernel(cols_ref, shift_ref, s0_ref, mu_ref, w0_ref, wup_ref, a0_ref, aup_ref, kk_ref, ka_ref, rk_ref,
                 lnw_ref, lnb_ref, y_ref, sout_ref, shout_ref,
                 carry, state, r_s, k_s, v_s, a_s, b_s, lw_s, y_s, *, tt, chunk, t_valid):
    j = pl.program_id(1)
    nb, t_in = cols_ref.shape[0], cols_ref.shape[1]
    zero_head = jnp.zeros((HEAD_DIM, HEAD_DIM), F32)

    @pl.when(j == 0)
    def _():
        carry[...] = shift_ref[...]
        for bi in range(nb):
            for p in range(RWKV_PAIRS):
                state[bi * RWKV_PAIRS + p] = jnp.concatenate(
                    [jnp.concatenate([s0_ref[bi, 2 * p], zero_head], axis=1),
                     jnp.concatenate([zero_head, s0_ref[bi, 2 * p + 1]], axis=1)], axis=0)

    direct = nb == 1 and t_in == tt
    y_dst = y_ref.at[0] if direct else y_s
    row = lax.broadcasted_iota(jnp.int32, (tt, 1), 0)
    live = row < t_valid
    blockdiag = (lax.broadcasted_iota(jnp.int32, (128, 128), 0) // HEAD_DIM
                 == lax.broadcasted_iota(jnp.int32, (128, 128), 1) // HEAD_DIM)

    def head_sum(x):
        lo = lax.broadcasted_iota(jnp.int32, x.shape, 1) < HEAD_DIM
        s0 = jnp.sum(jnp.where(lo, x, 0.0), axis=-1, keepdims=True)
        s1 = jnp.sum(jnp.where(lo, 0.0, x), axis=-1, keepdims=True)
        return jnp.where(lo, s0, s1)

    for bi in range(nb):
        base = slice(bi * tt, (bi + 1) * tt)
        cols = cols_ref[bi]
        if t_in < tt:
            cols = jnp.concatenate([cols, jnp.zeros((tt - t_in, cols.shape[1]), F32)], axis=0)
        prev = jnp.where(row == 0, carry[bi], pltpu.roll(cols, 1, axis=0))
        carry[bi] = cols[t_valid - 1:t_valid, :]
        xs = cols + mu_ref[...] * (prev - cols)
        r = xs[:, :RWKV_W]
        k = xs[:, RWKV_W:2 * RWKV_W]
        v = xs[:, 2 * RWKV_W:3 * RWKV_W]
        wd = xs[:, 3 * RWKV_W:3 * RWKV_W + LORA]
        ad = xs[:, 3 * RWKV_W + LORA:]
        z = w0_ref[...] + _dot(jnp.tanh(wd).astype(BF16), wup_ref[...])
        log_decay = -math.exp(-0.5) * jax.nn.sigmoid(z)
        a = jax.nn.sigmoid(a0_ref[...] + _dot(ad.astype(BF16), aup_ref[...]))
        kk = k * kk_ref[...]
        kk_sq = kk * kk
        k2 = k * (1.0 + (a - 1.0) * ka_ref[...])
        for p in range(RWKV_PAIRS):
            ps = slice(p * 128, (p + 1) * 128)
            nrm = jnp.maximum(jnp.sqrt(head_sum(kk_sq[:, ps])), 1e-12)
            kkn = kk[:, ps] / nrm
            if t_valid < tt:
                zero = jnp.zeros((tt, 128), F32)
                r_s[base, ps] = r[:, ps]
                k_s[base, ps] = jnp.where(live, k2[:, ps], zero)
                v_s[base, ps] = jnp.where(live, v[:, ps], zero)
                a_s[base, ps] = jnp.where(live, -kkn, zero)
                b_s[base, ps] = jnp.where(live, kkn * a[:, ps], zero)
                lw_s[base, ps] = jnp.where(live, log_decay[:, ps], zero)
            else:
                r_s[base, ps] = r[:, ps]
                k_s[base, ps] = k2[:, ps]
                v_s[base, ps] = v[:, ps]
                a_s[base, ps] = -kkn
                b_s[base, ps] = kkn * a[:, ps]
                lw_s[base, ps] = log_decay[:, ps]

    ci = lax.broadcasted_iota(jnp.int32, (chunk, chunk), 0)
    cj = lax.broadcasted_iota(jnp.int32, (chunk, chunk), 1)
    tri_incl = (ci >= cj).astype(BF16)
    lo_lane = lax.broadcasted_iota(jnp.int32, (chunk, 128), 1) < HEAD_DIM
    levels = max(1, math.ceil(math.log2(min(chunk, t_valid))))
    n2 = 2 * chunk
    ri = lax.broadcasted_iota(jnp.int32, (n2, n2), 0)
    rj = lax.broadcasted_iota(jnp.int32, (n2, n2), 1)
    same = ri // chunk == rj // chunk
    strict = jnp.logical_and(same, ri > rj)
    eye = (ri == rj).astype(F32)
    incl = jnp.logical_and(same, ri >= rj)
    own = lax.broadcasted_iota(jnp.int32, (n2, 128), 0) // chunk == lax.broadcasted_iota(jnp.int32, (n2, 128), 1) // HEAD_DIM
    dup = lambda z: jnp.concatenate([z, z], axis=0)
    cat = jnp.concatenate

    n_chunks = tt // chunk
    group = SCAN_INTERLEAVE if n_chunks % SCAN_INTERLEAVE == 0 else 1

    def chunk_body(ci, _):
        rows = [pl.ds(bi * tt + (ci * group + cc) * chunk, chunk) for cc in range(group) for bi in range(nb)]
        sl = [slice(p * 128, (p + 1) * 128) for p in range(RWKV_PAIRS)]
        chains = [(rw, s) for rw in rows for s in sl]
        ids = range(len(chains))
        per_chunk = nb * RWKV_PAIRS
        cum_all = [_mm(tri_incl, lw_s[rw, :], pb=3) for rw in rows]
        cum = [z[:, s] for z in cum_all for s in sl]
        p_incl = [jnp.exp(z) for z in cum]
        p_inv = [jnp.exp(-z) for z in cum]
        rr = [r_s[rw, s] for rw, s in chains]
        kc = [k_s[rw, s] for rw, s in chains]
        vc = [v_s[rw, s] for rw, s in chains]
        at2 = [jnp.where(own, dup(a_s[rw, s] * jnp.exp(cum[c] - lw_s[rw, s])), 0.0) for c, (rw, s) in enumerate(chains)]
        rt2 = [jnp.where(own, dup(rr[c] * p_incl[c]), 0.0) for c in ids]
        bt = [(b_s[rw, s] * p_inv[c]).astype(BF16) for c, (rw, s) in enumerate(chains)]
        kt = [(kc[c] * p_inv[c]).astype(BF16) for c in ids]
        vb = [z.astype(BF16) for z in vc]
        v2 = [dup(z) for z in vb]
        g = [_mm(cat([at2[c], rt2[c]], 0).astype(BF16), cat([dup(bt[c]), dup(kt[c])], 0), NT) for c in ids]
        a_ak = [jnp.where(strict, z[:n2, n2:], 0.0).astype(BF16) for z in g]
        apow = [jnp.where(strict, z[:n2, :n2], 0.0).astype(BF16) for z in g]
        a_r = [cat([jnp.where(incl, z[n2:, :n2], 0.0), jnp.where(incl, z[n2:, n2:], 0.0)], 1).astype(BF16)
               for z in g]
        akv = [_mm(a_ak[c], v2[c]) for c in ids]
        tinv = [eye + jnp.where(strict, z[:n2, :n2], 0.0) for z in g]
        for _ in range(levels - 1):
            apow = [_mm(z, z).astype(BF16) for z in apow]
            tinv = [tinv[c] + _mm(tinv[c].astype(BF16), apow[c]) for c in ids]
        sol = [_mm(tinv[c].astype(BF16), cat([at2[c], akv[c]], axis=1).astype(BF16)) for c in ids]
        ws = [z[:, :128] for z in sol]
        u0s = [jnp.where(own, z[:, 128:], 0.0) for z in sol]
        zeros2 = jnp.zeros((n2, 128), BF16)
        qy = [_mm(a_r[c], cat([cat([ws[c], u0s[c]], 1).astype(BF16), cat([zeros2, v2[c]], 1)], 0)) for c in ids]
        qs = [(rt2[c] + qy[c][:, :128]).astype(BF16) for c in ids]
        zeros1 = jnp.zeros((chunk, 128), BF16)
        lhs = [cat([cat([ws[c][:chunk] + ws[c][chunk:], u0s[c][:chunk] + u0s[c][chunk:]], 1).astype(BF16),
                    cat([zeros1, vb[c]], 1)], 0) for c in ids]
        wbn = [_mm(lhs[c], cat([bt[c], kt[c]], 0), TN) for c in ids]
        wb = [z[:128].astype(BF16) for z in wbn]
        s_cur = [state[q] for q in range(per_chunk)]
        ys = []
        for cc in range(group):
            base = cc * per_chunk
            s_b = [z.astype(BF16) for z in s_cur]
            ys += [_mm(qs[base + q], s_b[q], NT) + qy[base + q][:, 128:] for q in range(per_chunk)]
            sw = [_mm(s_b[q], wb[base + q]) for q in range(per_chunk)]
            s_cur = [jnp.where(blockdiag, (s_cur[q] + sw[q] + wbn[base + q][128:])
                               * p_incl[base + q][chunk - 1:chunk, :], 0.0) for q in range(per_chunk)]
        for q in range(per_chunk):
            state[q] = s_cur[q]
        y = [jnp.where(lo_lane, z[:chunk], z[chunk:]) for z in ys]

        bonus = [head_sum(rr[c] * kc[c] * rk_ref[:, s]) for c, (_, s) in enumerate(chains)]
        dlt = [y[c] - head_sum(y[c]) * (1.0 / HEAD_DIM) for c in ids]
        var = [head_sum(z * z) * (1.0 / HEAD_DIM) for z in dlt]
        for c, (rw, s) in enumerate(chains):
            yn = dlt[c] * lax.rsqrt(var[c] + GN_EPS) * lnw_ref[:, s] + lnb_ref[:, s]
            y_dst[rw, s] = yn + bonus[c] * vc[c]
        return 0

    for ci in range(n_chunks // group):
        chunk_body(ci, 0)
    if not direct:
        for bi in range(nb):
            y_ref[bi] = y_s[bi * tt:bi * tt + t_in, :]

    @pl.when(j == pl.num_programs(1) - 1)
    def _():
        for bi in range(nb):
            for p in range(RWKV_PAIRS):
                sout_ref[bi, 2 * p] = state[bi * RWKV_PAIRS + p, 0:HEAD_DIM, 0:HEAD_DIM]
                sout_ref[bi, 2 * p + 1] = state[bi * RWKV_PAIRS + p, HEAD_DIM:, HEAD_DIM:]
        shout_ref[...] = carry[...]


def _rwkv_scan(cols, shift_prev, s0, prm, tt, t_valid, nb):
    b, t, _ = cols.shape
    t_blk = min(t, tt)
    heads = pl.BlockSpec((nb, 2 * RWKV_PAIRS, HEAD_DIM, HEAD_DIM), lambda bi, j: (bi, 0, 0, 0))
    row = lambda n: pl.BlockSpec((1, n), lambda bi, j: (0, 0))
    vec = lambda a: a.reshape(1, -1)
    y, s_out, sh_out = pl.pallas_call(
        functools.partial(_rwkv_kernel, tt=tt, chunk=SCAN_CHUNK, t_valid=min(t_valid, tt)),
        grid=(b // nb, t // t_blk),
        in_specs=[pl.BlockSpec((nb, t_blk, C_SHIFT), lambda bi, j: (bi, j, 0)),
                  pl.BlockSpec((nb, 1, C_SHIFT), lambda bi, j: (bi, 0, 0)),
                  heads,
                  row(C_SHIFT), row(RWKV_W),
                  pl.BlockSpec((LORA, RWKV_W), lambda bi, j: (0, 0)),
                  row(RWKV_W),
                  pl.BlockSpec((LORA, RWKV_W), lambda bi, j: (0, 0)),
                  row(RWKV_W), row(RWKV_W), row(RWKV_W), row(RWKV_W), row(RWKV_W)],
        out_specs=[pl.BlockSpec((nb, t_blk, RWKV_W), lambda bi, j: (bi, j, 0)),
                   heads,
                   pl.BlockSpec((nb, 1, C_SHIFT), lambda bi, j: (bi, 0, 0))],
        out_shape=[jax.ShapeDtypeStruct((b, t, RWKV_W), F32),
                   jax.ShapeDtypeStruct((b, 2 * RWKV_PAIRS, HEAD_DIM, HEAD_DIM), F32),
                   jax.ShapeDtypeStruct((b, 1, C_SHIFT), F32)],
        scratch_shapes=[pltpu.VMEM((nb, 1, C_SHIFT), F32), pltpu.VMEM((nb * RWKV_PAIRS, 128, 128), F32)]
                       + [pltpu.VMEM((nb * tt, RWKV_W), F32)] * 7,
        compiler_params=_params("parallel", "arbitrary"),
        name="rwkv_scan",
    )(cols, shift_prev.reshape(b, 1, C_SHIFT), s0, vec(prm["mu"]), vec(prm["w0"]), prm["w_up"].astype(BF16),
      vec(prm["a0"]), prm["a_up"].astype(BF16), vec(prm["k_k"]), vec(prm["k_a"]), vec(prm["r_k"]),
      vec(prm["ln_w"]), vec(prm["ln_b"]))
    return y, s_out, sh_out.reshape(b, C_SHIFT)


def _post_kernel(*refs, merge_dils, mix_w, nb, recompute):
    x_ref, a_ref, b_ref, kvt_ref, w_ref, g_ref = refs[:6]
    mix_refs = refs[6:-4]
    y_ref, stage, omem, qmem = refs[-4:]
    tm = x_ref.shape[0]
    n_parts = POST_PARTS if tm % (8 * POST_PARTS) == 0 and tm // POST_PARTS >= 64 else 1
    parts = [slice(i * tm // n_parts, (i + 1) * tm // n_parts) for i in range(n_parts)]
    if recompute:
        xn = [_rms(x_ref[rs, :], a_ref[...]).astype(BF16) for rs in parts]
        for rs, z in zip(parts, xn):
            qmem[rs, :] = _dot(z, b_ref[:, :MEM_W])
        gate = [_dot(z, b_ref[:, MEM_W:]) for z in xn]
        _cross_attn(qmem, kvt_ref, omem, nb)
    else:
        gate = [b_ref[rs, :] for rs in parts]
        _cross_attn(a_ref, kvt_ref, omem, nb)
    if merge_dils:
        er = lax.broadcasted_iota(jnp.int32, (128, 2 * GROUP_W), 0)
        ec = lax.broadcasted_iota(jnp.int32, (128, 2 * GROUP_W), 1)
        expand = (er == ec // HEAD_DIM).astype(BF16)
        accs, stats = [], []
        for g, dil in enumerate(merge_dils):
            acc_ref, st_ref = mix_refs[2 * g], mix_refs[2 * g + 1]
            accs.append(acc_ref[...] if dil == 1 else _from_classes(acc_ref, stage, dil, GROUP_W))
            stats.append(st_ref[...] if dil == 1 else _from_classes(st_ref, stage, dil, 128))
        st = [[_mm(z[rs, :], expand, pa=3) for z in stats] for rs in parts]
        mx = [functools.reduce(jnp.maximum, [z[:, :GROUP_W] for z in sp]) for sp in st]
        wts = [[jnp.exp(z[:, :GROUP_W] - mx[i]) * z[:, GROUP_W:] for z in sp] for i, sp in enumerate(st)]
        mix = [sum(w * (a[rs, :] / z[:, GROUP_W:]) for w, a, z in zip(wts[i], accs, st[i])) / sum(wts[i])
               for i, rs in enumerate(parts)]
    else:
        mix = [mix_refs[0][rs, :] for rs in parts]
    act = [z * jax.nn.sigmoid(z) for z in gate]
    h1 = [(mix[i] * act[i][:, :mix_w]).astype(BF16) for i in range(n_parts)]
    h2 = [(omem[rs, :] * act[i][:, mix_w:]).astype(BF16) for i, rs in enumerate(parts)]
    out = [_dot(h1[i], w_ref[:mix_w, :]) + _dot(h2[i], w_ref[mix_w:, :]) for i in range(n_parts)]
    for i, rs in enumerate(parts):
        y_ref[rs, :] = x_ref[rs, :] + _rms(out[i], g_ref[...])


def _post(x, mix, qg, kvt, layer, w_out, g_post, tm, rows_per_batch, recompute):
    m, d = x.shape
    merge = isinstance(mix, (list, tuple))
    rows = lambda wd: pl.BlockSpec((tm, wd), lambda i: (i, 0))
    if recompute:
        gw = qg[1].shape[1] - MEM_W
        qg_arrays = [qg[0].reshape(1, d), qg[1].astype(BF16)]
        qg_specs = [pl.BlockSpec((1, d), lambda i: (0, 0)), pl.BlockSpec((d, MEM_W + gw), lambda i: (0, 0))]
    else:
        gw = qg[1].shape[1]
        qg_arrays, qg_specs = list(qg), [rows(MEM_W), rows(gw)]
    nb = max(1, tm // rows_per_batch)
    per_batch = max(1, rows_per_batch // tm)
    kvt_spec = pl.BlockSpec((None, nb, 2 * MEM_W, N_MEM), lambda i: (layer, i // per_batch, 0, 0))
    if merge:
        mix_w = GROUP_W
        merge_dils = tuple(acc.shape[2] // GROUP_W for acc, _ in mix)
        tiles = mix[0][0].shape[1] * merge_dils[0] // tm
        mix_arrays = [a for pair in mix for a in pair]
        mix_specs = [pl.BlockSpec((None, tm // dil, a.shape[2]), lambda i: (i // tiles, i % tiles, 0))
                     for dil, pair in zip(merge_dils, mix) for a in pair]
    else:
        mix_w, merge_dils, mix_arrays, mix_specs = mix.shape[1], (), [mix], [rows(mix.shape[1])]
    return pl.pallas_call(
        functools.partial(_post_kernel, merge_dils=merge_dils, mix_w=mix_w, nb=nb, recompute=recompute),
        grid=(m // tm,),
        in_specs=[rows(d)] + qg_specs + [
                  kvt_spec,
                  pl.BlockSpec((gw, d), lambda i: (0, 0)),
                  pl.BlockSpec((1, d), lambda i: (0, 0))] + mix_specs,
        out_specs=rows(d),
        out_shape=jax.ShapeDtypeStruct((m, d), F32),
        scratch_shapes=[pltpu.VMEM((tm, 128), F32), pltpu.VMEM((tm, MEM_W), F32), pltpu.VMEM((tm, MEM_W), F32)],
        compiler_params=_params("parallel"),
        name="post",
    )(x, *qg_arrays, kvt, w_out.astype(BF16), g_post.reshape(1, d), *mix_arrays)


def _t5_bucket(dist):
    max_exact = N_BUCKETS // 2
    d = jnp.maximum(dist, 1).astype(F32)
    large = max_exact + (jnp.log(d / max_exact) / math.log(MAX_DISTANCE / max_exact)
                         * (N_BUCKETS - max_exact)).astype(jnp.int32)
    large = jnp.minimum(large, N_BUCKETS - 1)
    return jnp.where(dist < max_exact, dist, large)


def _group_bias(rel_bias, g):
    dist = DILATIONS[g] * jnp.arange(A_KEYS, dtype=jnp.int32)
    bias = rel_bias[_t5_bucket(dist)]
    return bias[:, g * GROUP_HEADS:(g + 1) * GROUP_HEADS].T.astype(F32)


def _bias_tiles_kernel(x_ref, o_ref):
    row = lax.broadcasted_iota(jnp.int32, (Q_TILE, 2 * Q_TILE), 0)
    for h in range(o_ref.shape[0]):
        t = jnp.broadcast_to(x_ref[h:h + 1, :], (Q_TILE, 2 * Q_TILE))
        for bit in range(int(math.log2(Q_TILE))):
            t = jnp.where((row >> bit) & 1 == 1, pltpu.roll(t, 1 << bit, axis=1), t)
        o_ref[h] = t


def _prompt_bias(biases):
    bias = jnp.concatenate(biases, axis=0)
    neg = jnp.full((bias.shape[0], Q_TILE - 1), NEG_INF, F32)
    table = jnp.concatenate([bias[:, :1], neg, bias[:, :0:-1]], axis=1)
    n = table.shape[0]
    return pl.pallas_call(
        _bias_tiles_kernel,
        out_shape=jax.ShapeDtypeStruct((n, Q_TILE, 2 * Q_TILE), F32),
        name="bias_tiles",
    )(table)


def _sample_bias(biases, n_new, tp):
    t = np.arange(tp)[:, None]
    real = t < n_new
    past = []
    for g, bias in enumerate(biases):
        w, dil = WINDOWS[g], DILATIONS[g]
        spread = jnp.concatenate([bias[:, :0:-1, None], jnp.full((GROUP_HEADS, w // dil, dil - 1), NEG_INF, F32)],
                                 axis=-1).reshape(GROUP_HEADS, w)
        rows = [jnp.concatenate([jnp.full((GROUP_HEADS, tt), NEG_INF, F32), spread[:, :w - tt]], axis=-1)
                for tt in range(n_new)] + [jnp.zeros((GROUP_HEADS, w), F32)] * (tp - n_new)
        past.append(jnp.stack(rows, axis=1))
    s = np.arange(n_new)[None, :, None]
    g_idx = np.arange(N_GROUPS)[:, None, None]
    ok = (s <= t[None, :, 0]) & real[None, :, 0] & ((g_idx == 0) | (s == 0))
    vals = jnp.stack([bias[:, :n_new].T for bias in biases])
    new = jnp.where(jnp.asarray(ok)[..., None], vals[:, :, None, :], NEG_INF)
    return past, jnp.pad(new, ((0, 0), (0, 0), (0, 0), (0, 128 - GROUP_HEADS)))


def _pieces_a():
    q = [((g * GROUP_W, (g + 1) * GROUP_W),) for g in range(N_GROUPS)]
    kv = [((A_Q + g * GROUP_W, A_Q + (g + 1) * GROUP_W), (2 * A_Q + g * GROUP_W, 2 * A_Q + (g + 1) * GROUP_W))
          for g in range(N_GROUPS)]
    qmem = ((3 * A_Q, 3 * A_Q + MEM_W),)
    gate = ((3 * A_Q + MEM_W, 3 * A_Q + MEM_W + GROUP_W + MEM_W),)
    return tuple(q + kv + [qmem, gate])


def _pieces_b():
    return (((0, C_SHIFT),),)


def kernel(x_prompt, x_sample, mem_prompt, cache_mem_kv, cache_win0, cache_win1, cache_win2, state_wkv, state_shift, norm_pre, norm_post, norm_mem, w_mem_kv, rel_bias, w_in_a, w_out_a, w_in_b, w_out_b, rwkv_mu, rwkv_w0, rwkv_w_up, rwkv_a0, rwkv_a_up, rwkv_k_k, rwkv_k_a, rwkv_r_k, rwkv_ln_w, rwkv_ln_b):
    bp, tp, d = x_prompt.shape
    bs, ts, _ = x_sample.shape
    tsp = SAMPLE_PAD_T
    xp = x_prompt.reshape(bp * tp, d)
    xs = jnp.pad(x_sample, ((0, 0), (0, tsp - ts), (0, 0))).reshape(bs * tsp, d)
    tm_p, tm_s = 512, bs * tsp
    tm_post_s = 8 * tsp
    time_minor = lambda c: jnp.moveaxis(c, -4, -1)
    caches = [time_minor(c[0]) for c in (cache_win0, cache_win1, cache_win2)]
    mkv_s = time_minor(cache_mem_kv).reshape(cache_mem_kv.shape[0], bs, 2 * MEM_W, N_MEM)
    mkv_p = _mem_kv(mem_prompt, norm_mem, w_mem_kv)

    biases = [_group_bias(rel_bias, g) for g in range(N_GROUPS)]
    w_gq_b, w_mix_b = w_in_b[0][:, C_SHIFT:], w_in_b[0][:, :C_SHIFT]
    outs_p = _norm_proj(xp, norm_pre[0], w_in_a[0], _pieces_a(), tm_p, transposed=(3, 4, 5),
                        dils=DILATIONS + DILATIONS + (1, 1), rows_per_batch=tp)
    outs_s = _norm_proj(xs, norm_pre[0], w_in_a[0], _pieces_a(), tm_s)
    q_p, kv_p, qg_p, kvt_p = outs_p[0:3], outs_p[3:6], outs_p[6:8], outs_p[8:11]
    q_s, kv_s, qg_s = outs_s[0:3], outs_s[3:6], outs_s[6:8]

    bias_tiles = _prompt_bias(biases)
    merged = [_dil_attn(q_p[g].reshape(bp, tp // DILATIONS[g], -1), kv_p[g].reshape(bp, tp // DILATIONS[g], -1),
                        bias_tiles, g) for g in range(N_GROUPS)]
    xp1 = _post(xp, merged, qg_p, mkv_p, 0, w_out_a[0], norm_post[0], tm_p, tp, False)

    bias_past, bias_new = _sample_bias(biases, ts, tsp)
    q_s_all = jnp.concatenate(q_s, axis=-1).reshape(bs, tsp, A_Q)
    kvn = [a.reshape(bs, tsp, 2 * GROUP_W) for a in kv_s]
    o_s = _sample_attn(q_s_all, kvn, caches, bias_past, bias_new, ts)
    xs1 = _post(xs, o_s.reshape(bs * tsp, GROUP_W), qg_s, mkv_s, 0, w_out_a[0], norm_post[0], tm_post_s, tsp, False)

    (cols_p,) = _norm_proj(xp1, norm_pre[1], w_mix_b, _pieces_b(), tm_p)
    (cols_s,) = _norm_proj(xs1, norm_pre[1], w_mix_b, _pieces_b(), tm_s)
    prm = dict(mu=rwkv_mu[0], w0=rwkv_w0[0], w_up=rwkv_w_up[0], a0=rwkv_a0[0], a_up=rwkv_a_up[0], k_k=rwkv_k_k[0],
               k_a=rwkv_k_a[0], r_k=rwkv_r_k[0], ln_w=rwkv_ln_w[0], ln_b=rwkv_ln_b[0])
    y_p, wkv_p, sh_p = _rwkv_scan(cols_p.reshape(bp, tp, C_SHIFT), jnp.zeros((bp, C_SHIFT), F32),
                                  jnp.zeros((bp, 2 * RWKV_PAIRS, HEAD_DIM, HEAD_DIM), F32), prm, 512, tp, 1)
    y_s, wkv_s, sh_s = _rwkv_scan(cols_s.reshape(bs, tsp, C_SHIFT), state_shift[0], state_wkv[0], prm, SCAN_CHUNK,
                                  ts, 4)
    y_s = y_s.reshape(bs * tsp, RWKV_W)
    qg_b = (norm_pre[1], w_gq_b)
    xp2 = _post(xp1, y_p.reshape(bp * tp, RWKV_W), qg_b, mkv_p, 1, w_out_b[0], norm_post[1], tm_p, tp, True)
    xs2 = _post(xs1, y_s, qg_b, mkv_s, 1, w_out_b[0], norm_post[1], tm_post_s, tsp, True)

    kv_shape = (2, GROUP_HEADS, HEAD_DIM)
    time_major = lambda c: jnp.moveaxis(c, -1, -4)
    new_mem_kv = time_major(mkv_p.reshape(mkv_p.shape[0], bp, 2, MEM_W // HEAD_DIM, HEAD_DIM, N_MEM))
    win_p = [time_major(kvt_p[g].reshape(bp, *kv_shape, tp)[..., tp - min(WINDOWS[g], tp):])[None]
             for g in range(N_GROUPS)]
    win_s = [kv_s[g].reshape(bs, tsp, *kv_shape)[None, :, :ts] for g in range(N_GROUPS)]
    return (xp2.reshape(bp, tp, d), xs2.reshape(bs, tsp, d)[:, :ts], new_mem_kv,
            win_p[0], win_p[1], win_p[2], win_s[0], win_s[1], win_s[2],
            wkv_p[None], wkv_s[None], sh_p[None], sh_s[None])
```

```python
import functools
import math

import numpy as np
import jax
import jax.numpy as jnp
from jax import lax
from jax.experimental import pallas as pl
from jax.experimental.pallas import tpu as pltpu

F32 = jnp.float32
BF16 = jnp.bfloat16

D_MODEL = 1024
HEAD_DIM = 64
N_GROUPS = 3
GROUP_HEADS = 4
WINDOWS = (128, 512, 2048)
DILATIONS = (1, 4, 16)
A_HEADS = N_GROUPS * GROUP_HEADS
A_KEYS = 129
GROUP_W = GROUP_HEADS * HEAD_DIM
A_Q = A_HEADS * HEAD_DIM
N_MEM = 256
MEM_W = 256
RWKV_W = 768
RWKV_PAIRS = RWKV_W // 128
LORA = 64
C_SHIFT = 3 * RWKV_W + 2 * LORA
N_BUCKETS = 32
MAX_DISTANCE = WINDOWS[-1]
RMS_EPS = 1e-6
GN_EPS = 64e-5
NEG_INF = -1e30
SCALE = HEAD_DIM ** -0.5

Q_TILE = 128
DIL_SEGMENTS = 8
SCAN_CHUNK = 64
POST_PARTS = 2
SCAN_INTERLEAVE = 2
SAMPLE_PAD_T = 8
VMEM_LIMIT = 56 * 1024 * 1024
NN =(((1,), (0,)), ((), ()))
NT = (((1,), (1,)), ((), ()))
TN = (((0,), (0,)), ((), ()))


def _params(*sem):
    return pltpu.CompilerParams(dimension_semantics=sem, vmem_limit_bytes=VMEM_LIMIT)


def _dot(a, b):
    return jnp.dot(a, b, preferred_element_type=F32)


def _dot_nt(a, b):
    return lax.dot_general(a, b, NT, preferred_element_type=F32)


def _split(x, n):
    if x.dtype == BF16:
        return [x]
    parts, rem = [], x
    for i in range(n):
        parts.append(rem.astype(BF16))
        if i + 1 < n:
            rem = rem - parts[-1].astype(F32)
    return parts


def _mm(a, b, dims=NN, pa=1, pb=1):
    pas, pbs = _split(a, pa), _split(b, pb)
    order = max(len(pas), len(pbs))
    out = None
    for i, ai in enumerate(pas):
        for j, bj in enumerate(pbs):
            if i + j < order:
                term = lax.dot_general(ai, bj, dims, preferred_element_type=F32)
                out = term if out is None else out + term
    return out


def _rms(x, g):
    return x * lax.rsqrt(jnp.mean(x * x, axis=-1, keepdims=True) + RMS_EPS) * g


def _to_classes(dst_ref, stage, val, dil, width, off):
    tm = val.shape[0]
    for j in range(val.shape[1] // 128):
        stage[...] = val[:, 128 * j:128 * (j + 1)]
        for r in range(dil):
            lane0 = r * width + off + 128 * j
            dst_ref[:, lane0:lane0 + 128] = stage[pl.ds(r, tm // dil, stride=dil), :]


def _from_classes(src_ref, stage, dil, width):
    tm = stage.shape[0]
    chunks = []
    for j in range(width // 128):
        for r in range(dil):
            lane0 = r * width + 128 * j
            stage[pl.ds(r, tm // dil, stride=dil), :] = src_ref[:, lane0:lane0 + 128]
        chunks.append(stage[...])
    return jnp.concatenate(chunks, axis=1)


def _norm_proj_kernel(x_ref, g_ref, w_ref, *refs, pieces, transposed, dils):
    stage = refs[-1]
    out_refs = refs[:-1]
    xn = _rms(x_ref[...], g_ref[...]).astype(BF16)
    t_refs = iter(out_refs[len(pieces):])
    for idx, (out_ref, cols) in enumerate(zip(out_refs, pieces)):
        t_ref = next(t_refs) if idx in transposed else None
        width = sum(c1 - c0 for c0, c1 in cols)
        off = 0
        for c0, c1 in cols:
            res = _dot(xn, w_ref[:, c0:c1])
            if dils[idx] > 1:
                _to_classes(out_ref, stage, res, dils[idx], width, off)
            else:
                out_ref[:, off:off + c1 - c0] = res
            if t_ref is not None:
                t_ref[off:off + c1 - c0, :] = res.T
            off += c1 - c0


def _norm_proj(x, g, w, pieces, tm, transposed=(), dils=None, rows_per_batch=None):
    m, d = x.shape
    n = w.shape[1]
    dils = tuple(dils) if dils else (1,) * len(pieces)
    widths = [sum(c1 - c0 for c0, c1 in cols) for cols in pieces]
    tiles = rows_per_batch // tm if rows_per_batch else None
    out_specs, out_shape = [], []
    for wd, dil in zip(widths, dils):
        if dil > 1:
            out_specs.append(pl.BlockSpec((None, tm // dil, dil * wd), lambda i: (i // tiles, i % tiles, 0)))
            out_shape.append(jax.ShapeDtypeStruct((m // rows_per_batch, rows_per_batch // dil, dil * wd), F32))
        else:
            out_specs.append(pl.BlockSpec((tm, wd), lambda i: (i, 0)))
            out_shape.append(jax.ShapeDtypeStruct((m, wd), F32))
    out_specs += [pl.BlockSpec((None, widths[idx], tm), lambda i: (i // tiles, 0, i % tiles)) for idx in transposed]
    out_shape += [jax.ShapeDtypeStruct((m // rows_per_batch, widths[idx], rows_per_batch), F32) for idx in transposed]
    return pl.pallas_call(
        functools.partial(_norm_proj_kernel, pieces=pieces, transposed=tuple(transposed), dils=dils),
        grid=(m // tm,),
        in_specs=[pl.BlockSpec((tm, d), lambda i: (i, 0)),
                  pl.BlockSpec((1, d), lambda i: (0, 0)),
                  pl.BlockSpec((d, n), lambda i: (0, 0))],
        out_specs=out_specs,
        out_shape=out_shape,
        scratch_shapes=[pltpu.VMEM((tm, 128), F32)],
        compiler_params=_params("parallel"),
        name="norm_proj",
    )(x, g.reshape(1, d), w.astype(BF16))


def _mem_kv_kernel(x_ref, g_ref, wt_ref, o_ref):
    xn = _rms(x_ref[...], g_ref[...]).astype(BF16)
    o_ref[...] = _dot_nt(wt_ref[...], xn)


def _mem_kv(mem, g, w):
    b, n, d = mem.shape
    nl, _, wd = w.shape
    return pl.pallas_call(
        _mem_kv_kernel,
        grid=(nl, b),
        in_specs=[pl.BlockSpec((None, n, d), lambda l, bi: (bi, 0, 0)),
                  pl.BlockSpec((None, 1, d), lambda l, bi: (l, 0, 0)),
                  pl.BlockSpec((None, wd, d), lambda l, bi: (l, 0, 0))],
        out_specs=pl.BlockSpec((None, None, wd, n), lambda l, bi: (l, bi, 0, 0)),
        out_shape=jax.ShapeDtypeStruct((nl, b, wd, n), F32),
        compiler_params=_params("parallel", "parallel"),
        name="mem_kv",
    )(mem, g.reshape(nl, 1, d), jnp.swapaxes(w, 1, 2).astype(BF16))


def _dil_attn_kernel(q_ref, kvc_ref, kvp_ref, bias_ref, acc_ref, st_ref, *, nr, tiles, prev_block):
    i = pl.program_id(2)
    qf = q_ref[...] * SCALE
    kvc = kvc_ref[...].astype(BF16)
    kvp = kvp_ref[...].astype(BF16) if prev_block else None
    chains = [(rr, tt, h) for rr in range(nr) for tt in range(tiles) for h in range(GROUP_HEADS)]
    n = len(chains)
    rows = lambda tt: slice(tt * Q_TILE, (tt + 1) * Q_TILE)
    lo_lane = lax.broadcasted_iota(jnp.int32, (Q_TILE, 128), 1) < HEAD_DIM

    def keys(rr, tt, h, off):
        lanes = slice(rr * 2 * GROUP_W + off + (h // 2) * 128, rr * 2 * GROUP_W + off + (h // 2 + 1) * 128)
        return kvp[:, lanes] if tt < 0 else kvc[rows(tt), lanes]

    has_prev = [tt > 0 or prev_block for _, tt, _ in chains]
    qh = [jnp.where(lo_lane if h % 2 == 0 else jnp.logical_not(lo_lane),
                    qf[rows(tt), rr * GROUP_W + (h // 2) * 128:rr * GROUP_W + (h // 2 + 1) * 128], 0.0).astype(BF16)
          for rr, tt, h in chains]
    lc = [_dot_nt(qh[c], keys(rr, tt, h, 0)) + bias_ref[h, :, :Q_TILE] for c, (rr, tt, h) in enumerate(chains)]
    lp = [None] * n
    for c, (rr, tt, h) in enumerate(chains):
        if has_prev[c]:
            z = _dot_nt(qh[c], keys(rr, tt - 1, h, 0)) + bias_ref[h, :, Q_TILE:]
            lp[c] = jnp.where(i > 0, z, NEG_INF) if tt == 0 else z
    m = [jnp.max(z, axis=-1, keepdims=True) for z in lc]
    m = [jnp.maximum(m[c], jnp.max(lp[c], axis=-1, keepdims=True)) if has_prev[c] else m[c] for c in range(n)]
    pc = [jnp.exp(lc[c] - m[c]) for c in range(n)]
    pp = [jnp.exp(lp[c] - m[c]) if has_prev[c] else None for c in range(n)]
    den = [jnp.sum(z, axis=-1, keepdims=True) for z in pc]
    den = [den[c] + jnp.sum(pp[c], axis=-1, keepdims=True) if has_prev[c] else den[c] for c in range(n)]
    o = [_dot(pc[c].astype(BF16), keys(rr, tt, h, GROUP_W)) for c, (rr, tt, h) in enumerate(chains)]
    o = [o[c] + _dot(pp[c].astype(BF16), keys(rr, tt - 1, h, GROUP_W)) if has_prev[c] else o[c]
         for c, (rr, tt, h) in enumerate(chains)]
    lane = lax.broadcasted_iota(jnp.int32, (Q_TILE, 128), 1)
    st = None
    for c, (rr, tt, h) in enumerate(chains):
        if h % 2 == 1:
            pair = slice(rr * GROUP_W + (h // 2) * 128, rr * GROUP_W + (h // 2 + 1) * 128)
            acc_ref[rows(tt), pair] = jnp.where(lo_lane, o[c - 1], o[c])
        st = jnp.zeros((Q_TILE, 128), F32) if h == 0 else st
        st = jnp.where(lane == h, m[c], st)
        st = jnp.where(lane == GROUP_HEADS + h, den[c], st)
        if h == GROUP_HEADS - 1:
            st_ref[rows(tt), rr * 128:(rr + 1) * 128] = st


def _dil_attn(qv, kvv, bias_tiles, g):
    dil = DILATIONS[g]
    b, ln, _ = qv.shape
    tiles = min(ln // Q_TILE, DIL_SEGMENTS)
    nr = min(dil, DIL_SEGMENTS // tiles)
    nblk = ln // (tiles * Q_TILE)
    return pl.pallas_call(
        functools.partial(_dil_attn_kernel, nr=nr, tiles=tiles, prev_block=nblk > 1),
        grid=(b, dil // nr, nblk),
        in_specs=[pl.BlockSpec((None, tiles * Q_TILE, nr * GROUP_W), lambda bi, r, i: (bi, i, r)),
                  pl.BlockSpec((None, tiles * Q_TILE, nr * 2 * GROUP_W), lambda bi, r, i: (bi, i, r)),
                  pl.BlockSpec((None, Q_TILE if nblk > 1 else 8, nr * 2 * GROUP_W),
                               lambda bi, r, i: (bi, jnp.maximum(tiles * i - 1, 0), r)),
                  pl.BlockSpec((GROUP_HEADS, Q_TILE, 2 * Q_TILE), lambda bi, r, i: (g, 0, 0))],
        out_specs=[pl.BlockSpec((None, tiles * Q_TILE, nr * GROUP_W), lambda bi, r, i: (bi, i, r)),
                   pl.BlockSpec((None, tiles * Q_TILE, nr * 128), lambda bi, r, i: (bi, i, r))],
        out_shape=[jax.ShapeDtypeStruct((b, ln, dil * GROUP_W), F32),
                   jax.ShapeDtypeStruct((b, ln, dil * 128), F32)],
        compiler_params=_params("parallel", "parallel", "arbitrary"),
        name=f"dil_attn_d{dil}",
    )(qv, kvv, kvv, bias_tiles)


def _sample_attn_kernel(q_ref, kn0_ref, kn1_ref, kn2_ref, c0_ref, c1_ref, c2_ref, b0_ref, b1_ref, b2_ref,
                        bnew_ref, o_ref, *, n_new):
    q = (q_ref[...] * SCALE).astype(BF16)
    qf = q.astype(F32)
    groups = ((kn0_ref, c0_ref, b0_ref), (kn1_ref, c1_ref, b1_ref), (kn2_ref, c2_ref, b2_ref))
    new_rows = []
    for g, (kn_ref, _, _) in enumerate(groups):
        kn = kn_ref[...]
        shifted = [kn] + [pltpu.roll(kn, s, axis=0) for s in range(1, n_new if g == 0 else 1)]
        new_rows.append([z.astype(BF16).astype(F32) for z in shifted])
    chains = [(h, g) for h in range(GROUP_HEADS) for g in range(N_GROUPS)]
    ids = range(len(chains))
    ks = lambda h: slice(h * HEAD_DIM, (h + 1) * HEAD_DIM)
    vs = lambda h: slice(GROUP_W + h * HEAD_DIM, GROUP_W + (h + 1) * HEAD_DIM)
    qs = lambda h, g: slice(g * GROUP_W + h * HEAD_DIM, g * GROUP_W + (h + 1) * HEAD_DIM)
    lp = [_dot(q[:, qs(h, g)], groups[g][1][0, h].astype(BF16)) + groups[g][2][h] for h, g in chains]
    lns = [[jnp.sum(qf[:, qs(h, g)] * rows[:, ks(h)], axis=-1, keepdims=True) + bnew_ref[g, s][:, h:h + 1]
            for s, rows in enumerate(new_rows[g])] for h, g in chains]
    m = [functools.reduce(jnp.maximum, [jnp.max(lp[c], axis=-1, keepdims=True)] + lns[c]) for c in ids]
    pp = [jnp.exp(lp[c] - m[c]) for c in ids]
    pn = [[jnp.exp(ln - m[c]) for ln in lns[c]] for c in ids]
    den = [jnp.sum(pp[c], axis=-1, keepdims=True) + sum(pn[c]) for c in ids]
    o = [_dot_nt(pp[c].astype(BF16), groups[g][1][1, h].astype(BF16)) for c, (h, g) in enumerate(chains)]
    o = [o[c] + sum(z.astype(BF16).astype(F32) * rows[:, vs(h)] for z, rows in zip(pn[c], new_rows[g]))
         for c, (h, g) in enumerate(chains)]
    for h in range(GROUP_HEADS):
        cs = [h * N_GROUPS + g for g in range(N_GROUPS)]
        mx = functools.reduce(jnp.maximum, [m[c] for c in cs])
        wts = [jnp.exp(m[c] - mx) * den[c] for c in cs]
        num = sum(w * (o[c] / den[c]) for w, c in zip(wts, cs))
        o_ref[:, ks(h)] = num / sum(wts)


def _sample_attn(q, kvn, caches, bias_past, bias_new, n_new):
    b, tp, _ = q.shape
    new_spec = pl.BlockSpec((None, tp, 2 * GROUP_W), lambda bi: (bi, 0, 0))
    cache_spec = lambda w: pl.BlockSpec((None, 2, GROUP_HEADS, HEAD_DIM, w), lambda bi: (bi, 0, 0, 0, 0))
    bias_spec = lambda w: pl.BlockSpec((GROUP_HEADS, tp, w), lambda bi: (0, 0, 0))
    return pl.pallas_call(
        functools.partial(_sample_attn_kernel, n_new=n_new),
        grid=(b,),
        in_specs=[pl.BlockSpec((None, tp, A_Q), lambda bi: (bi, 0, 0)), new_spec, new_spec, new_spec]
                 + [cache_spec(w) for w in WINDOWS] + [bias_spec(w) for w in WINDOWS]
                 + [pl.BlockSpec((N_GROUPS, n_new, tp, 128), lambda bi: (0, 0, 0, 0))],
        out_specs=pl.BlockSpec((None, tp, GROUP_W), lambda bi: (bi, 0, 0)),
        out_shape=jax.ShapeDtypeStruct((b, tp, GROUP_W), F32),
        compiler_params=_params("parallel"),
        name="sample_attn",
    )(q, kvn[0], kvn[1], kvn[2], *caches, *bias_past, bias_new)


def _cross_attn(q_ref, kvt_ref, dst_ref, nb):
    tq = q_ref.shape[0] // nb
    sub = min(tq, Q_TILE)
    chains = [(bi, slice(r0, r0 + sub), h) for bi in range(nb) for r0 in range(bi * tq, (bi + 1) * tq, sub)
              for h in range(MEM_W // HEAD_DIM)]
    kvt = [kvt_ref[bi].astype(BF16) for bi in range(nb)]
    pair = lambda h, off=0: slice(off + (h // 2) * 128, off + (h // 2 + 1) * 128)
    lo_lane = lax.broadcasted_iota(jnp.int32, (sub, 128), 1) < HEAD_DIM
    qm = [jnp.where(lo_lane if h % 2 == 0 else jnp.logical_not(lo_lane), q_ref[rw, pair(h)] * SCALE, 0.0).astype(BF16)
          for bi, rw, h in chains]
    lg = [_dot(qm[c], kvt[bi][pair(h), :]) for c, (bi, rw, h) in enumerate(chains)]
    p = [jnp.exp(z - jnp.max(z, axis=-1, keepdims=True)) for z in lg]
    den = [jnp.sum(z, axis=-1, keepdims=True) for z in p]
    o = [_dot_nt(p[c].astype(BF16), kvt[bi][pair(h, MEM_W), :]) / den[c] for c, (bi, rw, h) in enumerate(chains)]
    for c, (bi, rw, h) in enumerate(chains):
        if h % 2 == 1:
            dst_ref[rw, pair(h)] = jnp.where(lo_lane, o[c - 1], o[c])


def _rwkv_kernel(cols_ref, shift_ref, s0_ref, mu_ref, w0_ref, wup_ref, a0_ref, aup_ref, kk_ref, ka_ref, rk_ref,
                 lnw_ref, lnb_ref, y_ref, sout_ref, shout_ref,
                 carry, state, r_s, k_s, v_s, a_s, b_s, lw_s, y_s, *, tt, chunk, t_valid):
    j = pl.program_id(1)
    nb, t_in = cols_ref.shape[0], cols_ref.shape[1]
    zero_head = jnp.zeros((HEAD_DIM, HEAD_DIM), F32)

    @pl.when(j == 0)
    def _():
        carry[...] = shift_ref[...]
        for bi in range(nb):
            for p in range(RWKV_PAIRS):
                state[bi * RWKV_PAIRS + p] = jnp.concatenate(
                    [jnp.concatenate([s0_ref[bi, 2 * p], zero_head], axis=1),
                     jnp.concatenate([zero_head, s0_ref[bi, 2 * p + 1]], axis=1)], axis=0)

    direct = nb == 1 and t_in == tt
    y_dst = y_ref.at[0] if direct else y_s
    row = lax.broadcasted_iota(jnp.int32, (tt, 1), 0)
    live = row < t_valid
    blockdiag = (lax.broadcasted_iota(jnp.int32, (128, 128), 0) // HEAD_DIM
                 == lax.broadcasted_iota(jnp.int32, (128, 128), 1) // HEAD_DIM)

    def head_sum(x):
        lo = lax.broadcasted_iota(jnp.int32, x.shape, 1) < HEAD_DIM
        s0 = jnp.sum(jnp.where(lo, x, 0.0), axis=-1, keepdims=True)
        s1 = jnp.sum(jnp.where(lo, 0.0, x), axis=-1, keepdims=True)
        return jnp.where(lo, s0, s1)

    for bi in range(nb):
        base = slice(bi * tt, (bi + 1) * tt)
        cols = cols_ref[bi]
        if t_in < tt:
            cols = jnp.concatenate([cols, jnp.zeros((tt - t_in, cols.shape[1]), F32)], axis=0)
        prev = jnp.where(row == 0, carry[bi], pltpu.roll(cols, 1, axis=0))
        carry[bi] = cols[t_valid - 1:t_valid, :]
        xs = cols + mu_ref[...] * (prev - cols)
        r = xs[:, :RWKV_W]
        k = xs[:, RWKV_W:2 * RWKV_W]
        v = xs[:, 2 * RWKV_W:3 * RWKV_W]
        wd = xs[:, 3 * RWKV_W:3 * RWKV_W + LORA]
        ad = xs[:, 3 * RWKV_W + LORA:]
        z = w0_ref[...] + _dot(jnp.tanh(wd).astype(BF16), wup_ref[...])
        log_decay = -math.exp(-0.5) * jax.nn.sigmoid(z)
        a = jax.nn.sigmoid(a0_ref[...] + _dot(ad.astype(BF16), aup_ref[...]))
        kk = k * kk_ref[...]
        kk_sq = kk * kk
        k2 = k * (1.0 + (a - 1.0) * ka_ref[...])
        for p in range(RWKV_PAIRS):
            ps = slice(p * 128, (p + 1) * 128)
            nrm = jnp.maximum(jnp.sqrt(head_sum(kk_sq[:, ps])), 1e-12)
            kkn = kk[:, ps] / nrm
            if t_valid < tt:
                zero = jnp.zeros((tt, 128), F32)
                r_s[base, ps] = r[:, ps]
                k_s[base, ps] = jnp.where(live, k2[:, ps], zero)
                v_s[base, ps] = jnp.where(live, v[:, ps], zero)
                a_s[base, ps] = jnp.where(live, -kkn, zero)
                b_s[base, ps] = jnp.where(live, kkn * a[:, ps], zero)
                lw_s[base, ps] = jnp.where(live, log_decay[:, ps], zero)
            else:
                r_s[base, ps] = r[:, ps]
                k_s[base, ps] = k2[:, ps]
                v_s[base, ps] = v[:, ps]
                a_s[base, ps] = -kkn
                b_s[base, ps] = kkn * a[:, ps]
                lw_s[base, ps] = log_decay[:, ps]

    ci = lax.broadcasted_iota(jnp.int32, (chunk, chunk), 0)
    cj = lax.broadcasted_iota(jnp.int32, (chunk, chunk), 1)
    tri_incl = (ci >= cj).astype(BF16)
    lo_lane = lax.broadcasted_iota(jnp.int32, (chunk, 128), 1) < HEAD_DIM
    levels = max(1, math.ceil(math.log2(min(chunk, t_valid))))
    n2 = 2 * chunk
    ri = lax.broadcasted_iota(jnp.int32, (n2, n2), 0)
    rj = lax.broadcasted_iota(jnp.int32, (n2, n2), 1)
    same = ri // chunk == rj // chunk
    strict = jnp.logical_and(same, ri > rj)
    eye = (ri == rj).astype(F32)
    incl = jnp.logical_and(same, ri >= rj)
    own = lax.broadcasted_iota(jnp.int32, (n2, 128), 0) // chunk == lax.broadcasted_iota(jnp.int32, (n2, 128), 1) // HEAD_DIM
    dup = lambda z: jnp.concatenate([z, z], axis=0)
    cat = jnp.concatenate

    n_chunks = tt // chunk
    group = SCAN_INTERLEAVE if n_chunks % SCAN_INTERLEAVE == 0 else 1

    def chunk_body(ci, _):
        rows = [pl.ds(bi * tt + (ci * group + cc) * chunk, chunk) for cc in range(group) for bi in range(nb)]
        sl = [slice(p * 128, (p + 1) * 128) for p in range(RWKV_PAIRS)]
        chains = [(rw, s) for rw in rows for s in sl]
        ids = range(len(chains))
        per_chunk = nb * RWKV_PAIRS
        cum_all = [_mm(tri_incl, lw_s[rw, :], pb=3) for rw in rows]
        cum = [z[:, s] for z in cum_all for s in sl]
        p_incl = [jnp.exp(z) for z in cum]
        p_inv = [jnp.exp(-z) for z in cum]
        rr = [r_s[rw, s] for rw, s in chains]
        kc = [k_s[rw, s] for rw, s in chains]
        vc = [v_s[rw, s] for rw, s in chains]
        at2 = [jnp.where(own, dup(a_s[rw, s] * jnp.exp(cum[c] - lw_s[rw, s])), 0.0) for c, (rw, s) in enumerate(chains)]
        rt2 = [jnp.where(own, dup(rr[c] * p_incl[c]), 0.0) for c in ids]
        bt = [(b_s[rw, s] * p_inv[c]).astype(BF16) for c, (rw, s) in enumerate(chains)]
        kt = [(kc[c] * p_inv[c]).astype(BF16) for c in ids]
        vb = [z.astype(BF16) for z in vc]
        v2 = [dup(z) for z in vb]
        g = [_mm(cat([at2[c], rt2[c]], 0).astype(BF16), cat([dup(bt[c]), dup(kt[c])], 0), NT) for c in ids]
        a_ak = [jnp.where(strict, z[:n2, n2:], 0.0).astype(BF16) for z in g]
        apow = [jnp.where(strict, z[:n2, :n2], 0.0).astype(BF16) for z in g]
        a_r = [cat([jnp.where(incl, z[n2:, :n2], 0.0), jnp.where(incl, z[n2:, n2:], 0.0)], 1).astype(BF16)
               for z in g]
        akv = [_mm(a_ak[c], v2[c]) for c in ids]
        tinv = [eye + jnp.where(strict, z[:n2, :n2], 0.0) for z in g]
        for _ in range(levels - 1):
            apow = [_mm(z, z).astype(BF16) for z in apow]
            tinv = [tinv[c] + _mm(tinv[c].astype(BF16), apow[c]) for c in ids]
        sol = [_mm(tinv[c].astype(BF16), cat([at2[c], akv[c]], axis=1).astype(BF16)) for c in ids]
        ws = [z[:, :128] for z in sol]
        u0s = [jnp.where(own, z[:, 128:], 0.0) for z in sol]
        zeros2 = jnp.zeros((n2, 128), BF16)
        qy = [_mm(a_r[c], cat([cat([ws[c], u0s[c]], 1).astype(BF16), cat([zeros2, v2[c]], 1)], 0)) for c in ids]
        qs = [(rt2[c] + qy[c][:, :128]).astype(BF16) for c in ids]
        zeros1 = jnp.zeros((chunk, 128), BF16)
        lhs = [cat([cat([ws[c][:chunk] + ws[c][chunk:], u0s[c][:chunk] + u0s[c][chunk:]], 1).astype(BF16),
                    cat([zeros1, vb[c]], 1)], 0) for c in ids]
        wbn = [_mm(lhs[c], cat([bt[c], kt[c]], 0), TN) for c in ids]
        wb = [z[:128].astype(BF16) for z in wbn]
        s_cur = [state[q] for q in range(per_chunk)]
        ys = []
        for cc in range(group):
            base = cc * per_chunk
            s_b = [z.astype(BF16) for z in s_cur]
            ys += [_mm(qs[base + q], s_b[q], NT) + qy[base + q][:, 128:] for q in range(per_chunk)]
            sw = [_mm(s_b[q], wb[base + q]) for q in range(per_chunk)]
            s_cur = [jnp.where(blockdiag, (s_cur[q] + sw[q] + wbn[base + q][128:])
                               * p_incl[base + q][chunk - 1:chunk, :], 0.0) for q in range(per_chunk)]
        for q in range(per_chunk):
            state[q] = s_cur[q]
        y = [jnp.where(lo_lane, z[:chunk], z[chunk:]) for z in ys]

        bonus = [head_sum(rr[c] * kc[c] * rk_ref[:, s]) for c, (_, s) in enumerate(chains)]
        dlt = [y[c] - head_sum(y[c]) * (1.0 / HEAD_DIM) for c in ids]
        var = [head_sum(z * z) * (1.0 / HEAD_DIM) for z in dlt]
        for c, (rw, s) in enumerate(chains):
            yn = dlt[c] * lax.rsqrt(var[c] + GN_EPS) * lnw_ref[:, s] + lnb_ref[:, s]
            y_dst[rw, s] = yn + bonus[c] * vc[c]
        return 0

    for ci in range(n_chunks // group):
        chunk_body(ci, 0)
    if not direct:
        for bi in range(nb):
            y_ref[bi] = y_s[bi * tt:bi * tt + t_in, :]

    @pl.when(j == pl.num_programs(1) - 1)
    def _():
        for bi in range(nb):
            for p in range(RWKV_PAIRS):
                sout_ref[bi, 2 * p] = state[bi * RWKV_PAIRS + p, 0:HEAD_DIM, 0:HEAD_DIM]
                sout_ref[bi, 2 * p + 1] = state[bi * RWKV_PAIRS + p, HEAD_DIM:, HEAD_DIM:]
        shout_ref[...] = carry[...]


def _rwkv_scan(cols, shift_prev, s0, prm, tt, t_valid, nb):
    b, t, _ = cols.shape
    t_blk = min(t, tt)
    heads = pl.BlockSpec((nb, 2 * RWKV_PAIRS, HEAD_DIM, HEAD_DIM), lambda bi, j: (bi, 0, 0, 0))
    row = lambda n: pl.BlockSpec((1, n), lambda bi, j: (0, 0))
    vec = lambda a: a.reshape(1, -1)
    y, s_out, sh_out = pl.pallas_call(
        functools.partial(_rwkv_kernel, tt=tt, chunk=SCAN_CHUNK, t_valid=min(t_valid, tt)),
        grid=(b // nb, t // t_blk),
        in_specs=[pl.BlockSpec((nb, t_blk, C_SHIFT), lambda bi, j: (bi, j, 0)),
                  pl.BlockSpec((nb, 1, C_SHIFT), lambda bi, j: (bi, 0, 0)),
                  heads,
                  row(C_SHIFT), row(RWKV_W),
                  pl.BlockSpec((LORA, RWKV_W), lambda bi, j: (0, 0)),
                  row(RWKV_W),
                  pl.BlockSpec((LORA, RWKV_W), lambda bi, j: (0, 0)),
                  row(RWKV_W), row(RWKV_W), row(RWKV_W), row(RWKV_W), row(RWKV_W)],
        out_specs=[pl.BlockSpec((nb, t_blk, RWKV_W), lambda bi, j: (bi, j, 0)),
                   heads,
                   pl.BlockSpec((nb, 1, C_SHIFT), lambda bi, j: (bi, 0, 0))],
        out_shape=[jax.ShapeDtypeStruct((b, t, RWKV_W), F32),
                   jax.ShapeDtypeStruct((b, 2 * RWKV_PAIRS, HEAD_DIM, HEAD_DIM), F32),
                   jax.ShapeDtypeStruct((b, 1, C_SHIFT), F32)],
        scratch_shapes=[pltpu.VMEM((nb, 1, C_SHIFT), F32), pltpu.VMEM((nb * RWKV_PAIRS, 128, 128), F32)]
                       + [pltpu.VMEM((nb * tt, RWKV_W), F32)] * 7,
        compiler_params=_params("parallel", "arbitrary"),
        name="rwkv_scan",
    )(cols, shift_prev.reshape(b, 1, C_SHIFT), s0, vec(prm["mu"]), vec(prm["w0"]), prm["w_up"].astype(BF16),
      vec(prm["a0"]), prm["a_up"].astype(BF16), vec(prm["k_k"]), vec(prm["k_a"]), vec(prm["r_k"]),
      vec(prm["ln_w"]), vec(prm["ln_b"]))
    return y, s_out, sh_out.reshape(b, C_SHIFT)


def _post_kernel(*refs, merge_dils, mix_w, nb, recompute):
    x_ref, a_ref, b_ref, kvt_ref, w_ref, g_ref = refs[:6]
    mix_refs = refs[6:-4]
    y_ref, stage, omem, qmem = refs[-4:]
    tm = x_ref.shape[0]
    n_parts = POST_PARTS if tm % (8 * POST_PARTS) == 0 and tm // POST_PARTS >= 64 else 1
    parts = [slice(i * tm // n_parts, (i + 1) * tm // n_parts) for i in range(n_parts)]
    if recompute:
        xn = [_rms(x_ref[rs, :], a_ref[...]).astype(BF16) for rs in parts]
        for rs, z in zip(parts, xn):
            qmem[rs, :] = _dot(z, b_ref[:, :MEM_W])
        gate = [_dot(z, b_ref[:, MEM_W:]) for z in xn]
        _cross_attn(qmem, kvt_ref, omem, nb)
    else:
        gate = [b_ref[rs, :] for rs in parts]
        _cross_attn(a_ref, kvt_ref, omem, nb)
    if merge_dils:
        er = lax.broadcasted_iota(jnp.int32, (128, 2 * GROUP_W), 0)
        ec = lax.broadcasted_iota(jnp.int32, (128, 2 * GROUP_W), 1)
        expand = (er == ec // HEAD_DIM).astype(BF16)
        accs, stats = [], []
        for g, dil in enumerate(merge_dils):
            acc_ref, st_ref = mix_refs[2 * g], mix_refs[2 * g + 1]
            accs.append(acc_ref[...] if dil == 1 else _from_classes(acc_ref, stage, dil, GROUP_W))
            stats.append(st_ref[...] if dil == 1 else _from_classes(st_ref, stage, dil, 128))
        st = [[_mm(z[rs, :], expand, pa=3) for z in stats] for rs in parts]
        mx = [functools.reduce(jnp.maximum, [z[:, :GROUP_W] for z in sp]) for sp in st]
        wts = [[jnp.exp(z[:, :GROUP_W] - mx[i]) * z[:, GROUP_W:] for z in sp] for i, sp in enumerate(st)]
        mix = [sum(w * (a[rs, :] / z[:, GROUP_W:]) for w, a, z in zip(wts[i], accs, st[i])) / sum(wts[i])
               for i, rs in enumerate(parts)]
    else:
        mix = [mix_refs[0][rs, :] for rs in parts]
    act = [z * jax.nn.sigmoid(z) for z in gate]
    h1 = [(mix[i] * act[i][:, :mix_w]).astype(BF16) for i in range(n_parts)]
    h2 = [(omem[rs, :] * act[i][:, mix_w:]).astype(BF16) for i, rs in enumerate(parts)]
    out = [_dot(h1[i], w_ref[:mix_w, :]) + _dot(h2[i], w_ref[mix_w:, :]) for i in range(n_parts)]
    for i, rs in enumerate(parts):
        y_ref[rs, :] = x_ref[rs, :] + _rms(out[i], g_ref[...])


def _post(x, mix, qg, kvt, layer, w_out, g_post, tm, rows_per_batch, recompute):
    m, d = x.shape
    merge = isinstance(mix, (list, tuple))
    rows = lambda wd: pl.BlockSpec((tm, wd), lambda i: (i, 0))
    if recompute:
        gw = qg[1].shape[1] - MEM_W
        qg_arrays = [qg[0].reshape(1, d), qg[1].astype(BF16)]
        qg_specs = [pl.BlockSpec((1, d), lambda i: (0, 0)), pl.BlockSpec((d, MEM_W + gw), lambda i: (0, 0))]
    else:
        gw = qg[1].shape[1]
        qg_arrays, qg_specs = list(qg), [rows(MEM_W), rows(gw)]
    nb = max(1, tm // rows_per_batch)
    per_batch = max(1, rows_per_batch // tm)
    kvt_spec = pl.BlockSpec((None, nb, 2 * MEM_W, N_MEM), lambda i: (layer, i // per_batch, 0, 0))
    if merge:
        mix_w = GROUP_W
        merge_dils = tuple(acc.shape[2] // GROUP_W for acc, _ in mix)
        tiles = mix[0][0].shape[1] * merge_dils[0] // tm
        mix_arrays = [a for pair in mix for a in pair]
        mix_specs = [pl.BlockSpec((None, tm // dil, a.shape[2]), lambda i: (i // tiles, i % tiles, 0))
                     for dil, pair in zip(merge_dils, mix) for a in pair]
    else:
        mix_w, merge_dils, mix_arrays, mix_specs = mix.shape[1], (), [mix], [rows(mix.shape[1])]
    return pl.pallas_call(
        functools.partial(_post_kernel, merge_dils=merge_dils, mix_w=mix_w, nb=nb, recompute=recompute),
        grid=(m // tm,),
        in_specs=[rows(d)] + qg_specs + [
                  kvt_spec,
                  pl.BlockSpec((gw, d), lambda i: (0, 0)),
                  pl.BlockSpec((1, d), lambda i: (0, 0))] + mix_specs,
        out_specs=rows(d),
        out_shape=jax.ShapeDtypeStruct((m, d), F32),
        scratch_shapes=[pltpu.VMEM((tm, 128), F32), pltpu.VMEM((tm, MEM_W), F32), pltpu.VMEM((tm, MEM_W), F32)],
        compiler_params=_params("parallel"),
        name="post",
    )(x, *qg_arrays, kvt, w_out.astype(BF16), g_post.reshape(1, d), *mix_arrays)


def _t5_bucket(dist):
    max_exact = N_BUCKETS // 2
    d = jnp.maximum(dist, 1).astype(F32)
    large = max_exact + (jnp.log(d / max_exact) / math.log(MAX_DISTANCE / max_exact)
                         * (N_BUCKETS - max_exact)).astype(jnp.int32)
    large = jnp.minimum(large, N_BUCKETS - 1)
    return jnp.where(dist < max_exact, dist, large)


def _group_bias(rel_bias, g):
    dist = DILATIONS[g] * jnp.arange(A_KEYS, dtype=jnp.int32)
    bias = rel_bias[_t5_bucket(dist)]
    return bias[:, g * GROUP_HEADS:(g + 1) * GROUP_HEADS].T.astype(F32)


def _bias_tiles_kernel(x_ref, o_ref):
    row = lax.broadcasted_iota(jnp.int32, (Q_TILE, 2 * Q_TILE), 0)
    for h in range(o_ref.shape[0]):
        t = jnp.broadcast_to(x_ref[h:h + 1, :], (Q_TILE, 2 * Q_TILE))
        for bit in range(int(math.log2(Q_TILE))):
            t = jnp.where((row >> bit) & 1 == 1, pltpu.roll(t, 1 << bit, axis=1), t)
        o_ref[h] = t


def _prompt_bias(biases):
    bias = jnp.concatenate(biases, axis=0)
    neg = jnp.full((bias.shape[0], Q_TILE - 1), NEG_INF, F32)
    table = jnp.concatenate([bias[:, :1], neg, bias[:, :0:-1]], axis=1)
    n = table.shape[0]
    return pl.pallas_call(
        _bias_tiles_kernel,
        out_shape=jax.ShapeDtypeStruct((n, Q_TILE, 2 * Q_TILE), F32),
        name="bias_tiles",
    )(table)


def _sample_bias(biases, n_new, tp):
    t = np.arange(tp)[:, None]
    real = t < n_new
    past = []
    for g, bias in enumerate(biases):
        w, dil = WINDOWS[g], DILATIONS[g]
        spread = jnp.concatenate([bias[:, :0:-1, None], jnp.full((GROUP_HEADS, w // dil, dil - 1), NEG_INF, F32)],
                                 axis=-1).reshape(GROUP_HEADS, w)
        rows = [jnp.concatenate([jnp.full((GROUP_HEADS, tt), NEG_INF, F32), spread[:, :w - tt]], axis=-1)
                for tt in range(n_new)] + [jnp.zeros((GROUP_HEADS, w), F32)] * (tp - n_new)
        past.append(jnp.stack(rows, axis=1))
    s = np.arange(n_new)[None, :, None]
    g_idx = np.arange(N_GROUPS)[:, None, None]
    ok = (s <= t[None, :, 0]) & real[None, :, 0] & ((g_idx == 0) | (s == 0))
    vals = jnp.stack([bias[:, :n_new].T for bias in biases])
    new = jnp.where(jnp.asarray(ok)[..., None], vals[:, :, None, :], NEG_INF)
    return past, jnp.pad(new, ((0, 0), (0, 0), (0, 0), (0, 128 - GROUP_HEADS)))


def _pieces_a():
    q = [((g * GROUP_W, (g + 1) * GROUP_W),) for g in range(N_GROUPS)]
    kv = [((A_Q + g * GROUP_W, A_Q + (g + 1) * GROUP_W), (2 * A_Q + g * GROUP_W, 2 * A_Q + (g + 1) * GROUP_W))
          for g in range(N_GROUPS)]
    qmem = ((3 * A_Q, 3 * A_Q + MEM_W),)
    gate = ((3 * A_Q + MEM_W, 3 * A_Q + MEM_W + GROUP_W + MEM_W),)
    return tuple(q + kv + [qmem, gate])


def _pieces_b():
    return (((0, C_SHIFT),),)


def kernel(x_prompt, x_sample, mem_prompt, cache_mem_kv, cache_win0, cache_win1, cache_win2, state_wkv, state_shift, norm_pre, norm_post, norm_mem, w_mem_kv, rel_bias, w_in_a, w_out_a, w_in_b, w_out_b, rwkv_mu, rwkv_w0, rwkv_w_up, rwkv_a0, rwkv_a_up, rwkv_k_k, rwkv_k_a, rwkv_r_k, rwkv_ln_w, rwkv_ln_b):
    bp, tp, d = x_prompt.shape
    bs, ts, _ = x_sample.shape
    tsp = SAMPLE_PAD_T
    xp = x_prompt.reshape(bp * tp, d)
    xs = jnp.pad(x_sample, ((0, 0), (0, tsp - ts), (0, 0))).reshape(bs * tsp, d)
    tm_p, tm_s = 512, bs * tsp
    tm_post_s = 8 * tsp
    time_minor = lambda c: jnp.moveaxis(c, -4, -1)
    caches = [time_minor(c[0]) for c in (cache_win0, cache_win1, cache_win2)]
    mkv_s = time_minor(cache_mem_kv).reshape(cache_mem_kv.shape[0], bs, 2 * MEM_W, N_MEM)
    mkv_p = _mem_kv(mem_prompt, norm_mem, w_mem_kv)

    biases = [_group_bias(rel_bias, g) for g in range(N_GROUPS)]
    w_gq_b, w_mix_b = w_in_b[0][:, C_SHIFT:], w_in_b[0][:, :C_SHIFT]
    outs_p = _norm_proj(xp, norm_pre[0], w_in_a[0], _pieces_a(), tm_p, transposed=(3, 4, 5),
                        dils=DILATIONS + DILATIONS + (1, 1), rows_per_batch=tp)
    outs_s = _norm_proj(xs, norm_pre[0], w_in_a[0], _pieces_a(), tm_s)
    q_p, kv_p, qg_p, kvt_p = outs_p[0:3], outs_p[3:6], outs_p[6:8], outs_p[8:11]
    q_s, kv_s, qg_s = outs_s[0:3], outs_s[3:6], outs_s[6:8]

    bias_tiles = _prompt_bias(biases)
    merged = [_dil_attn(q_p[g].reshape(bp, tp // DILATIONS[g], -1), kv_p[g].reshape(bp, tp // DILATIONS[g], -1),
                        bias_tiles, g) for g in range(N_GROUPS)]
    xp1 = _post(xp, merged, qg_p, mkv_p, 0, w_out_a[0], norm_post[0], tm_p, tp, False)

    bias_past, bias_new = _sample_bias(biases, ts, tsp)
    q_s_all = jnp.concatenate(q_s, axis=-1).reshape(bs, tsp, A_Q)
    kvn = [a.reshape(bs, tsp, 2 * GROUP_W) for a in kv_s]
    o_s = _sample_attn(q_s_all, kvn, caches, bias_past, bias_new, ts)
    xs1 = _post(xs, o_s.reshape(bs * tsp, GROUP_W), qg_s, mkv_s, 0, w_out_a[0], norm_post[0], tm_post_s, tsp, False)

    (cols_p,) = _norm_proj(xp1, norm_pre[1], w_mix_b, _pieces_b(), 2 * tm_p)
    (cols_s,) = _norm_proj(xs1, norm_pre[1], w_mix_b, _pieces_b(), tm_s)
    prm = dict(mu=rwkv_mu[0], w0=rwkv_w0[0], w_up=rwkv_w_up[0], a0=rwkv_a0[0], a_up=rwkv_a_up[0], k_k=rwkv_k_k[0],
               k_a=rwkv_k_a[0], r_k=rwkv_r_k[0], ln_w=rwkv_ln_w[0], ln_b=rwkv_ln_b[0])
    y_p, wkv_p, sh_p = _rwkv_scan(cols_p.reshape(bp, tp, C_SHIFT), jnp.zeros((bp, C_SHIFT), F32),
                                  jnp.zeros((bp, 2 * RWKV_PAIRS, HEAD_DIM, HEAD_DIM), F32), prm, 512, tp, 1)
    y_s, wkv_s, sh_s = _rwkv_scan(cols_s.reshape(bs, tsp, C_SHIFT), state_shift[0], state_wkv[0], prm, SCAN_CHUNK,
                                  ts, 4)
    y_s = y_s.reshape(bs * tsp, RWKV_W)
    qg_b = (norm_pre[1], w_gq_b)
    xp2 = _post(xp1, y_p.reshape(bp * tp, RWKV_W), qg_b, mkv_p, 1, w_out_b[0], norm_post[1], tm_p, tp, True)
    xs2 = _post(xs1, y_s, qg_b, mkv_s, 1, w_out_b[0], norm_post[1], tm_post_s, tsp, True)

    kv_shape = (2, GROUP_HEADS, HEAD_DIM)
    time_major = lambda c: jnp.moveaxis(c, -1, -4)
    new_mem_kv = time_major(mkv_p.reshape(mkv_p.shape[0], bp, 2, MEM_W // HEAD_DIM, HEAD_DIM, N_MEM))
    win_p = [time_major(kvt_p[g].reshape(bp, *kv_shape, tp)[..., tp - min(WINDOWS[g], tp):])[None]
             for g in range(N_GROUPS)]
    win_s = [kv_s[g].reshape(bs, tsp, *kv_shape)[None, :, :ts] for g in range(N_GROUPS)]
    return (xp2.reshape(bp, tp, d), xs2.reshape(bs, tsp, d)[:, :ts], new_mem_kv,
            win_p[0], win_p[1], win_p[2], win_s[0], win_s[1], win_s[2],
            wkv_p[None], wkv_s[None], sh_p[None], sh_s[None])
```

```python
import functools
import math

import numpy as np
import jax
import jax.numpy as jnp
from jax import lax
from jax.experimental import pallas as pl
from jax.experimental.pallas import tpu as pltpu

F32 = jnp.float32
BF16 = jnp.bfloat16

D_MODEL = 1024
HEAD_DIM = 64
N_GROUPS = 3
GROUP_HEADS = 4
WINDOWS = (128, 512, 2048)
DILATIONS = (1, 4, 16)
A_HEADS = N_GROUPS * GROUP_HEADS
A_KEYS = 129
GROUP_W = GROUP_HEADS * HEAD_DIM
A_Q = A_HEADS * HEAD_DIM
N_MEM = 256
MEM_W = 256
RWKV_W = 768
RWKV_PAIRS = RWKV_W // 128
LORA = 64
C_SHIFT = 3 * RWKV_W + 2 * LORA
N_BUCKETS = 32
MAX_DISTANCE = WINDOWS[-1]
RMS_EPS = 1e-6
GN_EPS = 64e-5
NEG_INF = -1e30
SCALE = HEAD_DIM ** -0.5

Q_TILE = 128
DIL_SEGMENTS = 16
SCAN_CHUNK = 64
POST_PARTS = 2
SCAN_INTERLEAVE = 2
SAMPLE_PAD_T = 8
VMEM_LIMIT = 56 * 1024 * 1024
NN =(((1,), (0,)), ((), ()))
NT = (((1,), (1,)), ((), ()))
TN = (((0,), (0,)), ((), ()))


def _params(*sem):
    return pltpu.CompilerParams(dimension_semantics=sem, vmem_limit_bytes=VMEM_LIMIT)


def _dot(a, b):
    return jnp.dot(a, b, preferred_element_type=F32)


def _dot_nt(a, b):
    return lax.dot_general(a, b, NT, preferred_element_type=F32)


def _split(x, n):
    if x.dtype == BF16:
        return [x]
    parts, rem = [], x
    for i in range(n):
        parts.append(rem.astype(BF16))
        if i + 1 < n:
            rem = rem - parts[-1].astype(F32)
    return parts


def _mm(a, b, dims=NN, pa=1, pb=1):
    pas, pbs = _split(a, pa), _split(b, pb)
    order = max(len(pas), len(pbs))
    out = None
    for i, ai in enumerate(pas):
        for j, bj in enumerate(pbs):
            if i + j < order:
                term = lax.dot_general(ai, bj, dims, preferred_element_type=F32)
                out = term if out is None else out + term
    return out


def _rms(x, g):
    return x * lax.rsqrt(jnp.mean(x * x, axis=-1, keepdims=True) + RMS_EPS) * g


def _to_classes(dst_ref, stage, val, dil, width, off):
    tm = val.shape[0]
    for j in range(val.shape[1] // 128):
        stage[...] = val[:, 128 * j:128 * (j + 1)]
        for r in range(dil):
            lane0 = r * width + off + 128 * j
            dst_ref[:, lane0:lane0 + 128] = stage[pl.ds(r, tm // dil, stride=dil), :]


def _from_classes(src_ref, stage, dil, width):
    tm = stage.shape[0]
    chunks = []
    for j in range(width // 128):
        for r in range(dil):
            lane0 = r * width + 128 * j
            stage[pl.ds(r, tm // dil, stride=dil), :] = src_ref[:, lane0:lane0 + 128]
        chunks.append(stage[...])
    return jnp.concatenate(chunks, axis=1)


def _norm_proj_kernel(x_ref, g_ref, w_ref, *refs, pieces, transposed, dils):
    stage = refs[-1]
    out_refs = refs[:-1]
    xn = _rms(x_ref[...], g_ref[...]).astype(BF16)
    t_refs = iter(out_refs[len(pieces):])
    for idx, (out_ref, cols) in enumerate(zip(out_refs, pieces)):
        t_ref = next(t_refs) if idx in transposed else None
        width = sum(c1 - c0 for c0, c1 in cols)
        off = 0
        for c0, c1 in cols:
            res = _dot(xn, w_ref[:, c0:c1])
            if dils[idx] > 1:
                _to_classes(out_ref, stage, res, dils[idx], width, off)
            else:
                out_ref[:, off:off + c1 - c0] = res
            if t_ref is not None:
                t_ref[off:off + c1 - c0, :] = res.T
            off += c1 - c0


def _norm_proj(x, g, w, pieces, tm, transposed=(), dils=None, rows_per_batch=None):
    m, d = x.shape
    n = w.shape[1]
    dils = tuple(dils) if dils else (1,) * len(pieces)
    widths = [sum(c1 - c0 for c0, c1 in cols) for cols in pieces]
    tiles = rows_per_batch // tm if rows_per_batch else None
    out_specs, out_shape = [], []
    for wd, dil in zip(widths, dils):
        if dil > 1:
            out_specs.append(pl.BlockSpec((None, tm // dil, dil * wd), lambda i: (i // tiles, i % tiles, 0)))
            out_shape.append(jax.ShapeDtypeStruct((m // rows_per_batch, rows_per_batch // dil, dil * wd), F32))
        else:
            out_specs.append(pl.BlockSpec((tm, wd), lambda i: (i, 0)))
            out_shape.append(jax.ShapeDtypeStruct((m, wd), F32))
    out_specs += [pl.BlockSpec((None, widths[idx], tm), lambda i: (i // tiles, 0, i % tiles)) for idx in transposed]
    out_shape += [jax.ShapeDtypeStruct((m // rows_per_batch, widths[idx], rows_per_batch), F32) for idx in transposed]
    return pl.pallas_call(
        functools.partial(_norm_proj_kernel, pieces=pieces, transposed=tuple(transposed), dils=dils),
        grid=(m // tm,),
        in_specs=[pl.BlockSpec((tm, d), lambda i: (i, 0)),
                  pl.BlockSpec((1, d), lambda i: (0, 0)),
                  pl.BlockSpec((d, n), lambda i: (0, 0))],
        out_specs=out_specs,
        out_shape=out_shape,
        scratch_shapes=[pltpu.VMEM((tm, 128), F32)],
        compiler_params=_params("parallel"),
        name="norm_proj",
    )(x, g.reshape(1, d), w.astype(BF16))


def _mem_kv_kernel(x_ref, g_ref, wt_ref, o_ref):
    xn = _rms(x_ref[...], g_ref[...]).astype(BF16)
    o_ref[...] = _dot_nt(wt_ref[...], xn)


def _mem_kv(mem, g, w):
    b, n, d = mem.shape
    nl, _, wd = w.shape
    return pl.pallas_call(
        _mem_kv_kernel,
        grid=(nl, b),
        in_specs=[pl.BlockSpec((None, n, d), lambda l, bi: (bi, 0, 0)),
                  pl.BlockSpec((None, 1, d), lambda l, bi: (l, 0, 0)),
                  pl.BlockSpec((None, wd, d), lambda l, bi: (l, 0, 0))],
        out_specs=pl.BlockSpec((None, None, wd, n), lambda l, bi: (l, bi, 0, 0)),
        out_shape=jax.ShapeDtypeStruct((nl, b, wd, n), F32),
        compiler_params=_params("parallel", "parallel"),
        name="mem_kv",
    )(mem, g.reshape(nl, 1, d), jnp.swapaxes(w, 1, 2).astype(BF16))


def _dil_attn_kernel(q_ref, kvc_ref, kvp_ref, bias_ref, acc_ref, st_ref, *, nr, tiles, prev_block):
    i = pl.program_id(2)
    qf = q_ref[...] * SCALE
    kvc = kvc_ref[...].astype(BF16)
    kvp = kvp_ref[...].astype(BF16) if prev_block else None
    chains = [(rr, tt, h) for rr in range(nr) for tt in range(tiles) for h in range(GROUP_HEADS)]
    n = len(chains)
    rows = lambda tt: slice(tt * Q_TILE, (tt + 1) * Q_TILE)
    lo_lane = lax.broadcasted_iota(jnp.int32, (Q_TILE, 128), 1) < HEAD_DIM

    def keys(rr, tt, h, off):
        lanes = slice(rr * 2 * GROUP_W + off + (h // 2) * 128, rr * 2 * GROUP_W + off + (h // 2 + 1) * 128)
        return kvp[:, lanes] if tt < 0 else kvc[rows(tt), lanes]

    has_prev = [tt > 0 or prev_block for _, tt, _ in chains]
    qh = [jnp.where(lo_lane if h % 2 == 0 else jnp.logical_not(lo_lane),
                    qf[rows(tt), rr * GROUP_W + (h // 2) * 128:rr * GROUP_W + (h // 2 + 1) * 128], 0.0).astype(BF16)
          for rr, tt, h in chains]
    lc = [_dot_nt(qh[c], keys(rr, tt, h, 0)) + bias_ref[h, :, :Q_TILE] for c, (rr, tt, h) in enumerate(chains)]
    lp = [None] * n
    for c, (rr, tt, h) in enumerate(chains):
        if has_prev[c]:
            z = _dot_nt(qh[c], keys(rr, tt - 1, h, 0)) + bias_ref[h, :, Q_TILE:]
            lp[c] = jnp.where(i > 0, z, NEG_INF) if tt == 0 else z
    m = [jnp.max(z, axis=-1, keepdims=True) for z in lc]
    m = [jnp.maximum(m[c], jnp.max(lp[c], axis=-1, keepdims=True)) if has_prev[c] else m[c] for c in range(n)]
    pc = [jnp.exp(lc[c] - m[c]) for c in range(n)]
    pp = [jnp.exp(lp[c] - m[c]) if has_prev[c] else None for c in range(n)]
    den = [jnp.sum(z, axis=-1, keepdims=True) for z in pc]
    den = [den[c] + jnp.sum(pp[c], axis=-1, keepdims=True) if has_prev[c] else den[c] for c in range(n)]
    o = [_dot(pc[c].astype(BF16), keys(rr, tt, h, GROUP_W)) for c, (rr, tt, h) in enumerate(chains)]
    o = [o[c] + _dot(pp[c].astype(BF16), keys(rr, tt - 1, h, GROUP_W)) if has_prev[c] else o[c]
         for c, (rr, tt, h) in enumerate(chains)]
    lane = lax.broadcasted_iota(jnp.int32, (Q_TILE, 128), 1)
    st = None
    for c, (rr, tt, h) in enumerate(chains):
        if h % 2 == 1:
            pair = slice(rr * GROUP_W + (h // 2) * 128, rr * GROUP_W + (h // 2 + 1) * 128)
            acc_ref[rows(tt), pair] = jnp.where(lo_lane, o[c - 1], o[c])
        st = jnp.zeros((Q_TILE, 128), F32) if h == 0 else st
        st = jnp.where(lane == h, m[c], st)
        st = jnp.where(lane == GROUP_HEADS + h, den[c], st)
        if h == GROUP_HEADS - 1:
            st_ref[rows(tt), rr * 128:(rr + 1) * 128] = st


def _dil_attn(qv, kvv, bias_tiles, g):
    dil = DILATIONS[g]
    b, ln, _ = qv.shape
    tiles = min(ln // Q_TILE, DIL_SEGMENTS)
    nr = min(dil, DIL_SEGMENTS // tiles)
    nblk = ln // (tiles * Q_TILE)
    return pl.pallas_call(
        functools.partial(_dil_attn_kernel, nr=nr, tiles=tiles, prev_block=nblk > 1),
        grid=(b, dil // nr, nblk),
        in_specs=[pl.BlockSpec((None, tiles * Q_TILE, nr * GROUP_W), lambda bi, r, i: (bi, i, r)),
                  pl.BlockSpec((None, tiles * Q_TILE, nr * 2 * GROUP_W), lambda bi, r, i: (bi, i, r)),
                  pl.BlockSpec((None, Q_TILE if nblk > 1 else 8, nr * 2 * GROUP_W),
                               lambda bi, r, i: (bi, jnp.maximum(tiles * i - 1, 0), r)),
                  pl.BlockSpec((GROUP_HEADS, Q_TILE, 2 * Q_TILE), lambda bi, r, i: (g, 0, 0))],
        out_specs=[pl.BlockSpec((None, tiles * Q_TILE, nr * GROUP_W), lambda bi, r, i: (bi, i, r)),
                   pl.BlockSpec((None, tiles * Q_TILE, nr * 128), lambda bi, r, i: (bi, i, r))],
        out_shape=[jax.ShapeDtypeStruct((b, ln, dil * GROUP_W), F32),
                   jax.ShapeDtypeStruct((b, ln, dil * 128), F32)],
        compiler_params=_params("parallel", "parallel", "arbitrary"),
        name=f"dil_attn_d{dil}",
    )(qv, kvv, kvv, bias_tiles)


def _sample_attn_kernel(q_ref, kn0_ref, kn1_ref, kn2_ref, c0_ref, c1_ref, c2_ref, b0_ref, b1_ref, b2_ref,
                        bnew_ref, o_ref, *, n_new):
    q = (q_ref[...] * SCALE).astype(BF16)
    qf = q.astype(F32)
    groups = ((kn0_ref, c0_ref, b0_ref), (kn1_ref, c1_ref, b1_ref), (kn2_ref, c2_ref, b2_ref))
    new_rows = []
    for g, (kn_ref, _, _) in enumerate(groups):
        kn = kn_ref[...]
        shifted = [kn] + [pltpu.roll(kn, s, axis=0) for s in range(1, n_new if g == 0 else 1)]
        new_rows.append([z.astype(BF16).astype(F32) for z in shifted])
    chains = [(h, g) for h in range(GROUP_HEADS) for g in range(N_GROUPS)]
    ids = range(len(chains))
    ks = lambda h: slice(h * HEAD_DIM, (h + 1) * HEAD_DIM)
    vs = lambda h: slice(GROUP_W + h * HEAD_DIM, GROUP_W + (h + 1) * HEAD_DIM)
    qs = lambda h, g: slice(g * GROUP_W + h * HEAD_DIM, g * GROUP_W + (h + 1) * HEAD_DIM)
    lp = [_dot(q[:, qs(h, g)], groups[g][1][0, h].astype(BF16)) + groups[g][2][h] for h, g in chains]
    lns = [[jnp.sum(qf[:, qs(h, g)] * rows[:, ks(h)], axis=-1, keepdims=True) + bnew_ref[g, s][:, h:h + 1]
            for s, rows in enumerate(new_rows[g])] for h, g in chains]
    m = [functools.reduce(jnp.maximum, [jnp.max(lp[c], axis=-1, keepdims=True)] + lns[c]) for c in ids]
    pp = [jnp.exp(lp[c] - m[c]) for c in ids]
    pn = [[jnp.exp(ln - m[c]) for ln in lns[c]] for c in ids]
    den = [jnp.sum(pp[c], axis=-1, keepdims=True) + sum(pn[c]) for c in ids]
    o = [_dot_nt(pp[c].astype(BF16), groups[g][1][1, h].astype(BF16)) for c, (h, g) in enumerate(chains)]
    o = [o[c] + sum(z.astype(BF16).astype(F32) * rows[:, vs(h)] for z, rows in zip(pn[c], new_rows[g]))
         for c, (h, g) in enumerate(chains)]
    for h in range(GROUP_HEADS):
        cs = [h * N_GROUPS + g for g in range(N_GROUPS)]
        mx = functools.reduce(jnp.maximum, [m[c] for c in cs])
        wts = [jnp.exp(m[c] - mx) * den[c] for c in cs]
        num = sum(w * (o[c] / den[c]) for w, c in zip(wts, cs))
        o_ref[:, ks(h)] = num / sum(wts)


def _sample_attn(q, kvn, caches, bias_past, bias_new, n_new):
    b, tp, _ = q.shape
    new_spec = pl.BlockSpec((None, tp, 2 * GROUP_W), lambda bi: (bi, 0, 0))
    cache_spec = lambda w: pl.BlockSpec((None, 2, GROUP_HEADS, HEAD_DIM, w), lambda bi: (bi, 0, 0, 0, 0))
    bias_spec = lambda w: pl.BlockSpec((GROUP_HEADS, tp, w), lambda bi: (0, 0, 0))
    return pl.pallas_call(
        functools.partial(_sample_attn_kernel, n_new=n_new),
        grid=(b,),
        in_specs=[pl.BlockSpec((None, tp, A_Q), lambda bi: (bi, 0, 0)), new_spec, new_spec, new_spec]
                 + [cache_spec(w) for w in WINDOWS] + [bias_spec(w) for w in WINDOWS]
                 + [pl.BlockSpec((N_GROUPS, n_new, tp, 128), lambda bi: (0, 0, 0, 0))],
        out_specs=pl.BlockSpec((None, tp, GROUP_W), lambda bi: (bi, 0, 0)),
        out_shape=jax.ShapeDtypeStruct((b, tp, GROUP_W), F32),
        compiler_params=_params("parallel"),
        name="sample_attn",
    )(q, kvn[0], kvn[1], kvn[2], *caches, *bias_past, bias_new)


def _cross_attn(q_ref, kvt_ref, dst_ref, nb):
    tq = q_ref.shape[0] // nb
    sub = min(tq, Q_TILE)
    chains = [(bi, slice(r0, r0 + sub), h) for bi in range(nb) for r0 in range(bi * tq, (bi + 1) * tq, sub)
              for h in range(MEM_W // HEAD_DIM)]
    kvt = [kvt_ref[bi].astype(BF16) for bi in range(nb)]
    pair = lambda h, off=0: slice(off + (h // 2) * 128, off + (h // 2 + 1) * 128)
    lo_lane = lax.broadcasted_iota(jnp.int32, (sub, 128), 1) < HEAD_DIM
    qm = [jnp.where(lo_lane if h % 2 == 0 else jnp.logical_not(lo_lane), q_ref[rw, pair(h)] * SCALE, 0.0).astype(BF16)
          for bi, rw, h in chains]
    lg = [_dot(qm[c], kvt[bi][pair(h), :]) for c, (bi, rw, h) in enumerate(chains)]
    p = [jnp.exp(z - jnp.max(z, axis=-1, keepdims=True)) for z in lg]
    den = [jnp.sum(z, axis=-1, keepdims=True) for z in p]
    o = [_dot_nt(p[c].astype(BF16), kvt[bi][pair(h, MEM_W), :]) / den[c] for c, (bi, rw, h) in enumerate(chains)]
    for c, (bi, rw, h) in enumerate(chains):
        if h % 2 == 1:
            dst_ref[rw, pair(h)] = jnp.where(lo_lane, o[c - 1], o[c])


def _rwkv_kernel(cols_ref, shift_ref, s0_ref, mu_ref, w0_ref, wup_ref, a0_ref, aup_ref, kk_ref, ka_ref, rk_ref,
                 lnw_ref, lnb_ref, y_ref, sout_ref, shout_ref,
                 carry, state, r_s, k_s, v_s, a_s, b_s, lw_s, y_s, *, tt, chunk, t_valid):
    j = pl.program_id(1)
    nb, t_in = cols_ref.shape[0], cols_ref.shape[1]
    zero_head = jnp.zeros((HEAD_DIM, HEAD_DIM), F32)

    @pl.when(j == 0)
    def _():
        carry[...] = shift_ref[...]
        for bi in range(nb):
            for p in range(RWKV_PAIRS):
                state[bi * RWKV_PAIRS + p] = jnp.concatenate(
                    [jnp.concatenate([s0_ref[bi, 2 * p], zero_head], axis=1),
                     jnp.concatenate([zero_head, s0_ref[bi, 2 * p + 1]], axis=1)], axis=0)

    direct = nb == 1 and t_in == tt
    y_dst = y_ref.at[0] if direct else y_s
    row = lax.broadcasted_iota(jnp.int32, (tt, 1), 0)
    live = row < t_valid
    blockdiag = (lax.broadcasted_iota(jnp.int32, (128, 128), 0) // HEAD_DIM
                 == lax.broadcasted_iota(jnp.int32, (128, 128), 1) // HEAD_DIM)

    def head_sum(x):
        lo = lax.broadcasted_iota(jnp.int32, x.shape, 1) < HEAD_DIM
        s0 = jnp.sum(jnp.where(lo, x, 0.0), axis=-1, keepdims=True)
        s1 = jnp.sum(jnp.where(lo, 0.0, x), axis=-1, keepdims=True)
        return jnp.where(lo, s0, s1)

    for bi in range(nb):
        base = slice(bi * tt, (bi + 1) * tt)
        cols = cols_ref[bi]
        if t_in < tt:
            cols = jnp.concatenate([cols, jnp.zeros((tt - t_in, cols.shape[1]), F32)], axis=0)
        prev = jnp.where(row == 0, carry[bi], pltpu.roll(cols, 1, axis=0))
        carry[bi] = cols[t_valid - 1:t_valid, :]
        xs = cols + mu_ref[...] * (prev - cols)
        r = xs[:, :RWKV_W]
        k = xs[:, RWKV_W:2 * RWKV_W]
        v = xs[:, 2 * RWKV_W:3 * RWKV_W]
        wd = xs[:, 3 * RWKV_W:3 * RWKV_W + LORA]
        ad = xs[:, 3 * RWKV_W + LORA:]
        z = w0_ref[...] + _dot(jnp.tanh(wd).astype(BF16), wup_ref[...])
        log_decay = -math.exp(-0.5) * jax.nn.sigmoid(z)
        a = jax.nn.sigmoid(a0_ref[...] + _dot(ad.astype(BF16), aup_ref[...]))
        kk = k * kk_ref[...]
        kk_sq = kk * kk
        k2 = k * (1.0 + (a - 1.0) * ka_ref[...])
        for p in range(RWKV_PAIRS):
            ps = slice(p * 128, (p + 1) * 128)
            nrm = jnp.maximum(jnp.sqrt(head_sum(kk_sq[:, ps])), 1e-12)
            kkn = kk[:, ps] / nrm
            if t_valid < tt:
                zero = jnp.zeros((tt, 128), F32)
                r_s[base, ps] = r[:, ps]
                k_s[base, ps] = jnp.where(live, k2[:, ps], zero)
                v_s[base, ps] = jnp.where(live, v[:, ps], zero)
                a_s[base, ps] = jnp.where(live, -kkn, zero)
                b_s[base, ps] = jnp.where(live, kkn * a[:, ps], zero)
                lw_s[base, ps] = jnp.where(live, log_decay[:, ps], zero)
            else:
                r_s[base, ps] = r[:, ps]
                k_s[base, ps] = k2[:, ps]
                v_s[base, ps] = v[:, ps]
                a_s[base, ps] = -kkn
                b_s[base, ps] = kkn * a[:, ps]
                lw_s[base, ps] = log_decay[:, ps]

    ci = lax.broadcasted_iota(jnp.int32, (chunk, chunk), 0)
    cj = lax.broadcasted_iota(jnp.int32, (chunk, chunk), 1)
    tri_incl = (ci >= cj).astype(BF16)
    lo_lane = lax.broadcasted_iota(jnp.int32, (chunk, 128), 1) < HEAD_DIM
    levels = max(1, math.ceil(math.log2(min(chunk, t_valid))))
    n2 = 2 * chunk
    ri = lax.broadcasted_iota(jnp.int32, (n2, n2), 0)
    rj = lax.broadcasted_iota(jnp.int32, (n2, n2), 1)
    same = ri // chunk == rj // chunk
    strict = jnp.logical_and(same, ri > rj)
    eye = (ri == rj).astype(F32)
    incl = jnp.logical_and(same, ri >= rj)
    own = lax.broadcasted_iota(jnp.int32, (n2, 128), 0) // chunk == lax.broadcasted_iota(jnp.int32, (n2, 128), 1) // HEAD_DIM
    dup = lambda z: jnp.concatenate([z, z], axis=0)
    cat = jnp.concatenate

    n_chunks = tt // chunk
    group = SCAN_INTERLEAVE if n_chunks % SCAN_INTERLEAVE == 0 else 1

    def chunk_body(ci, _):
        rows = [pl.ds(bi * tt + (ci * group + cc) * chunk, chunk) for cc in range(group) for bi in range(nb)]
        sl = [slice(p * 128, (p + 1) * 128) for p in range(RWKV_PAIRS)]
        chains = [(rw, s) for rw in rows for s in sl]
        ids = range(len(chains))
        per_chunk = nb * RWKV_PAIRS
        cum_all = [_mm(tri_incl, lw_s[rw, :], pb=3) for rw in rows]
        cum = [z[:, s] for z in cum_all for s in sl]
        p_incl = [jnp.exp(z) for z in cum]
        p_inv = [jnp.exp(-z) for z in cum]
        rr = [r_s[rw, s] for rw, s in chains]
        kc = [k_s[rw, s] for rw, s in chains]
        vc = [v_s[rw, s] for rw, s in chains]
        at2 = [jnp.where(own, dup(a_s[rw, s] * jnp.exp(cum[c] - lw_s[rw, s])), 0.0) for c, (rw, s) in enumerate(chains)]
        rt2 = [jnp.where(own, dup(rr[c] * p_incl[c]), 0.0) for c in ids]
        bt = [(b_s[rw, s] * p_inv[c]).astype(BF16) for c, (rw, s) in enumerate(chains)]
        kt = [(kc[c] * p_inv[c]).astype(BF16) for c in ids]
        vb = [z.astype(BF16) for z in vc]
        v2 = [dup(z) for z in vb]
        g = [_mm(cat([at2[c], rt2[c]], 0).astype(BF16), cat([dup(bt[c]), dup(kt[c])], 0), NT) for c in ids]
        a_ak = [jnp.where(strict, z[:n2, n2:], 0.0).astype(BF16) for z in g]
        apow = [jnp.where(strict, z[:n2, :n2], 0.0).astype(BF16) for z in g]
        a_r = [cat([jnp.where(incl, z[n2:, :n2], 0.0), jnp.where(incl, z[n2:, n2:], 0.0)], 1).astype(BF16)
               for z in g]
        akv = [_mm(a_ak[c], v2[c]) for c in ids]
        tinv = [eye + jnp.where(strict, z[:n2, :n2], 0.0) for z in g]
        for _ in range(levels - 1):
            apow = [_mm(z, z).astype(BF16) for z in apow]
            tinv = [tinv[c] + _mm(tinv[c].astype(BF16), apow[c]) for c in ids]
        sol = [_mm(tinv[c].astype(BF16), cat([at2[c], akv[c]], axis=1).astype(BF16)) for c in ids]
        ws = [z[:, :128] for z in sol]
        u0s = [jnp.where(own, z[:, 128:], 0.0) for z in sol]
        zeros2 = jnp.zeros((n2, 128), BF16)
        qy = [_mm(a_r[c], cat([cat([ws[c], u0s[c]], 1).astype(BF16), cat([zeros2, v2[c]], 1)], 0)) for c in ids]
        qs = [(rt2[c] + qy[c][:, :128]).astype(BF16) for c in ids]
        zeros1 = jnp.zeros((chunk, 128), BF16)
        lhs = [cat([cat([ws[c][:chunk] + ws[c][chunk:], u0s[c][:chunk] + u0s[c][chunk:]], 1).astype(BF16),
                    cat([zeros1, vb[c]], 1)], 0) for c in ids]
        wbn = [_mm(lhs[c], cat([bt[c], kt[c]], 0), TN) for c in ids]
        wb = [z[:128].astype(BF16) for z in wbn]
        s_cur = [state[q] for q in range(per_chunk)]
        ys = []
        for cc in range(group):
            base = cc * per_chunk
            s_b = [z.astype(BF16) for z in s_cur]
            ys += [_mm(qs[base + q], s_b[q], NT) + qy[base + q][:, 128:] for q in range(per_chunk)]
            sw = [_mm(s_b[q], wb[base + q]) for q in range(per_chunk)]
            s_cur = [jnp.where(blockdiag, (s_cur[q] + sw[q] + wbn[base + q][128:])
                               * p_incl[base + q][chunk - 1:chunk, :], 0.0) for q in range(per_chunk)]
        for q in range(per_chunk):
            state[q] = s_cur[q]
        y = [jnp.where(lo_lane, z[:chunk], z[chunk:]) for z in ys]

        bonus = [head_sum(rr[c] * kc[c] * rk_ref[:, s]) for c, (_, s) in enumerate(chains)]
        dlt = [y[c] - head_sum(y[c]) * (1.0 / HEAD_DIM) for c in ids]
        var = [head_sum(z * z) * (1.0 / HEAD_DIM) for z in dlt]
        for c, (rw, s) in enumerate(chains):
            yn = dlt[c] * lax.rsqrt(var[c] + GN_EPS) * lnw_ref[:, s] + lnb_ref[:, s]
            y_dst[rw, s] = yn + bonus[c] * vc[c]
        return 0

    for ci in range(n_chunks // group):
        chunk_body(ci, 0)
    if not direct:
        for bi in range(nb):
            y_ref[bi] = y_s[bi * tt:bi * tt + t_in, :]

    @pl.when(j == pl.num_programs(1) - 1)
    def _():
        for bi in range(nb):
            for p in range(RWKV_PAIRS):
                sout_ref[bi, 2 * p] = state[bi * RWKV_PAIRS + p, 0:HEAD_DIM, 0:HEAD_DIM]
                sout_ref[bi, 2 * p + 1] = state[bi * RWKV_PAIRS + p, HEAD_DIM:, HEAD_DIM:]
        shout_ref[...] = carry[...]


def _rwkv_scan(cols, shift_prev, s0, prm, tt, t_valid, nb):
    b, t, _ = cols.shape
    t_blk = min(t, tt)
    heads = pl.BlockSpec((nb, 2 * RWKV_PAIRS, HEAD_DIM, HEAD_DIM), lambda bi, j: (bi, 0, 0, 0))
    row = lambda n: pl.BlockSpec((1, n), lambda bi, j: (0, 0))
    vec = lambda a: a.reshape(1, -1)
    y, s_out, sh_out = pl.pallas_call(
        functools.partial(_rwkv_kernel, tt=tt, chunk=SCAN_CHUNK, t_valid=min(t_valid, tt)),
        grid=(b // nb, t // t_blk),
        in_specs=[pl.BlockSpec((nb, t_blk, C_SHIFT), lambda bi, j: (bi, j, 0)),
                  pl.BlockSpec((nb, 1, C_SHIFT), lambda bi, j: (bi, 0, 0)),
                  heads,
                  row(C_SHIFT), row(RWKV_W),
                  pl.BlockSpec((LORA, RWKV_W), lambda bi, j: (0, 0)),
                  row(RWKV_W),
                  pl.BlockSpec((LORA, RWKV_W), lambda bi, j: (0, 0)),
                  row(RWKV_W), row(RWKV_W), row(RWKV_W), row(RWKV_W), row(RWKV_W)],
        out_specs=[pl.BlockSpec((nb, t_blk, RWKV_W), lambda bi, j: (bi, j, 0)),
                   heads,
                   pl.BlockSpec((nb, 1, C_SHIFT), lambda bi, j: (bi, 0, 0))],
        out_shape=[jax.ShapeDtypeStruct((b, t, RWKV_W), F32),
                   jax.ShapeDtypeStruct((b, 2 * RWKV_PAIRS, HEAD_DIM, HEAD_DIM), F32),
                   jax.ShapeDtypeStruct((b, 1, C_SHIFT), F32)],
        scratch_shapes=[pltpu.VMEM((nb, 1, C_SHIFT), F32), pltpu.VMEM((nb * RWKV_PAIRS, 128, 128), F32)]
                       + [pltpu.VMEM((nb * tt, RWKV_W), F32)] * 7,
        compiler_params=_params("parallel", "arbitrary"),
        name="rwkv_scan",
    )(cols, shift_prev.reshape(b, 1, C_SHIFT), s0, vec(prm["mu"]), vec(prm["w0"]), prm["w_up"].astype(BF16),
      vec(prm["a0"]), prm["a_up"].astype(BF16), vec(prm["k_k"]), vec(prm["k_a"]), vec(prm["r_k"]),
      vec(prm["ln_w"]), vec(prm["ln_b"]))
    return y, s_out, sh_out.reshape(b, C_SHIFT)


def _post_kernel(*refs, merge_dils, mix_w, nb, recompute):
    x_ref, a_ref, b_ref, kvt_ref, w_ref, g_ref = refs[:6]
    mix_refs = refs[6:-4]
    y_ref, stage, omem, qmem = refs[-4:]
    tm = x_ref.shape[0]
    n_parts = POST_PARTS if tm % (8 * POST_PARTS) == 0 and tm // POST_PARTS >= 64 else 1
    parts = [slice(i * tm // n_parts, (i + 1) * tm // n_parts) for i in range(n_parts)]
    if recompute:
        xn = [_rms(x_ref[rs, :], a_ref[...]).astype(BF16) for rs in parts]
        for rs, z in zip(parts, xn):
            qmem[rs, :] = _dot(z, b_ref[:, :MEM_W])
        gate = [_dot(z, b_ref[:, MEM_W:]) for z in xn]
        _cross_attn(qmem, kvt_ref, omem, nb)
    else:
        gate = [b_ref[rs, :] for rs in parts]
        _cross_attn(a_ref, kvt_ref, omem, nb)
    if merge_dils:
        er = lax.broadcasted_iota(jnp.int32, (128, 2 * GROUP_W), 0)
        ec = lax.broadcasted_iota(jnp.int32, (128, 2 * GROUP_W), 1)
        expand = (er == ec // HEAD_DIM).astype(BF16)
        accs, stats = [], []
        for g, dil in enumerate(merge_dils):
            acc_ref, st_ref = mix_refs[2 * g], mix_refs[2 * g + 1]
            accs.append(acc_ref[...] if dil == 1 else _from_classes(acc_ref, stage, dil, GROUP_W))
            stats.append(st_ref[...] if dil == 1 else _from_classes(st_ref, stage, dil, 128))
        st = [[_mm(z[rs, :], expand, pa=3) for z in stats] for rs in parts]
        mx = [functools.reduce(jnp.maximum, [z[:, :GROUP_W] for z in sp]) for sp in st]
        wts = [[jnp.exp(z[:, :GROUP_W] - mx[i]) * z[:, GROUP_W:] for z in sp] for i, sp in enumerate(st)]
        mix = [sum(w * (a[rs, :] / z[:, GROUP_W:]) for w, a, z in zip(wts[i], accs, st[i])) / sum(wts[i])
               for i, rs in enumerate(parts)]
    else:
        mix = [mix_refs[0][rs, :] for rs in parts]
    act = [z * jax.nn.sigmoid(z) for z in gate]
    h1 = [(mix[i] * act[i][:, :mix_w]).astype(BF16) for i in range(n_parts)]
    h2 = [(omem[rs, :] * act[i][:, mix_w:]).astype(BF16) for i, rs in enumerate(parts)]
    out = [_dot(h1[i], w_ref[:mix_w, :]) + _dot(h2[i], w_ref[mix_w:, :]) for i in range(n_parts)]
    for i, rs in enumerate(parts):
        y_ref[rs, :] = x_ref[rs, :] + _rms(out[i], g_ref[...])


def _post(x, mix, qg, kvt, layer, w_out, g_post, tm, rows_per_batch, recompute):
    m, d = x.shape
    merge = isinstance(mix, (list, tuple))
    rows = lambda wd: pl.BlockSpec((tm, wd), lambda i: (i, 0))
    if recompute:
        gw = qg[1].shape[1] - MEM_W
        qg_arrays = [qg[0].reshape(1, d), qg[1].astype(BF16)]
        qg_specs = [pl.BlockSpec((1, d), lambda i: (0, 0)), pl.BlockSpec((d, MEM_W + gw), lambda i: (0, 0))]
    else:
        gw = qg[1].shape[1]
        qg_arrays, qg_specs = list(qg), [rows(MEM_W), rows(gw)]
    nb = max(1, tm // rows_per_batch)
    per_batch = max(1, rows_per_batch // tm)
    kvt_spec = pl.BlockSpec((None, nb, 2 * MEM_W, N_MEM), lambda i: (layer, i // per_batch, 0, 0))
    if merge:
        mix_w = GROUP_W
        merge_dils = tuple(acc.shape[2] // GROUP_W for acc, _ in mix)
        tiles = mix[0][0].shape[1] * merge_dils[0] // tm
        mix_arrays = [a for pair in mix for a in pair]
        mix_specs = [pl.BlockSpec((None, tm // dil, a.shape[2]), lambda i: (i // tiles, i % tiles, 0))
                     for dil, pair in zip(merge_dils, mix) for a in pair]
    else:
        mix_w, merge_dils, mix_arrays, mix_specs = mix.shape[1], (), [mix], [rows(mix.shape[1])]
    return pl.pallas_call(
        functools.partial(_post_kernel, merge_dils=merge_dils, mix_w=mix_w, nb=nb, recompute=recompute),
        grid=(m // tm,),
        in_specs=[rows(d)] + qg_specs + [
                  kvt_spec,
                  pl.BlockSpec((gw, d), lambda i: (0, 0)),
                  pl.BlockSpec((1, d), lambda i: (0, 0))] + mix_specs,
        out_specs=rows(d),
        out_shape=jax.ShapeDtypeStruct((m, d), F32),
        scratch_shapes=[pltpu.VMEM((tm, 128), F32), pltpu.VMEM((tm, MEM_W), F32), pltpu.VMEM((tm, MEM_W), F32)],
        compiler_params=_params("parallel"),
        name="post",
    )(x, *qg_arrays, kvt, w_out.astype(BF16), g_post.reshape(1, d), *mix_arrays)


def _t5_bucket(dist):
    max_exact = N_BUCKETS // 2
    d = jnp.maximum(dist, 1).astype(F32)
    large = max_exact + (jnp.log(d / max_exact) / math.log(MAX_DISTANCE / max_exact)
                         * (N_BUCKETS - max_exact)).astype(jnp.int32)
    large = jnp.minimum(large, N_BUCKETS - 1)
    return jnp.where(dist < max_exact, dist, large)


def _group_bias(rel_bias, g):
    dist = DILATIONS[g] * jnp.arange(A_KEYS, dtype=jnp.int32)
    bias = rel_bias[_t5_bucket(dist)]
    return bias[:, g * GROUP_HEADS:(g + 1) * GROUP_HEADS].T.astype(F32)


def _bias_tiles_kernel(x_ref, o_ref):
    row = lax.broadcasted_iota(jnp.int32, (Q_TILE, 2 * Q_TILE), 0)
    for h in range(o_ref.shape[0]):
        t = jnp.broadcast_to(x_ref[h:h + 1, :], (Q_TILE, 2 * Q_TILE))
        for bit in range(int(math.log2(Q_TILE))):
            t = jnp.where((row >> bit) & 1 == 1, pltpu.roll(t, 1 << bit, axis=1), t)
        o_ref[h] = t


def _prompt_bias(biases):
    bias = jnp.concatenate(biases, axis=0)
    neg = jnp.full((bias.shape[0], Q_TILE - 1), NEG_INF, F32)
    table = jnp.concatenate([bias[:, :1], neg, bias[:, :0:-1]], axis=1)
    n = table.shape[0]
    return pl.pallas_call(
        _bias_tiles_kernel,
        out_shape=jax.ShapeDtypeStruct((n, Q_TILE, 2 * Q_TILE), F32),
        name="bias_tiles",
    )(table)


def _sample_bias(biases, n_new, tp):
    t = np.arange(tp)[:, None]
    real = t < n_new
    past = []
    for g, bias in enumerate(biases):
        w, dil = WINDOWS[g], DILATIONS[g]
        spread = jnp.concatenate([bias[:, :0:-1, None], jnp.full((GROUP_HEADS, w // dil, dil - 1), NEG_INF, F32)],
                                 axis=-1).reshape(GROUP_HEADS, w)
        rows = [jnp.concatenate([jnp.full((GROUP_HEADS, tt), NEG_INF, F32), spread[:, :w - tt]], axis=-1)
                for tt in range(n_new)] + [jnp.zeros((GROUP_HEADS, w), F32)] * (tp - n_new)
        past.append(jnp.stack(rows, axis=1))
    s = np.arange(n_new)[None, :, None]
    g_idx = np.arange(N_GROUPS)[:, None, None]
    ok = (s <= t[None, :, 0]) & real[None, :, 0] & ((g_idx == 0) | (s == 0))
    vals = jnp.stack([bias[:, :n_new].T for bias in biases])
    new = jnp.where(jnp.asarray(ok)[..., None], vals[:, :, None, :], NEG_INF)
    return past, jnp.pad(new, ((0, 0), (0, 0), (0, 0), (0, 128 - GROUP_HEADS)))


def _pieces_a():
    q = [((g * GROUP_W, (g + 1) * GROUP_W),) for g in range(N_GROUPS)]
    kv = [((A_Q + g * GROUP_W, A_Q + (g + 1) * GROUP_W), (2 * A_Q + g * GROUP_W, 2 * A_Q + (g + 1) * GROUP_W))
          for g in range(N_GROUPS)]
    qmem = ((3 * A_Q, 3 * A_Q + MEM_W),)
    gate = ((3 * A_Q + MEM_W, 3 * A_Q + MEM_W + GROUP_W + MEM_W),)
    return tuple(q + kv + [qmem, gate])


def _pieces_b():
    return (((0, C_SHIFT),),)


def kernel(x_prompt, x_sample, mem_prompt, cache_mem_kv, cache_win0, cache_win1, cache_win2, state_wkv, state_shift, norm_pre, norm_post, norm_mem, w_mem_kv, rel_bias, w_in_a, w_out_a, w_in_b, w_out_b, rwkv_mu, rwkv_w0, rwkv_w_up, rwkv_a0, rwkv_a_up, rwkv_k_k, rwkv_k_a, rwkv_r_k, rwkv_ln_w, rwkv_ln_b):
    bp, tp, d = x_prompt.shape
    bs, ts, _ = x_sample.shape
    tsp = SAMPLE_PAD_T
    xp = x_prompt.reshape(bp * tp, d)
    xs = jnp.pad(x_sample, ((0, 0), (0, tsp - ts), (0, 0))).reshape(bs * tsp, d)
    tm_p, tm_s = 512, bs * tsp
    tm_post_s = 8 * tsp
    time_minor = lambda c: jnp.moveaxis(c, -4, -1)
    caches = [time_minor(c[0]) for c in (cache_win0, cache_win1, cache_win2)]
    mkv_s = time_minor(cache_mem_kv).reshape(cache_mem_kv.shape[0], bs, 2 * MEM_W, N_MEM)
    mkv_p = _mem_kv(mem_prompt, norm_mem, w_mem_kv)

    biases = [_group_bias(rel_bias, g) for g in range(N_GROUPS)]
    w_gq_b, w_mix_b = w_in_b[0][:, C_SHIFT:], w_in_b[0][:, :C_SHIFT]
    outs_p = _norm_proj(xp, norm_pre[0], w_in_a[0], _pieces_a(), tm_p, transposed=(3, 4, 5),
                        dils=DILATIONS + DILATIONS + (1, 1), rows_per_batch=tp)
    outs_s = _norm_proj(xs, norm_pre[0], w_in_a[0], _pieces_a(), tm_s)
    q_p, kv_p, qg_p, kvt_p = outs_p[0:3], outs_p[3:6], outs_p[6:8], outs_p[8:11]
    q_s, kv_s, qg_s = outs_s[0:3], outs_s[3:6], outs_s[6:8]

    bias_tiles = _prompt_bias(biases)
    merged = [_dil_attn(q_p[g].reshape(bp, tp // DILATIONS[g], -1), kv_p[g].reshape(bp, tp // DILATIONS[g], -1),
                        bias_tiles, g) for g in range(N_GROUPS)]
    xp1 = _post(xp, merged, qg_p, mkv_p, 0, w_out_a[0], norm_post[0], tm_p, tp, False)

    bias_past, bias_new = _sample_bias(biases, ts, tsp)
    q_s_all = jnp.concatenate(q_s, axis=-1).reshape(bs, tsp, A_Q)
    kvn = [a.reshape(bs, tsp, 2 * GROUP_W) for a in kv_s]
    o_s = _sample_attn(q_s_all, kvn, caches, bias_past, bias_new, ts)
    xs1 = _post(xs, o_s.reshape(bs * tsp, GROUP_W), qg_s, mkv_s, 0, w_out_a[0], norm_post[0], tm_post_s, tsp, False)

    (cols_p,) = _norm_proj(xp1, norm_pre[1], w_mix_b, _pieces_b(), 2 * tm_p)
    (cols_s,) = _norm_proj(xs1, norm_pre[1], w_mix_b, _pieces_b(), tm_s)
    prm = dict(mu=rwkv_mu[0], w0=rwkv_w0[0], w_up=rwkv_w_up[0], a0=rwkv_a0[0], a_up=rwkv_a_up[0], k_k=rwkv_k_k[0],
               k_a=rwkv_k_a[0], r_k=rwkv_r_k[0], ln_w=rwkv_ln_w[0], ln_b=rwkv_ln_b[0])
    y_p, wkv_p, sh_p = _rwkv_scan(cols_p.reshape(bp, tp, C_SHIFT), jnp.zeros((bp, C_SHIFT), F32),
                                  jnp.zeros((bp, 2 * RWKV_PAIRS, HEAD_DIM, HEAD_DIM), F32), prm, 512, tp, 1)
    y_s, wkv_s, sh_s = _rwkv_scan(cols_s.reshape(bs, tsp, C_SHIFT), state_shift[0], state_wkv[0], prm, SCAN_CHUNK,
                                  ts, 8)
    y_s = y_s.reshape(bs * tsp, RWKV_W)
    qg_b = (norm_pre[1], w_gq_b)
    xp2 = _post(xp1, y_p.reshape(bp * tp, RWKV_W), qg_b, mkv_p, 1, w_out_b[0], norm_post[1], tm_p, tp, True)
    xs2 = _post(xs1, y_s, qg_b, mkv_s, 1, w_out_b[0], norm_post[1], tm_post_s, tsp, True)

    kv_shape = (2, GROUP_HEADS, HEAD_DIM)
    time_major = lambda c: jnp.moveaxis(c, -1, -4)
    new_mem_kv = time_major(mkv_p.reshape(mkv_p.shape[0], bp, 2, MEM_W // HEAD_DIM, HEAD_DIM, N_MEM))
    win_p = [time_major(kvt_p[g].reshape(bp, *kv_shape, tp)[..., tp - min(WINDOWS[g], tp):])[None]
             for g in range(N_GROUPS)]
    win_s = [kv_s[g].reshape(bs, tsp, *kv_shape)[None, :, :ts] for g in range(N_GROUPS)]
    return (xp2.reshape(bp, tp, d), xs2.reshape(bs, tsp, d)[:, :ts], new_mem_kv,
            win_p[0], win_p[1], win_p[2], win_s[0], win_s[1], win_s[2],
            wkv_p[None], wkv_s[None], sh_p[None], sh_s[None])
```

```python
import functools
import math

import numpy as np
import jax
import jax.numpy as jnp
from jax import lax
from jax.experimental import pallas as pl
from jax.experimental.pallas import tpu as pltpu

F32 = jnp.float32
BF16 = jnp.bfloat16

D_MODEL = 1024
HEAD_DIM = 64
N_GROUPS = 3
GROUP_HEADS = 4
WINDOWS = (128, 512, 2048)
DILATIONS = (1, 4, 16)
A_HEADS = N_GROUPS * GROUP_HEADS
A_KEYS = 129
GROUP_W = GROUP_HEADS * HEAD_DIM
A_Q = A_HEADS * HEAD_DIM
N_MEM = 256
MEM_W = 256
RWKV_W = 768
RWKV_PAIRS = RWKV_W // 128
LORA = 64
C_SHIFT = 3 * RWKV_W + 2 * LORA
N_BUCKETS = 32
MAX_DISTANCE = WINDOWS[-1]
RMS_EPS = 1e-6
GN_EPS = 64e-5
NEG_INF = -1e30
SCALE = HEAD_DIM ** -0.5

Q_TILE = 128
DIL_SEGMENTS = 8
SCAN_CHUNK = 64
POST_PARTS = 2
SCAN_INTERLEAVE = 2
SAMPLE_PAD_T = 8
VMEM_LIMIT = 56 * 1024 * 1024
NN =(((1,), (0,)), ((), ()))
NT = (((1,), (1,)), ((), ()))
TN = (((0,), (0,)), ((), ()))


def _params(*sem):
    return pltpu.CompilerParams(dimension_semantics=sem, vmem_limit_bytes=VMEM_LIMIT)


def _dot(a, b):
    return jnp.dot(a, b, preferred_element_type=F32)


def _dot_nt(a, b):
    return lax.dot_general(a, b, NT, preferred_element_type=F32)


def _split(x, n):
    if x.dtype == BF16:
        return [x]
    parts, rem = [], x
    for i in range(n):
        parts.append(rem.astype(BF16))
        if i + 1 < n:
            rem = rem - parts[-1].astype(F32)
    return parts


def _mm(a, b, dims=NN, pa=1, pb=1):
    pas, pbs = _split(a, pa), _split(b, pb)
    order = max(len(pas), len(pbs))
    out = None
    for i, ai in enumerate(pas):
        for j, bj in enumerate(pbs):
            if i + j < order:
                term = lax.dot_general(ai, bj, dims, preferred_element_type=F32)
                out = term if out is None else out + term
    return out


def _rms(x, g):
    return x * lax.rsqrt(jnp.mean(x * x, axis=-1, keepdims=True) + RMS_EPS) * g


def _to_classes(dst_ref, stage, val, dil, width, off):
    tm = val.shape[0]
    for j in range(val.shape[1] // 128):
        stage[...] = val[:, 128 * j:128 * (j + 1)]
        for r in range(dil):
            lane0 = r * width + off + 128 * j
            dst_ref[:, lane0:lane0 + 128] = stage[pl.ds(r, tm // dil, stride=dil), :]


def _from_classes(src_ref, stage, dil, width):
    tm = stage.shape[0]
    chunks = []
    for j in range(width // 128):
        for r in range(dil):
            lane0 = r * width + 128 * j
            stage[pl.ds(r, tm // dil, stride=dil), :] = src_ref[:, lane0:lane0 + 128]
        chunks.append(stage[...])
    return jnp.concatenate(chunks, axis=1)


def _norm_proj_kernel(x_ref, g_ref, w_ref, *refs, pieces, transposed, dils):
    stage = refs[-1]
    out_refs = refs[:-1]
    xn = _rms(x_ref[...], g_ref[...]).astype(BF16)
    t_refs = iter(out_refs[len(pieces):])
    for idx, (out_ref, cols) in enumerate(zip(out_refs, pieces)):
        t_ref = next(t_refs) if idx in transposed else None
        width = sum(c1 - c0 for c0, c1 in cols)
        off = 0
        for c0, c1 in cols:
            res = _dot(xn, w_ref[:, c0:c1])
            if dils[idx] > 1:
                _to_classes(out_ref, stage, res, dils[idx], width, off)
            else:
                out_ref[:, off:off + c1 - c0] = res
            if t_ref is not None:
                t_ref[off:off + c1 - c0, :] = res.T
            off += c1 - c0


def _norm_proj(x, g, w, pieces, tm, transposed=(), dils=None, rows_per_batch=None):
    m, d = x.shape
    n = w.shape[1]
    dils = tuple(dils) if dils else (1,) * len(pieces)
    widths = [sum(c1 - c0 for c0, c1 in cols) for cols in pieces]
    tiles = rows_per_batch // tm if rows_per_batch else None
    out_specs, out_shape = [], []
    for wd, dil in zip(widths, dils):
        if dil > 1:
            out_specs.append(pl.BlockSpec((None, tm // dil, dil * wd), lambda i: (i // tiles, i % tiles, 0)))
            out_shape.append(jax.ShapeDtypeStruct((m // rows_per_batch, rows_per_batch // dil, dil * wd), F32))
        else:
            out_specs.append(pl.BlockSpec((tm, wd), lambda i: (i, 0)))
            out_shape.append(jax.ShapeDtypeStruct((m, wd), F32))
    out_specs += [pl.BlockSpec((None, widths[idx], tm), lambda i: (i // tiles, 0, i % tiles)) for idx in transposed]
    out_shape += [jax.ShapeDtypeStruct((m // rows_per_batch, widths[idx], rows_per_batch), F32) for idx in transposed]
    return pl.pallas_call(
        functools.partial(_norm_proj_kernel, pieces=pieces, transposed=tuple(transposed), dils=dils),
        grid=(m // tm,),
        in_specs=[pl.BlockSpec((tm, d), lambda i: (i, 0)),
                  pl.BlockSpec((1, d), lambda i: (0, 0)),
                  pl.BlockSpec((d, n), lambda i: (0, 0))],
        out_specs=out_specs,
        out_shape=out_shape,
        scratch_shapes=[pltpu.VMEM((tm, 128), F32)],
        compiler_params=_params("parallel"),
        name="norm_proj",
    )(x, g.reshape(1, d), w.astype(BF16))


def _mem_kv_kernel(x_ref, g_ref, wt_ref, o_ref):
    x = x_ref[...]
    for l in range(o_ref.shape[0]):
        o_ref[l] = _dot_nt(wt_ref[l], _rms(x, g_ref[l]).astype(BF16))


def _mem_kv(mem, g, w):
    b, n, d = mem.shape
    nl, _, wd = w.shape
    return pl.pallas_call(
        _mem_kv_kernel,
        grid=(b,),
        in_specs=[pl.BlockSpec((None, n, d), lambda bi: (bi, 0, 0)),
                  pl.BlockSpec((nl, 1, d), lambda bi: (0, 0, 0)),
                  pl.BlockSpec((nl, wd, d), lambda bi: (0, 0, 0))],
        out_specs=pl.BlockSpec((nl, None, wd, n), lambda bi: (0, bi, 0, 0)),
        out_shape=jax.ShapeDtypeStruct((nl, b, wd, n), F32),
        compiler_params=_params("parallel"),
        name="mem_kv",
    )(mem, g.reshape(nl, 1, d), jnp.swapaxes(w, 1, 2).astype(BF16))


def _dil_attn_kernel(q_ref, kvc_ref, kvp_ref, bias_ref, acc_ref, st_ref, *, nr, tiles, prev_block):
    i = pl.program_id(2)
    qf = q_ref[...] * SCALE
    kvc = kvc_ref[...].astype(BF16)
    kvp = kvp_ref[...].astype(BF16) if prev_block else None
    chains = [(rr, tt, h) for rr in range(nr) for tt in range(tiles) for h in range(GROUP_HEADS)]
    n = len(chains)
    rows = lambda tt: slice(tt * Q_TILE, (tt + 1) * Q_TILE)
    lo_lane = lax.broadcasted_iota(jnp.int32, (Q_TILE, 128), 1) < HEAD_DIM

    def keys(rr, tt, h, off):
        lanes = slice(rr * 2 * GROUP_W + off + (h // 2) * 128, rr * 2 * GROUP_W + off + (h // 2 + 1) * 128)
        return kvp[:, lanes] if tt < 0 else kvc[rows(tt), lanes]

    has_prev = [tt > 0 or prev_block for _, tt, _ in chains]
    qh = [jnp.where(lo_lane if h % 2 == 0 else jnp.logical_not(lo_lane),
                    qf[rows(tt), rr * GROUP_W + (h // 2) * 128:rr * GROUP_W + (h // 2 + 1) * 128], 0.0).astype(BF16)
          for rr, tt, h in chains]
    lc = [_dot_nt(qh[c], keys(rr, tt, h, 0)) + bias_ref[h, :, :Q_TILE] for c, (rr, tt, h) in enumerate(chains)]
    lp = [None] * n
    for c, (rr, tt, h) in enumerate(chains):
        if has_prev[c]:
            z = _dot_nt(qh[c], keys(rr, tt - 1, h, 0)) + bias_ref[h, :, Q_TILE:]
            lp[c] = jnp.where(i > 0, z, NEG_INF) if tt == 0 else z
    m = [jnp.max(z, axis=-1, keepdims=True) for z in lc]
    m = [jnp.maximum(m[c], jnp.max(lp[c], axis=-1, keepdims=True)) if has_prev[c] else m[c] for c in range(n)]
    pc = [jnp.exp(lc[c] - m[c]) for c in range(n)]
    pp = [jnp.exp(lp[c] - m[c]) if has_prev[c] else None for c in range(n)]
    den = [jnp.sum(z, axis=-1, keepdims=True) for z in pc]
    den = [den[c] + jnp.sum(pp[c], axis=-1, keepdims=True) if has_prev[c] else den[c] for c in range(n)]
    o = [_dot(pc[c].astype(BF16), keys(rr, tt, h, GROUP_W)) for c, (rr, tt, h) in enumerate(chains)]
    o = [o[c] + _dot(pp[c].astype(BF16), keys(rr, tt - 1, h, GROUP_W)) if has_prev[c] else o[c]
         for c, (rr, tt, h) in enumerate(chains)]
    lane = lax.broadcasted_iota(jnp.int32, (Q_TILE, 128), 1)
    st = None
    for c, (rr, tt, h) in enumerate(chains):
        if h % 2 == 1:
            pair = slice(rr * GROUP_W + (h // 2) * 128, rr * GROUP_W + (h // 2 + 1) * 128)
            acc_ref[rows(tt), pair] = jnp.where(lo_lane, o[c - 1], o[c])
        st = jnp.zeros((Q_TILE, 128), F32) if h == 0 else st
        st = jnp.where(lane == h, m[c], st)
        st = jnp.where(lane == GROUP_HEADS + h, den[c], st)
        if h == GROUP_HEADS - 1:
            st_ref[rows(tt), rr * 128:(rr + 1) * 128] = st


def _dil_attn(qv, kvv, bias_tiles, g):
    dil = DILATIONS[g]
    b, ln, _ = qv.shape
    tiles = min(ln // Q_TILE, DIL_SEGMENTS)
    nr = min(dil, DIL_SEGMENTS // tiles)
    nblk = ln // (tiles * Q_TILE)
    return pl.pallas_call(
        functools.partial(_dil_attn_kernel, nr=nr, tiles=tiles, prev_block=nblk > 1),
        grid=(b, dil // nr, nblk),
        in_specs=[pl.BlockSpec((None, tiles * Q_TILE, nr * GROUP_W), lambda bi, r, i: (bi, i, r)),
                  pl.BlockSpec((None, tiles * Q_TILE, nr * 2 * GROUP_W), lambda bi, r, i: (bi, i, r)),
                  pl.BlockSpec((None, Q_TILE if nblk > 1 else 8, nr * 2 * GROUP_W),
                               lambda bi, r, i: (bi, jnp.maximum(tiles * i - 1, 0), r)),
                  pl.BlockSpec((GROUP_HEADS, Q_TILE, 2 * Q_TILE), lambda bi, r, i: (g, 0, 0))],
        out_specs=[pl.BlockSpec((None, tiles * Q_TILE, nr * GROUP_W), lambda bi, r, i: (bi, i, r)),
                   pl.BlockSpec((None, tiles * Q_TILE, nr * 128), lambda bi, r, i: (bi, i, r))],
        out_shape=[jax.ShapeDtypeStruct((b, ln, dil * GROUP_W), F32),
                   jax.ShapeDtypeStruct((b, ln, dil * 128), F32)],
        compiler_params=_params("parallel", "parallel", "arbitrary"),
        name=f"dil_attn_d{dil}",
    )(qv, kvv, kvv, bias_tiles)


def _sample_attn_kernel(q_ref, kn0_ref, kn1_ref, kn2_ref, c0_ref, c1_ref, c2_ref, b0_ref, b1_ref, b2_ref,
                        bnew_ref, o_ref, *, n_new):
    q = (q_ref[...] * SCALE).astype(BF16)
    qf = q.astype(F32)
    groups = ((kn0_ref, c0_ref, b0_ref), (kn1_ref, c1_ref, b1_ref), (kn2_ref, c2_ref, b2_ref))
    new_rows = []
    for g, (kn_ref, _, _) in enumerate(groups):
        kn = kn_ref[...]
        shifted = [kn] + [pltpu.roll(kn, s, axis=0) for s in range(1, n_new if g == 0 else 1)]
        new_rows.append([z.astype(BF16).astype(F32) for z in shifted])
    chains = [(h, g) for h in range(GROUP_HEADS) for g in range(N_GROUPS)]
    ids = range(len(chains))
    ks = lambda h: slice(h * HEAD_DIM, (h + 1) * HEAD_DIM)
    vs = lambda h: slice(GROUP_W + h * HEAD_DIM, GROUP_W + (h + 1) * HEAD_DIM)
    qs = lambda h, g: slice(g * GROUP_W + h * HEAD_DIM, g * GROUP_W + (h + 1) * HEAD_DIM)
    lp = [_dot(q[:, qs(h, g)], groups[g][1][0, h].astype(BF16)) + groups[g][2][h] for h, g in chains]
    lns = [[jnp.sum(qf[:, qs(h, g)] * rows[:, ks(h)], axis=-1, keepdims=True) + bnew_ref[g, s][:, h:h + 1]
            for s, rows in enumerate(new_rows[g])] for h, g in chains]
    m = [functools.reduce(jnp.maximum, [jnp.max(lp[c], axis=-1, keepdims=True)] + lns[c]) for c in ids]
    pp = [jnp.exp(lp[c] - m[c]) for c in ids]
    pn = [[jnp.exp(ln - m[c]) for ln in lns[c]] for c in ids]
    den = [jnp.sum(pp[c], axis=-1, keepdims=True) + sum(pn[c]) for c in ids]
    o = [_dot_nt(pp[c].astype(BF16), groups[g][1][1, h].astype(BF16)) for c, (h, g) in enumerate(chains)]
    o = [o[c] + sum(z.astype(BF16).astype(F32) * rows[:, vs(h)] for z, rows in zip(pn[c], new_rows[g]))
         for c, (h, g) in enumerate(chains)]
    for h in range(GROUP_HEADS):
        cs = [h * N_GROUPS + g for g in range(N_GROUPS)]
        mx = functools.reduce(jnp.maximum, [m[c] for c in cs])
        wts = [jnp.exp(m[c] - mx) * den[c] for c in cs]
        num = sum(w * (o[c] / den[c]) for w, c in zip(wts, cs))
        o_ref[:, ks(h)] = num / sum(wts)


def _sample_attn(q, kvn, caches, bias_past, bias_new, n_new):
    b, tp, _ = q.shape
    new_spec = pl.BlockSpec((None, tp, 2 * GROUP_W), lambda bi: (bi, 0, 0))
    cache_spec = lambda w: pl.BlockSpec((None, 2, GROUP_HEADS, HEAD_DIM, w), lambda bi: (bi, 0, 0, 0, 0))
    bias_spec = lambda w: pl.BlockSpec((GROUP_HEADS, tp, w), lambda bi: (0, 0, 0))
    return pl.pallas_call(
        functools.partial(_sample_attn_kernel, n_new=n_new),
        grid=(b,),
        in_specs=[pl.BlockSpec((None, tp, A_Q), lambda bi: (bi, 0, 0)), new_spec, new_spec, new_spec]
                 + [cache_spec(w) for w in WINDOWS] + [bias_spec(w) for w in WINDOWS]
                 + [pl.BlockSpec((N_GROUPS, n_new, tp, 128), lambda bi: (0, 0, 0, 0))],
        out_specs=pl.BlockSpec((None, tp, GROUP_W), lambda bi: (bi, 0, 0)),
        out_shape=jax.ShapeDtypeStruct((b, tp, GROUP_W), F32),
        compiler_params=_params("parallel"),
        name="sample_attn",
    )(q, kvn[0], kvn[1], kvn[2], *caches, *bias_past, bias_new)


def _cross_attn(q_ref, kvt_ref, dst_ref, nb):
    tq = q_ref.shape[0] // nb
    sub = min(tq, Q_TILE)
    chains = [(bi, slice(r0, r0 + sub), h) for bi in range(nb) for r0 in range(bi * tq, (bi + 1) * tq, sub)
              for h in range(MEM_W // HEAD_DIM)]
    kvt = [kvt_ref[bi].astype(BF16) for bi in range(nb)]
    pair = lambda h, off=0: slice(off + (h // 2) * 128, off + (h // 2 + 1) * 128)
    lo_lane = lax.broadcasted_iota(jnp.int32, (sub, 128), 1) < HEAD_DIM
    qm = [jnp.where(lo_lane if h % 2 == 0 else jnp.logical_not(lo_lane), q_ref[rw, pair(h)] * SCALE, 0.0).astype(BF16)
          for bi, rw, h in chains]
    lg = [_dot(qm[c], kvt[bi][pair(h), :]) for c, (bi, rw, h) in enumerate(chains)]
    p = [jnp.exp(z - jnp.max(z, axis=-1, keepdims=True)) for z in lg]
    den = [jnp.sum(z, axis=-1, keepdims=True) for z in p]
    o = [_dot_nt(p[c].astype(BF16), kvt[bi][pair(h, MEM_W), :]) / den[c] for c, (bi, rw, h) in enumerate(chains)]
    for c, (bi, rw, h) in enumerate(chains):
        if h % 2 == 1:
            dst_ref[rw, pair(h)] = jnp.where(lo_lane, o[c - 1], o[c])


def _rwkv_kernel(cols_ref, shift_ref, s0_ref, mu_ref, w0_ref, wup_ref, a0_ref, aup_ref, kk_ref, ka_ref, rk_ref,
                 lnw_ref, lnb_ref, y_ref, sout_ref, shout_ref,
                 carry, state, r_s, k_s, v_s, a_s, b_s, lw_s, y_s, *, tt, chunk, t_valid):
    j = pl.program_id(1)
    nb, t_in = cols_ref.shape[0], cols_ref.shape[1]
    zero_head = jnp.zeros((HEAD_DIM, HEAD_DIM), F32)

    @pl.when(j == 0)
    def _():
        carry[...] = shift_ref[...]
        for bi in range(nb):
            for p in range(RWKV_PAIRS):
                state[bi * RWKV_PAIRS + p] = jnp.concatenate(
                    [jnp.concatenate([s0_ref[bi, 2 * p], zero_head], axis=1),
                     jnp.concatenate([zero_head, s0_ref[bi, 2 * p + 1]], axis=1)], axis=0)

    direct = nb == 1 and t_in == tt
    y_dst = y_ref.at[0] if direct else y_s
    row = lax.broadcasted_iota(jnp.int32, (tt, 1), 0)
    live = row < t_valid
    blockdiag = (lax.broadcasted_iota(jnp.int32, (128, 128), 0) // HEAD_DIM
                 == lax.broadcasted_iota(jnp.int32, (128, 128), 1) // HEAD_DIM)

    def head_sum(x):
        lo = lax.broadcasted_iota(jnp.int32, x.shape, 1) < HEAD_DIM
        s0 = jnp.sum(jnp.where(lo, x, 0.0), axis=-1, keepdims=True)
        s1 = jnp.sum(jnp.where(lo, 0.0, x), axis=-1, keepdims=True)
        return jnp.where(lo, s0, s1)

    for bi in range(nb):
        base = slice(bi * tt, (bi + 1) * tt)
        cols = cols_ref[bi]
        if t_in < tt:
            cols = jnp.concatenate([cols, jnp.zeros((tt - t_in, cols.shape[1]), F32)], axis=0)
        prev = jnp.where(row == 0, carry[bi], pltpu.roll(cols, 1, axis=0))
        carry[bi] = cols[t_valid - 1:t_valid, :]
        xs = cols + mu_ref[...] * (prev - cols)
        r = xs[:, :RWKV_W]
        k = xs[:, RWKV_W:2 * RWKV_W]
        v = xs[:, 2 * RWKV_W:3 * RWKV_W]
        wd = xs[:, 3 * RWKV_W:3 * RWKV_W + LORA]
        ad = xs[:, 3 * RWKV_W + LORA:]
        z = w0_ref[...] + _dot(jnp.tanh(wd).astype(BF16), wup_ref[...])
        log_decay = -math.exp(-0.5) * jax.nn.sigmoid(z)
        a = jax.nn.sigmoid(a0_ref[...] + _dot(ad.astype(BF16), aup_ref[...]))
        kk = k * kk_ref[...]
        kk_sq = kk * kk
        k2 = k * (1.0 + (a - 1.0) * ka_ref[...])
        for p in range(RWKV_PAIRS):
            ps = slice(p * 128, (p + 1) * 128)
            nrm = jnp.maximum(jnp.sqrt(head_sum(kk_sq[:, ps])), 1e-12)
            kkn = kk[:, ps] / nrm
            if t_valid < tt:
                zero = jnp.zeros((tt, 128), F32)
                r_s[base, ps] = r[:, ps]
                k_s[base, ps] = jnp.where(live, k2[:, ps], zero)
                v_s[base, ps] = jnp.where(live, v[:, ps], zero)
                a_s[base, ps] = jnp.where(live, -kkn, zero)
                b_s[base, ps] = jnp.where(live, kkn * a[:, ps], zero)
                lw_s[base, ps] = jnp.where(live, log_decay[:, ps], zero)
            else:
                r_s[base, ps] = r[:, ps]
                k_s[base, ps] = k2[:, ps]
                v_s[base, ps] = v[:, ps]
                a_s[base, ps] = -kkn
                b_s[base, ps] = kkn * a[:, ps]
                lw_s[base, ps] = log_decay[:, ps]

    ci = lax.broadcasted_iota(jnp.int32, (chunk, chunk), 0)
    cj = lax.broadcasted_iota(jnp.int32, (chunk, chunk), 1)
    tri_incl = (ci >= cj).astype(BF16)
    lo_lane = lax.broadcasted_iota(jnp.int32, (chunk, 128), 1) < HEAD_DIM
    levels = max(1, math.ceil(math.log2(min(chunk, t_valid))))
    n2 = 2 * chunk
    ri = lax.broadcasted_iota(jnp.int32, (n2, n2), 0)
    rj = lax.broadcasted_iota(jnp.int32, (n2, n2), 1)
    same = ri // chunk == rj // chunk
    strict = jnp.logical_and(same, ri > rj)
    eye = (ri == rj).astype(F32)
    incl = jnp.logical_and(same, ri >= rj)
    own = lax.broadcasted_iota(jnp.int32, (n2, 128), 0) // chunk == lax.broadcasted_iota(jnp.int32, (n2, 128), 1) // HEAD_DIM
    dup = lambda z: jnp.concatenate([z, z], axis=0)
    cat = jnp.concatenate

    n_chunks = tt // chunk
    group = SCAN_INTERLEAVE if n_chunks % SCAN_INTERLEAVE == 0 else 1

    def chunk_body(ci, _):
        rows = [pl.ds(bi * tt + (ci * group + cc) * chunk, chunk) for cc in range(group) for bi in range(nb)]
        sl = [slice(p * 128, (p + 1) * 128) for p in range(RWKV_PAIRS)]
        chains = [(rw, s) for rw in rows for s in sl]
        ids = range(len(chains))
        per_chunk = nb * RWKV_PAIRS
        cum_all = [_mm(tri_incl, lw_s[rw, :], pb=3) for rw in rows]
        cum = [z[:, s] for z in cum_all for s in sl]
        p_incl = [jnp.exp(z) for z in cum]
        p_inv = [jnp.exp(-z) for z in cum]
        rr = [r_s[rw, s] for rw, s in chains]
        kc = [k_s[rw, s] for rw, s in chains]
        vc = [v_s[rw, s] for rw, s in chains]
        at2 = [jnp.where(own, dup(a_s[rw, s] * jnp.exp(cum[c] - lw_s[rw, s])), 0.0) for c, (rw, s) in enumerate(chains)]
        rt2 = [jnp.where(own, dup(rr[c] * p_incl[c]), 0.0) for c in ids]
        bt = [(b_s[rw, s] * p_inv[c]).astype(BF16) for c, (rw, s) in enumerate(chains)]
        kt = [(kc[c] * p_inv[c]).astype(BF16) for c in ids]
        vb = [z.astype(BF16) for z in vc]
        v2 = [dup(z) for z in vb]
        g = [_mm(cat([at2[c], rt2[c]], 0).astype(BF16), cat([dup(bt[c]), dup(kt[c])], 0), NT) for c in ids]
        a_ak = [jnp.where(strict, z[:n2, n2:], 0.0).astype(BF16) for z in g]
        apow = [jnp.where(strict, z[:n2, :n2], 0.0).astype(BF16) for z in g]
        a_r = [cat([jnp.where(incl, z[n2:, :n2], 0.0), jnp.where(incl, z[n2:, n2:], 0.0)], 1).astype(BF16)
               for z in g]
        akv = [_mm(a_ak[c], v2[c]) for c in ids]
        tinv = [eye + jnp.where(strict, z[:n2, :n2], 0.0) for z in g]
        for _ in range(levels - 1):
            apow = [_mm(z, z).astype(BF16) for z in apow]
            tinv = [tinv[c] + _mm(tinv[c].astype(BF16), apow[c]) for c in ids]
        sol = [_mm(tinv[c].astype(BF16), cat([at2[c], akv[c]], axis=1).astype(BF16)) for c in ids]
        ws = [z[:, :128] for z in sol]
        u0s = [jnp.where(own, z[:, 128:], 0.0) for z in sol]
        zeros2 = jnp.zeros((n2, 128), BF16)
        qy = [_mm(a_r[c], cat([cat([ws[c], u0s[c]], 1).astype(BF16), cat([zeros2, v2[c]], 1)], 0)) for c in ids]
        qs = [(rt2[c] + qy[c][:, :128]).astype(BF16) for c in ids]
        zeros1 = jnp.zeros((chunk, 128), BF16)
        lhs = [cat([cat([ws[c][:chunk] + ws[c][chunk:], u0s[c][:chunk] + u0s[c][chunk:]], 1).astype(BF16),
                    cat([zeros1, vb[c]], 1)], 0) for c in ids]
        wbn = [_mm(lhs[c], cat([bt[c], kt[c]], 0), TN) for c in ids]
        wb = [z[:128].astype(BF16) for z in wbn]
        s_cur = [state[q] for q in range(per_chunk)]
        ys = []
        for cc in range(group):
            base = cc * per_chunk
            s_b = [z.astype(BF16) for z in s_cur]
            ys += [_mm(qs[base + q], s_b[q], NT) + qy[base + q][:, 128:] for q in range(per_chunk)]
            sw = [_mm(s_b[q], wb[base + q]) for q in range(per_chunk)]
            s_cur = [jnp.where(blockdiag, (s_cur[q] + sw[q] + wbn[base + q][128:])
                               * p_incl[base + q][chunk - 1:chunk, :], 0.0) for q in range(per_chunk)]
        for q in range(per_chunk):
            state[q] = s_cur[q]
        y = [jnp.where(lo_lane, z[:chunk], z[chunk:]) for z in ys]

        bonus = [head_sum(rr[c] * kc[c] * rk_ref[:, s]) for c, (_, s) in enumerate(chains)]
        dlt = [y[c] - head_sum(y[c]) * (1.0 / HEAD_DIM) for c in ids]
        var = [head_sum(z * z) * (1.0 / HEAD_DIM) for z in dlt]
        for c, (rw, s) in enumerate(chains):
            yn = dlt[c] * lax.rsqrt(var[c] + GN_EPS) * lnw_ref[:, s] + lnb_ref[:, s]
            y_dst[rw, s] = yn + bonus[c] * vc[c]
        return 0

    for ci in range(n_chunks // group):
        chunk_body(ci, 0)
    if not direct:
        for bi in range(nb):
            y_ref[bi] = y_s[bi * tt:bi * tt + t_in, :]

    @pl.when(j == pl.num_programs(1) - 1)
    def _():
        for bi in range(nb):
            for p in range(RWKV_PAIRS):
                sout_ref[bi, 2 * p] = state[bi * RWKV_PAIRS + p, 0:HEAD_DIM, 0:HEAD_DIM]
                sout_ref[bi, 2 * p + 1] = state[bi * RWKV_PAIRS + p, HEAD_DIM:, HEAD_DIM:]
        shout_ref[...] = carry[...]


def _rwkv_scan(cols, shift_prev, s0, prm, tt, t_valid, nb):
    b, t, _ = cols.shape
    t_blk = min(t, tt)
    heads = pl.BlockSpec((nb, 2 * RWKV_PAIRS, HEAD_DIM, HEAD_DIM), lambda bi, j: (bi, 0, 0, 0))
    row = lambda n: pl.BlockSpec((1, n), lambda bi, j: (0, 0))
    vec = lambda a: a.reshape(1, -1)
    y, s_out, sh_out = pl.pallas_call(
        functools.partial(_rwkv_kernel, tt=tt, chunk=SCAN_CHUNK, t_valid=min(t_valid, tt)),
        grid=(b // nb, t // t_blk),
        in_specs=[pl.BlockSpec((nb, t_blk, C_SHIFT), lambda bi, j: (bi, j, 0)),
                  pl.BlockSpec((nb, 1, C_SHIFT), lambda bi, j: (bi, 0, 0)),
                  heads,
                  row(C_SHIFT), row(RWKV_W),
                  pl.BlockSpec((LORA, RWKV_W), lambda bi, j: (0, 0)),
                  row(RWKV_W),
                  pl.BlockSpec((LORA, RWKV_W), lambda bi, j: (0, 0)),
                  row(RWKV_W), row(RWKV_W), row(RWKV_W), row(RWKV_W), row(RWKV_W)],
        out_specs=[pl.BlockSpec((nb, t_blk, RWKV_W), lambda bi, j: (bi, j, 0)),
                   heads,
                   pl.BlockSpec((nb, 1, C_SHIFT), lambda bi, j: (bi, 0, 0))],
        out_shape=[jax.ShapeDtypeStruct((b, t, RWKV_W), F32),
                   jax.ShapeDtypeStruct((b, 2 * RWKV_PAIRS, HEAD_DIM, HEAD_DIM), F32),
                   jax.ShapeDtypeStruct((b, 1, C_SHIFT), F32)],
        scratch_shapes=[pltpu.VMEM((nb, 1, C_SHIFT), F32), pltpu.VMEM((nb * RWKV_PAIRS, 128, 128), F32)]
                       + [pltpu.VMEM((nb * tt, RWKV_W), F32)] * 7,
        compiler_params=_params("parallel", "arbitrary"),
        name="rwkv_scan",
    )(cols, shift_prev.reshape(b, 1, C_SHIFT), s0, vec(prm["mu"]), vec(prm["w0"]), prm["w_up"].astype(BF16),
      vec(prm["a0"]), prm["a_up"].astype(BF16), vec(prm["k_k"]), vec(prm["k_a"]), vec(prm["r_k"]),
      vec(prm["ln_w"]), vec(prm["ln_b"]))
    return y, s_out, sh_out.reshape(b, C_SHIFT)


def _post_kernel(*refs, merge_dils, mix_w, nb, recompute):
    x_ref, a_ref, b_ref, kvt_ref, w_ref, g_ref = refs[:6]
    mix_refs = refs[6:-4]
    y_ref, stage, omem, qmem = refs[-4:]
    tm = x_ref.shape[0]
    n_parts = POST_PARTS if tm % (8 * POST_PARTS) == 0 and tm // POST_PARTS >= 64 else 1
    parts = [slice(i * tm // n_parts, (i + 1) * tm // n_parts) for i in range(n_parts)]
    if recompute:
        xn = [_rms(x_ref[rs, :], a_ref[...]).astype(BF16) for rs in parts]
        for rs, z in zip(parts, xn):
            qmem[rs, :] = _dot(z, b_ref[:, :MEM_W])
        gate = [_dot(z, b_ref[:, MEM_W:]) for z in xn]
        _cross_attn(qmem, kvt_ref, omem, nb)
    else:
        gate = [b_ref[rs, :] for rs in parts]
        _cross_attn(a_ref, kvt_ref, omem, nb)
    if merge_dils:
        er = lax.broadcasted_iota(jnp.int32, (128, 2 * GROUP_W), 0)
        ec = lax.broadcasted_iota(jnp.int32, (128, 2 * GROUP_W), 1)
        expand = (er == ec // HEAD_DIM).astype(BF16)
        accs, stats = [], []
        for g, dil in enumerate(merge_dils):
            acc_ref, st_ref = mix_refs[2 * g], mix_refs[2 * g + 1]
            accs.append(acc_ref[...] if dil == 1 else _from_classes(acc_ref, stage, dil, GROUP_W))
            stats.append(st_ref[...] if dil == 1 else _from_classes(st_ref, stage, dil, 128))
        st = [[_mm(z[rs, :], expand, pa=3) for z in stats] for rs in parts]
        mx = [functools.reduce(jnp.maximum, [z[:, :GROUP_W] for z in sp]) for sp in st]
        wts = [[jnp.exp(z[:, :GROUP_W] - mx[i]) * z[:, GROUP_W:] for z in sp] for i, sp in enumerate(st)]
        mix = [sum(w * (a[rs, :] / z[:, GROUP_W:]) for w, a, z in zip(wts[i], accs, st[i])) / sum(wts[i])
               for i, rs in enumerate(parts)]
    else:
        mix = [mix_refs[0][rs, :] for rs in parts]
    act = [z * jax.nn.sigmoid(z) for z in gate]
    h1 = [(mix[i] * act[i][:, :mix_w]).astype(BF16) for i in range(n_parts)]
    h2 = [(omem[rs, :] * act[i][:, mix_w:]).astype(BF16) for i, rs in enumerate(parts)]
    out = [_dot(h1[i], w_ref[:mix_w, :]) + _dot(h2[i], w_ref[mix_w:, :]) for i in range(n_parts)]
    for i, rs in enumerate(parts):
        y_ref[rs, :] = x_ref[rs, :] + _rms(out[i], g_ref[...])


def _post(x, mix, qg, kvt, layer, w_out, g_post, tm, rows_per_batch, recompute):
    m, d = x.shape
    merge = isinstance(mix, (list, tuple))
    rows = lambda wd: pl.BlockSpec((tm, wd), lambda i: (i, 0))
    if recompute:
        gw = qg[1].shape[1] - MEM_W
        qg_arrays = [qg[0].reshape(1, d), qg[1].astype(BF16)]
        qg_specs = [pl.BlockSpec((1, d), lambda i: (0, 0)), pl.BlockSpec((d, MEM_W + gw), lambda i: (0, 0))]
    else:
        gw = qg[1].shape[1]
        qg_arrays, qg_specs = list(qg), [rows(MEM_W), rows(gw)]
    nb = max(1, tm // rows_per_batch)
    per_batch = max(1, rows_per_batch // tm)
    kvt_spec = pl.BlockSpec((None, nb, 2 * MEM_W, N_MEM), lambda i: (layer, i // per_batch, 0, 0))
    if merge:
        mix_w = GROUP_W
        merge_dils = tuple(acc.shape[2] // GROUP_W for acc, _ in mix)
        tiles = mix[0][0].shape[1] * merge_dils[0] // tm
        mix_arrays = [a for pair in mix for a in pair]
        mix_specs = [pl.BlockSpec((None, tm // dil, a.shape[2]), lambda i: (i // tiles, i % tiles, 0))
                     for dil, pair in zip(merge_dils, mix) for a in pair]
    else:
        mix_w, merge_dils, mix_arrays, mix_specs = mix.shape[1], (), [mix], [rows(mix.shape[1])]
    return pl.pallas_call(
        functools.partial(_post_kernel, merge_dils=merge_dils, mix_w=mix_w, nb=nb, recompute=recompute),
        grid=(m // tm,),
        in_specs=[rows(d)] + qg_specs + [
                  kvt_spec,
                  pl.BlockSpec((gw, d), lambda i: (0, 0)),
                  pl.BlockSpec((1, d), lambda i: (0, 0))] + mix_specs,
        out_specs=rows(d),
        out_shape=jax.ShapeDtypeStruct((m, d), F32),
        scratch_shapes=[pltpu.VMEM((tm, 128), F32), pltpu.VMEM((tm, MEM_W), F32), pltpu.VMEM((tm, MEM_W), F32)],
        compiler_params=_params("parallel"),
        name="post",
    )(x, *qg_arrays, kvt, w_out.astype(BF16), g_post.reshape(1, d), *mix_arrays)


def _t5_bucket(dist):
    max_exact = N_BUCKETS // 2
    d = jnp.maximum(dist, 1).astype(F32)
    large = max_exact + (jnp.log(d / max_exact) / math.log(MAX_DISTANCE / max_exact)
                         * (N_BUCKETS - max_exact)).astype(jnp.int32)
    large = jnp.minimum(large, N_BUCKETS - 1)
    return jnp.where(dist < max_exact, dist, large)


def _group_bias(rel_bias, g):
    dist = DILATIONS[g] * jnp.arange(A_KEYS, dtype=jnp.int32)
    bias = rel_bias[_t5_bucket(dist)]
    return bias[:, g * GROUP_HEADS:(g + 1) * GROUP_HEADS].T.astype(F32)


def _bias_tiles_kernel(x_ref, o_ref):
    row = lax.broadcasted_iota(jnp.int32, (Q_TILE, 2 * Q_TILE), 0)
    for h in range(o_ref.shape[0]):
        t = jnp.broadcast_to(x_ref[h:h + 1, :], (Q_TILE, 2 * Q_TILE))
        for bit in range(int(math.log2(Q_TILE))):
            t = jnp.where((row >> bit) & 1 == 1, pltpu.roll(t, 1 << bit, axis=1), t)
        o_ref[h] = t


def _prompt_bias(biases):
    bias = jnp.concatenate(biases, axis=0)
    neg = jnp.full((bias.shape[0], Q_TILE - 1), NEG_INF, F32)
    table = jnp.concatenate([bias[:, :1], neg, bias[:, :0:-1]], axis=1)
    n = table.shape[0]
    return pl.pallas_call(
        _bias_tiles_kernel,
        out_shape=jax.ShapeDtypeStruct((n, Q_TILE, 2 * Q_TILE), F32),
        name="bias_tiles",
    )(table)


def _sample_bias(biases, n_new, tp):
    t = np.arange(tp)[:, None]
    real = t < n_new
    past = []
    for g, bias in enumerate(biases):
        w, dil = WINDOWS[g], DILATIONS[g]
        spread = jnp.concatenate([bias[:, :0:-1, None], jnp.full((GROUP_HEADS, w // dil, dil - 1), NEG_INF, F32)],
                                 axis=-1).reshape(GROUP_HEADS, w)
        rows = [jnp.concatenate([jnp.full((GROUP_HEADS, tt), NEG_INF, F32), spread[:, :w - tt]], axis=-1)
                for tt in range(n_new)] + [jnp.zeros((GROUP_HEADS, w), F32)] * (tp - n_new)
        past.append(jnp.stack(rows, axis=1))
    s = np.arange(n_new)[None, :, None]
    g_idx = np.arange(N_GROUPS)[:, None, None]
    ok = (s <= t[None, :, 0]) & real[None, :, 0] & ((g_idx == 0) | (s == 0))
    vals = jnp.stack([bias[:, :n_new].T for bias in biases])
    new = jnp.where(jnp.asarray(ok)[..., None], vals[:, :, None, :], NEG_INF)
    return past, jnp.pad(new, ((0, 0), (0, 0), (0, 0), (0, 128 - GROUP_HEADS)))


def _pieces_a():
    q = [((g * GROUP_W, (g + 1) * GROUP_W),) for g in range(N_GROUPS)]
    kv = [((A_Q + g * GROUP_W, A_Q + (g + 1) * GROUP_W), (2 * A_Q + g * GROUP_W, 2 * A_Q + (g + 1) * GROUP_W))
          for g in range(N_GROUPS)]
    qmem = ((3 * A_Q, 3 * A_Q + MEM_W),)
    gate = ((3 * A_Q + MEM_W, 3 * A_Q + MEM_W + GROUP_W + MEM_W),)
    return tuple(q + kv + [qmem, gate])


def _pieces_b():
    return (((0, C_SHIFT),),)


def kernel(x_prompt, x_sample, mem_prompt, cache_mem_kv, cache_win0, cache_win1, cache_win2, state_wkv, state_shift, norm_pre, norm_post, norm_mem, w_mem_kv, rel_bias, w_in_a, w_out_a, w_in_b, w_out_b, rwkv_mu, rwkv_w0, rwkv_w_up, rwkv_a0, rwkv_a_up, rwkv_k_k, rwkv_k_a, rwkv_r_k, rwkv_ln_w, rwkv_ln_b):
    bp, tp, d = x_prompt.shape
    bs, ts, _ = x_sample.shape
    tsp = SAMPLE_PAD_T
    xp = x_prompt.reshape(bp * tp, d)
    xs = jnp.pad(x_sample, ((0, 0), (0, tsp - ts), (0, 0))).reshape(bs * tsp, d)
    tm_p, tm_s = 512, bs * tsp
    tm_post_s = 8 * tsp
    time_minor = lambda c: jnp.moveaxis(c, -4, -1)
    caches = [time_minor(c[0]) for c in (cache_win0, cache_win1, cache_win2)]
    mkv_s = time_minor(cache_mem_kv).reshape(cache_mem_kv.shape[0], bs, 2 * MEM_W, N_MEM)
    mkv_p = _mem_kv(mem_prompt, norm_mem, w_mem_kv)

    biases = [_group_bias(rel_bias, g) for g in range(N_GROUPS)]
    w_b = w_in_b[0].astype(BF16)
    w_gq_b, w_mix_b = w_b[:, C_SHIFT:], w_b[:, :C_SHIFT]
    outs_p = _norm_proj(xp, norm_pre[0], w_in_a[0], _pieces_a(), tm_p, transposed=(3, 4, 5),
                        dils=DILATIONS + DILATIONS + (1, 1), rows_per_batch=tp)
    outs_s = _norm_proj(xs, norm_pre[0], w_in_a[0], _pieces_a(), tm_s)
    q_p, kv_p, qg_p, kvt_p = outs_p[0:3], outs_p[3:6], outs_p[6:8], outs_p[8:11]
    q_s, kv_s, qg_s = outs_s[0:3], outs_s[3:6], outs_s[6:8]

    bias_tiles = _prompt_bias(biases)
    merged = [_dil_attn(q_p[g].reshape(bp, tp // DILATIONS[g], -1), kv_p[g].reshape(bp, tp // DILATIONS[g], -1),
                        bias_tiles, g) for g in range(N_GROUPS)]
    xp1 = _post(xp, merged, qg_p, mkv_p, 0, w_out_a[0], norm_post[0], tm_p, tp, False)

    bias_past, bias_new = _sample_bias(biases, ts, tsp)
    q_s_all = jnp.concatenate(q_s, axis=-1).reshape(bs, tsp, A_Q)
    kvn = [a.reshape(bs, tsp, 2 * GROUP_W) for a in kv_s]
    o_s = _sample_attn(q_s_all, kvn, caches, bias_past, bias_new, ts)
    xs1 = _post(xs, o_s.reshape(bs * tsp, GROUP_W), qg_s, mkv_s, 0, w_out_a[0], norm_post[0], tm_post_s, tsp, False)

    (cols_p,) = _norm_proj(xp1, norm_pre[1], w_mix_b, _pieces_b(), 2 * tm_p)
    (cols_s,) = _norm_proj(xs1, norm_pre[1], w_mix_b, _pieces_b(), tm_s)
    prm = dict(mu=rwkv_mu[0], w0=rwkv_w0[0], w_up=rwkv_w_up[0], a0=rwkv_a0[0], a_up=rwkv_a_up[0], k_k=rwkv_k_k[0],
               k_a=rwkv_k_a[0], r_k=rwkv_r_k[0], ln_w=rwkv_ln_w[0], ln_b=rwkv_ln_b[0])
    y_p, wkv_p, sh_p = _rwkv_scan(cols_p.reshape(bp, tp, C_SHIFT), jnp.zeros((bp, C_SHIFT), F32),
                                  jnp.zeros((bp, 2 * RWKV_PAIRS, HEAD_DIM, HEAD_DIM), F32), prm, 512, tp, 1)
    y_s, wkv_s, sh_s = _rwkv_scan(cols_s.reshape(bs, tsp, C_SHIFT), state_shift[0], state_wkv[0], prm, SCAN_CHUNK,
                                  ts, 4)
    y_s = y_s.reshape(bs * tsp, RWKV_W)
    qg_b = (norm_pre[1], w_gq_b)
    xp2 = _post(xp1, y_p.reshape(bp * tp, RWKV_W), qg_b, mkv_p, 1, w_out_b[0], norm_post[1], tm_p, tp, True)
    xs2 = _post(xs1, y_s, qg_b, mkv_s, 1, w_out_b[0], norm_post[1], tm_post_s, tsp, True)

    kv_shape = (2, GROUP_HEADS, HEAD_DIM)
    time_major = lambda c: jnp.moveaxis(c, -1, -4)
    new_mem_kv = time_major(mkv_p.reshape(mkv_p.shape[0], bp, 2, MEM_W // HEAD_DIM, HEAD_DIM, N_MEM))
    win_p = [time_major(kvt_p[g].reshape(bp, *kv_shape, tp)[..., tp - min(WINDOWS[g], tp):])[None]
             for g in range(N_GROUPS)]
    win_s = [kv_s[g].reshape(bs, tsp, *kv_shape)[None, :, :ts] for g in range(N_GROUPS)]
    return (xp2.reshape(bp, tp, d), xs2.reshape(bs, tsp, d)[:, :ts], new_mem_kv,
            win_p[0], win_p[1], win_p[2], win_s[0], win_s[1], win_s[2],
            wkv_p[None], wkv_s[None], sh_p[None], sh_s[None])
```
